```python
import jax, jax.numpy as jnp
from jax import lax
import numpy as np

D_MODEL = 1024
BATCH = 1
SEQ = 16384
DEPTH = 4

D_MIX = D_MODEL
MLA_HEADS = 8
MLA_V_DIM = 64
MLA_NOPE_DIM = 64
MLA_ROPE_DIM = 32
MLA_QK_DIM = MLA_NOPE_DIM + MLA_ROPE_DIM
MLA_Q_RANK = 256
MLA_KV_RANK = 128
ROPE_THETA = 10000.0
Q_BLOCK = 128
MLSTM_HEADS = 4
MLSTM_HEAD_DIM = 128
D_MLSTM = MLSTM_HEADS * MLSTM_HEAD_DIM
CONV_WIDTH = 4
CHUNK = 64
N_GROUPS = 8
EXPERTS_PER_GROUP = 8
N_EXPERTS = N_GROUPS * EXPERTS_PER_GROUP
D_EXPERT = 256
TOP_K = 2
EXPERT_BLOCK = 128
DN_ALPHA = (2 * DEPTH) ** 0.25
DN_BETA = (8 * DEPTH) ** -0.25
LN_EPS = 1e-5
RMS_EPS = 1e-6
COL_SIZES = (MLA_Q_RANK, MLA_KV_RANK, MLA_ROPE_DIM, D_MLSTM, D_MLSTM, D_MLSTM, D_MLSTM, MLSTM_HEADS, MLSTM_HEADS)
D_IN = sum(COL_SIZES)

kernel_name = 'hymba_mla_mlstm_hmoe_deepnorm'


def _layer_norm(x, g, b):
    xf = x.astype(jnp.float32)
    mu = jnp.mean(xf, -1, keepdims=True)
    var = jnp.mean(jnp.square(xf - mu), -1, keepdims=True)
    return ((xf - mu) * lax.rsqrt(var + LN_EPS) * g + b).astype(x.dtype)


def _rms_norm(x, g):
    xf = x.astype(jnp.float32)
    return (xf * lax.rsqrt(jnp.mean(xf * xf, -1, keepdims=True) + RMS_EPS) * g).astype(x.dtype)


def _head_norm(h):
    hf = h.astype(jnp.float32)
    return (hf * lax.rsqrt(jnp.mean(hf * hf, -1, keepdims=True) + RMS_EPS)).astype(h.dtype)


def _rope(x, pos):
    half = MLA_ROPE_DIM // 2
    inv_freq = ROPE_THETA ** (-jnp.arange(half, dtype=jnp.float32) / half)
    ang = pos.astype(jnp.float32)[..., None] * inv_freq
    cos, sin = jnp.cos(ang), jnp.sin(ang)
    x1 = x[..., :half].astype(jnp.float32)
    x2 = x[..., half:].astype(jnp.float32)
    return jnp.concatenate([x1 * cos - x2 * sin, x2 * cos + x1 * sin], -1).astype(x.dtype)


def _mla(c_q, c_kv, k_rope, positions, q_a_g, kv_a_g, w_qb, w_kvb):
    B, S, _ = c_q.shape
    q = (_rms_norm(c_q, q_a_g) @ w_qb).reshape(B, S, MLA_HEADS, MLA_QK_DIM)
    q_nope = q[..., :MLA_NOPE_DIM]
    q_rope = _rope(q[..., MLA_NOPE_DIM:], positions[:, :, None])
    kv = (_rms_norm(c_kv, kv_a_g) @ w_kvb).reshape(B, S, MLA_HEADS, MLA_NOPE_DIM + MLA_V_DIM)
    k_nope, v = kv[..., :MLA_NOPE_DIM], kv[..., MLA_NOPE_DIM:]
    k_rope = _rope(k_rope, positions)
    n_blocks = S // Q_BLOCK

    def to_blocks(t):
        return t.reshape(B, n_blocks, Q_BLOCK, *t.shape[2:]).swapaxes(0, 1)

    key_idx = jnp.arange(S)
    scale = MLA_QK_DIM ** -0.5

    def attend(args):
        qn, qr, blk = args
        s = jnp.einsum('bqhd,bkhd->bhqk', qn, k_nope) + jnp.einsum('bqhr,bkr->bhqk', qr, k_rope)
        q_idx = blk * Q_BLOCK + jnp.arange(Q_BLOCK)
        s = jnp.where(key_idx[None, :] <= q_idx[:, None], s.astype(jnp.float32) * scale, -jnp.inf)
        p = jax.nn.softmax(s, axis=-1).astype(v.dtype)
        return jnp.einsum('bhqk,bkhd->bqhd', p, v)

    o = lax.map(attend, (to_blocks(q_nope), to_blocks(q_rope), jnp.arange(n_blocks)))
    return o.swapaxes(0, 1).reshape(B, S, MLA_HEADS, MLA_V_DIM)


def _causal_conv(x, w, b):
    S = x.shape[1]
    xp = jnp.pad(x, ((0, 0), (CONV_WIDTH - 1, 0), (0, 0)))
    return sum(xp[:, j:j + S] * w[j] for j in range(CONV_WIDTH)) + b


def _mlstm(q, k, v, i_pre, f_pre):
    B, S, H, dh = q.shape
    nc = S // CHUNK
    f32 = jnp.float32

    def vec_chunks(t):
        return t.astype(f32).transpose(0, 2, 1, 3).reshape(B, H, nc, CHUNK, dh).transpose(2, 0, 1, 3, 4)

    def gate_chunks(t):
        return t.astype(f32).transpose(0, 2, 1).reshape(B, H, nc, CHUNK).transpose(2, 0, 1, 3)

    xs = (vec_chunks(q) * dh ** -0.5, vec_chunks(k), vec_chunks(v),
          gate_chunks(i_pre),
          gate_chunks(jax.nn.log_sigmoid(f_pre.astype(f32))))
    causal = jnp.tril(jnp.ones((CHUNK, CHUNK), dtype=bool))

    def step(carry, inp):
        C, n, m = carry
        qc, kc, vc, log_i, log_f = inp
        b = jnp.cumsum(log_f, axis=-1)
        b_end = b[..., -1]
        log_d = jnp.where(causal, b[..., :, None] - b[..., None, :] + log_i[..., None, :], -jnp.inf)
        log_inter = b + m[..., None]
        m_t = jnp.maximum(log_inter, jnp.max(log_d, -1))
        w_intra = jnp.exp(log_d - m_t[..., None])
        w_inter = jnp.exp(log_inter - m_t)
        s = jnp.einsum('bhld,bhsd->bhls', qc, kc) * w_intra
        num = jnp.einsum('bhls,bhse->bhle', s, vc) + w_inter[..., None] * jnp.einsum('bhld,bhde->bhle', qc, C)
        den = jnp.sum(s, -1) + w_inter * jnp.einsum('bhld,bhd->bhl', qc, n)
        h = num / jnp.maximum(jnp.abs(den), jnp.exp(-m_t))[..., None]
        log_w = b_end[..., None] - b + log_i
        m_new = jnp.maximum(b_end + m, jnp.max(log_w, -1))
        w_s = jnp.exp(log_w - m_new[..., None])
        decay = jnp.exp(b_end + m - m_new)
        C = decay[..., None, None] * C + jnp.einsum('bhs,bhsd,bhse->bhde', w_s, kc, vc)
        n = decay[..., None] * n + jnp.einsum('bhs,bhsd->bhd', w_s, kc)
        return (C, n, m_new), h

    init = (jnp.zeros((B, H, dh, dh), f32), jnp.zeros((B, H, dh), f32), jnp.zeros((B, H), f32))
    _, h = lax.scan(step, init, xs)
    return h.transpose(1, 2, 0, 3, 4).reshape(B, H, S, dh).transpose(0, 2, 1, 3).astype(q.dtype)


def _moe(x, w_rg, w_re, w_gate, w_up, w_down):
    B, S, D = x.shape
    T = B * S
    xf = x.reshape(T, D)
    g_prob = jax.nn.softmax((xf @ w_rg).astype(jnp.float32), axis=-1)
    g_p, g_top = lax.top_k(g_prob, 1)
    e_logits = (xf @ w_re).astype(jnp.float32).reshape(T, N_GROUPS, EXPERTS_PER_GROUP)
    e_in_group = e_logits[jnp.arange(T), g_top[:, 0]]
    e_p, e_top = lax.top_k(jax.nn.softmax(e_in_group, axis=-1), TOP_K)
    e_p = e_p / jnp.sum(e_p, -1, keepdims=True)
    weights = g_p * e_p
    expert_id = g_top * EXPERTS_PER_GROUP + e_top
    n_assign = T * TOP_K
    flat_e = expert_id.reshape(-1)
    flat_tok = jnp.repeat(jnp.arange(T, dtype=jnp.int32), TOP_K)
    flat_w = weights.reshape(-1)
    order = jnp.argsort(flat_e)
    se, stok, sw = flat_e[order], flat_tok[order], flat_w[order]
    counts = jnp.zeros((N_EXPERTS,), jnp.int32).at[flat_e].add(1)
    starts = jnp.cumsum(counts) - counts
    padded = (counts + EXPERT_BLOCK - 1) // EXPERT_BLOCK * EXPERT_BLOCK
    pad_ends = jnp.cumsum(padded)
    pad_starts = pad_ends - padded
    dest = pad_starts[se] + jnp.arange(n_assign, dtype=jnp.int32) - starts[se]
    n_blocks = -(-n_assign // EXPERT_BLOCK) + N_EXPERTS
    n_slots = n_blocks * EXPERT_BLOCK
    slot_tok = jnp.full((n_slots,), T, jnp.int32).at[dest].set(stok)
    slot_w = jnp.zeros((n_slots,), jnp.float32).at[dest].set(sw)
    block_start = jnp.arange(n_blocks, dtype=jnp.int32) * EXPERT_BLOCK
    block_expert = jnp.minimum(jnp.searchsorted(pad_ends, block_start, side='right'), N_EXPERTS - 1)
    x_pad = jnp.concatenate([xf, jnp.zeros((1, D), xf.dtype)], 0)
    xb = x_pad[slot_tok].reshape(n_blocks, EXPERT_BLOCK, D)

    def expert_block(args):
        xs, e = args
        h = jax.nn.silu(xs @ w_gate[e]) * (xs @ w_up[e])
        return h @ w_down[e]

    yb = lax.map(expert_block, (xb, block_expert)).reshape(n_slots, D)
    yb = yb * slot_w[:, None].astype(yb.dtype)
    out = jnp.zeros((T + 1, D), yb.dtype).at[slot_tok].add(yb)[:T]
    return out.reshape(B, S, D).astype(x.dtype)


def _to_heads(t, n_heads):
    return t.reshape(t.shape[0], t.shape[1], n_heads, t.shape[2] // n_heads)


def setup_inputs(seed: int = 0) -> dict:
    key = jax.random.key(seed)
    ks = jax.random.split(key, 24)
    f32 = jnp.float32

    def nrm(k, shape, scale):
        return jax.random.normal(k, shape, f32) * scale

    x = nrm(ks[0], (BATCH, SEQ, D_MODEL), 1.0)
    positions = jnp.arange(SEQ, dtype=jnp.int32)[None, :] + jax.random.randint(ks[1], (BATCH, 1), 0, 1024, jnp.int32)
    w_in = nrm(ks[2], (DEPTH, D_MODEL, D_IN), D_MODEL ** -0.5)
    conv_w = nrm(ks[3], (DEPTH, CONV_WIDTH, 2 * D_MLSTM), CONV_WIDTH ** -0.5)
    conv_b = nrm(ks[4], (DEPTH, 2 * D_MLSTM), 0.01)
    i_bias = nrm(ks[5], (DEPTH, MLSTM_HEADS), 0.1)
    f_bias = jnp.linspace(3.0, 6.0, MLSTM_HEADS, dtype=f32)[None, :] + nrm(ks[6], (DEPTH, MLSTM_HEADS), 0.1)
    gate_b = jnp.concatenate([i_bias, f_bias], -1)
    q_a_g = 1.0 + nrm(ks[7], (DEPTH, MLA_Q_RANK), 0.02)
    kv_a_g = 1.0 + nrm(ks[8], (DEPTH, MLA_KV_RANK), 0.02)
    w_qb = nrm(ks[9], (DEPTH, MLA_Q_RANK, MLA_HEADS * MLA_QK_DIM), MLA_Q_RANK ** -0.5)
    w_kvb = nrm(ks[10], (DEPTH, MLA_KV_RANK, MLA_HEADS * (MLA_NOPE_DIM + MLA_V_DIM)), MLA_KV_RANK ** -0.5)
    out_g = 1.0 + nrm(ks[11], (DEPTH, D_MIX), 0.02)
    w_out = nrm(ks[12], (DEPTH, D_MIX, D_MODEL), D_MIX ** -0.5 * DN_BETA)
    ln1_g = 1.0 + nrm(ks[13], (DEPTH, D_MODEL), 0.02)
    ln1_b = nrm(ks[14], (DEPTH, D_MODEL), 0.02)
    w_rg = nrm(ks[15], (DEPTH, D_MODEL, N_GROUPS), D_MODEL ** -0.5)
    w_re = nrm(ks[16], (DEPTH, D_MODEL, N_EXPERTS), D_MODEL ** -0.5)
    w_gate = nrm(ks[17], (DEPTH, N_EXPERTS, D_MODEL, D_EXPERT), D_MODEL ** -0.5)
    w_up = nrm(ks[18], (DEPTH, N_EXPERTS, D_MODEL, D_EXPERT), D_MODEL ** -0.5 * DN_BETA)
    w_down = nrm(ks[19], (DEPTH, N_EXPERTS, D_EXPERT, D_MODEL), D_EXPERT ** -0.5 * DN_BETA)
    ln2_g = 1.0 + nrm(ks[20], (DEPTH, D_MODEL), 0.02)
    ln2_b = nrm(ks[21], (DEPTH, D_MODEL), 0.02)
    return {'x': x, 'positions': positions, 'w_in': w_in, 'conv_w': conv_w, 'conv_b': conv_b,
            'gate_b': gate_b, 'q_a_g': q_a_g, 'kv_a_g': kv_a_g, 'w_qb': w_qb, 'w_kvb': w_kvb,
            'out_g': out_g, 'w_out': w_out, 'ln1_g': ln1_g, 'ln1_b': ln1_b, 'w_rg': w_rg, 'w_re': w_re,
            'w_gate': w_gate, 'w_up': w_up, 'w_down': w_down, 'ln2_g': ln2_g, 'ln2_b': ln2_b}


def reference(x, positions, w_in, conv_w, conv_b, gate_b, q_a_g, kv_a_g, w_qb, w_kvb, out_g, w_out,
              ln1_g, ln1_b, w_rg, w_re, w_gate, w_up, w_down, ln2_g, ln2_b):
    B, S, _ = x.shape
    splits = [int(c) for c in np.cumsum(COL_SIZES)[:-1]]
    for l in range(DEPTH):
        proj = x @ w_in[l]
        c_q, c_kv, k_rope, m_q, m_k, m_v, m_o, m_i, m_f = jnp.split(proj, splits, axis=-1)
        a = _mla(c_q, c_kv, k_rope, positions, q_a_g[l], kv_a_g[l], w_qb[l], w_kvb[l])
        qk = jax.nn.silu(_causal_conv(jnp.concatenate([m_q, m_k], -1), conv_w[l], conv_b[l]))
        h = _mlstm(_to_heads(qk[..., :D_MLSTM], MLSTM_HEADS), _to_heads(qk[..., D_MLSTM:], MLSTM_HEADS),
                   _to_heads(m_v, MLSTM_HEADS),
                   m_i + gate_b[l, :MLSTM_HEADS], m_f + gate_b[l, MLSTM_HEADS:])
        h = _head_norm(h) * jax.nn.sigmoid(_to_heads(m_o, MLSTM_HEADS))
        mix = jnp.concatenate([_head_norm(a).reshape(B, S, MLA_HEADS * MLA_V_DIM),
                               h.reshape(B, S, D_MLSTM)], -1) * out_g[l]
        x = _layer_norm(DN_ALPHA * x + mix @ w_out[l], ln1_g[l], ln1_b[l])
        y = _moe(x, w_rg[l], w_re[l], w_gate[l], w_up[l], w_down[l])
        x = _layer_norm(DN_ALPHA * x + y, ln2_g[l], ln2_b[l])
    return x
```

```python
import functools

import numpy as np
import jax
import jax.numpy as jnp
from jax import lax
from jax.experimental import pallas as pl
from jax.experimental.pallas import tpu as pltpu

F32 = jnp.float32
BF16 = jnp.bfloat16

D_MODEL = 1024
MLA_HEADS = 8
MLA_V_DIM = 64
MLA_NOPE_DIM = 64
MLA_ROPE_DIM = 32
MLA_QK_DIM = MLA_NOPE_DIM + MLA_ROPE_DIM
MLA_Q_RANK = 256
MLA_KV_RANK = 128
ROPE_THETA = 10000.0
MLSTM_HEADS = 4
MLSTM_HEAD_DIM = 128
D_MLSTM = MLSTM_HEADS * MLSTM_HEAD_DIM
CONV_WIDTH = 4
N_GROUPS = 8
EXPERTS_PER_GROUP = 8
N_EXPERTS = N_GROUPS * EXPERTS_PER_GROUP
D_EXPERT = 256
TOP_K = 2
EXPERT_BLOCK = 128
LN_EPS = 1e-5
RMS_EPS = 1e-6

LANES = 128
SUBLANES = 8
HEAD_PAD = 128
VMEM_LIMIT = 56 * 1024 * 1024

_C_Q0 = 0
_C_KV0 = MLA_Q_RANK
_C_MQK0 = _C_KV0 + MLA_KV_RANK
_C_MV0 = _C_MQK0 + 2 * D_MLSTM
_C_MO0 = _C_MV0 + D_MLSTM
_C_SMALL0 = _C_MO0 + D_MLSTM
D_IN_PAD = _C_SMALL0 + LANES
_S_I0 = 2 * MLA_ROPE_DIM
_S_F0 = _S_I0 + MLSTM_HEADS
_S_ONE = MLA_ROPE_DIM


def _cparams(sem, vmem=VMEM_LIMIT):
    return pltpu.CompilerParams(dimension_semantics=sem, vmem_limit_bytes=vmem)


def _proj_kernel(x_ref, win_ref, wq_ref, wkv_ref, qg_ref, kvg_ref, cw_ref, cb_ref, gb_ref,
                 tabk_ref, cq_ref, sq_ref,
                 q_out, k_out, v_out, mq_out, mk_out, mv_out, mo_out, g_out, carry_ref, *, tm):
    i = pl.program_id(0)

    @pl.when(i == 0)
    def _():
        carry_ref[...] = jnp.zeros_like(carry_ref)

    xb = x_ref[...].astype(BF16)
    p = jnp.dot(xb, win_ref[...], preferred_element_type=F32)

    def rms(v, g):
        return v * lax.rsqrt(jnp.mean(v * v, axis=-1, keepdims=True) + RMS_EPS) * g

    cqn = rms(p[:, _C_Q0:_C_KV0], qg_ref[...]).astype(BF16)
    ckvn = rms(p[:, _C_KV0:_C_MQK0], kvg_ref[...]).astype(BF16)

    qq = jnp.dot(cqn, wq_ref[...], preferred_element_type=F32)
    cq = cq_ref[...]
    sq = sq_ref[...]
    hw = MLA_HEADS * HEAD_PAD
    for h in range(MLA_HEADS):
        a = qq[:, h * HEAD_PAD:(h + 1) * HEAD_PAD]
        b = qq[:, hw + h * HEAD_PAD: hw + (h + 1) * HEAD_PAD]
        q_out[:, h * HEAD_PAD:(h + 1) * HEAD_PAD] = (a * cq + b * sq).astype(BF16)

    small = p[:, _C_SMALL0:D_IN_PAD]
    lane = lax.broadcasted_iota(jnp.int32, small.shape, 1)
    prod = small * tabk_ref[...]
    kr = prod + pltpu.roll(prod, LANES - MLA_ROPE_DIM, 1)
    kvs = jnp.where(lane < MLA_ROPE_DIM, kr, jnp.where(lane == _S_ONE, 1.0, 0.0))
    kvin = jnp.concatenate([ckvn, kvs.astype(BF16)], axis=1)
    kv = jnp.dot(kvin, wkv_ref[...], preferred_element_type=F32)
    k_out[...] = kv[:, :hw].astype(BF16)
    v_out[...] = kv[:, hw:].astype(BF16)

    g = small + gb_ref[...]
    lsig = jnp.minimum(g, 0.0) - jnp.log1p(jnp.exp(-jnp.abs(g)))
    is_f = (lane >= _S_F0) & (lane < _S_F0 + MLSTM_HEADS)
    g_out[...] = jnp.where(is_f, lsig, g)

    mqk = p[:, _C_MQK0:_C_MV0]
    ext = jnp.concatenate([carry_ref[...], mqk], axis=0)
    carry_ref[...] = mqk[tm - SUBLANES:tm, :]
    acc = cb_ref[...] + mqk * cw_ref[CONV_WIDTH - 1:CONV_WIDTH, :]
    for j in range(CONV_WIDTH - 1):
        off = SUBLANES - (CONV_WIDTH - 1) + j
        acc = acc + ext[off:off + tm, :] * cw_ref[j:j + 1, :]
    y = acc * (1.0 / (1.0 + jnp.exp(-acc)))
    mq_out[...] = (y[:, :D_MLSTM] * (MLSTM_HEAD_DIM ** -0.5)).astype(BF16)
    mk_out[...] = y[:, D_MLSTM:].astype(BF16)
    mv_out[...] = p[:, _C_MV0:_C_MO0].astype(BF16)
    mo = p[:, _C_MO0:_C_SMALL0]
    mo_out[...] = (1.0 / (1.0 + jnp.exp(-mo))).astype(BF16)


def _proj(x, win, wq, wkv, qg, kvg, cw, cb, gb, tabk, cqt, sqt, *, tm):
    S = x.shape[0]
    hw = MLA_HEADS * HEAD_PAD
    row = lambda w: pl.BlockSpec((tm, w), lambda i: (i, 0))
    full = lambda a: pl.BlockSpec(a.shape, lambda i: (0,) * a.ndim)
    out_shapes = (
        jax.ShapeDtypeStruct((S, hw), BF16), jax.ShapeDtypeStruct((S, hw), BF16),
        jax.ShapeDtypeStruct((S, hw), BF16),
        jax.ShapeDtypeStruct((S, D_MLSTM), BF16), jax.ShapeDtypeStruct((S, D_MLSTM), BF16),
        jax.ShapeDtypeStruct((S, D_MLSTM), BF16), jax.ShapeDtypeStruct((S, D_MLSTM), BF16),
        jax.ShapeDtypeStruct((S, LANES), F32))
    return pl.pallas_call(
        functools.partial(_proj_kernel, tm=tm),
        grid=(S // tm,),
        in_specs=[row(D_MODEL), full(win), full(wq), full(wkv), full(qg), full(kvg), full(cw),
                  full(cb), full(gb), row(LANES), row(LANES), row(LANES)],
        out_specs=(row(hw), row(hw), row(hw), row(D_MLSTM), row(D_MLSTM), row(D_MLSTM),
                   row(D_MLSTM), row(LANES)),
        out_shape=out_shapes,
        scratch_shapes=[pltpu.VMEM((SUBLANES, 2 * D_MLSTM), F32)],
        compiler_params=_cparams(("arbitrary",)),
        name="proj",
    )(x, win, wq, wkv, qg, kvg, cw, cb, gb, tabk, cqt, sqt)


def _attn_kernel(q_ref, k_ref, v_ref, og_ref, o_ref, *, bq, heads_per_step):
    qi = pl.program_id(1)
    outs = []
    for hh in range(heads_per_step):
        c0, c1 = hh * HEAD_PAD, (hh + 1) * HEAD_PAD
        q = q_ref[:, c0:c1]

        def scores(j):
            kt = k_ref[pl.ds(pl.multiple_of(j * bq, bq), bq), c0:c1]
            return lax.dot_general(q, kt, (((1,), (1,)), ((), ())), preferred_element_type=F32)

        def update(j, s, m, acc):
            vt = v_ref[pl.ds(pl.multiple_of(j * bq, bq), bq), c0:c1]
            m_new = jnp.maximum(m, jnp.max(s, axis=-1, keepdims=True))
            alpha = jnp.exp(m - m_new)
            p = jnp.exp(s - m_new)
            acc = alpha * acc + jnp.dot(p.astype(BF16), vt, preferred_element_type=F32)
            return m_new, acc

        def body(j, carry):
            m, acc = carry
            return update(j, scores(j), m, acc)

        m0 = jnp.full((bq, 1), -jnp.inf, F32)
        acc0 = jnp.zeros((bq, HEAD_PAD), F32)
        m, acc = lax.fori_loop(0, qi, body, (m0, acc0))
        s = scores(qi)
        r = lax.broadcasted_iota(jnp.int32, (bq, bq), 0)
        c = lax.broadcasted_iota(jnp.int32, (bq, bq), 1)
        s = jnp.where(c <= r, s, -jnp.inf)
        m, acc = update(qi, s, m, acc)
        o = acc[:, :MLA_V_DIM] / acc[:, MLA_V_DIM:MLA_V_DIM + 1]
        o = o * lax.rsqrt(jnp.mean(o * o, axis=-1, keepdims=True) + RMS_EPS)
        outs.append(o)
    o_ref[...] = (jnp.concatenate(outs, axis=1) * og_ref[...]).astype(BF16)


def _attention(q, k, v, og, *, bq, heads_per_step=2):
    S = q.shape[0]
    n_hp = MLA_HEADS // heads_per_step
    wq = heads_per_step * HEAD_PAD
    wo = heads_per_step * MLA_V_DIM
    return pl.pallas_call(
        functools.partial(_attn_kernel, bq=bq, heads_per_step=heads_per_step),
        grid=(n_hp, S // bq),
        in_specs=[pl.BlockSpec((bq, wq), lambda h, i: (i, h)),
                  pl.BlockSpec((S, wq), lambda h, i: (0, h)),
                  pl.BlockSpec((S, wq), lambda h, i: (0, h)),
                  pl.BlockSpec((1, wo), lambda h, i: (0, h))],
        out_specs=pl.BlockSpec((bq, wo), lambda h, i: (i, h)),
        out_shape=jax.ShapeDtypeStruct((S, MLA_HEADS * MLA_V_DIM), BF16),
        compiler_params=_cparams(("arbitrary", "arbitrary")),
        name="attention",
    )(q, k, v, og)


def _mlstm_kernel(q_ref, k_ref, v_ref, o_ref, gc_ref, gr_ref, og_ref, h_out, c_ref, m_ref, *, L):
    ci = pl.program_id(0)

    @pl.when(ci == 0)
    def _():
        c_ref[...] = jnp.zeros_like(c_ref)
        m_ref[...] = jnp.zeros_like(m_ref)

    r = lax.broadcasted_iota(jnp.int32, (L, L), 0)
    c = lax.broadcasted_iota(jnp.int32, (L, L), 1)
    tri = c <= r
    lane = lax.broadcasted_iota(jnp.int32, (L, LANES), 1)
    ones_blk = jnp.where(lane == 0, 1.0, 0.0).astype(BF16)
    gc = gc_ref[...]
    gr = gr_ref[...]
    d = MLSTM_HEAD_DIM
    for h in range(MLSTM_HEADS):
        q = q_ref[:, h * d:(h + 1) * d]
        k = k_ref[:, h * d:(h + 1) * d]
        v = v_ref[:, h * d:(h + 1) * d]
        li_col = gc[:, h:h + 1]
        lf_col = gc[:, MLSTM_HEADS + h:MLSTM_HEADS + h + 1]
        li_row = gr[h:h + 1, :]
        lf_row = gr[MLSTM_HEADS + h:MLSTM_HEADS + h + 1, :]
        b_col = jnp.sum(jnp.where(tri, lf_row, 0.0), axis=1, keepdims=True)
        b_row = jnp.sum(jnp.where(r <= c, lf_col, 0.0), axis=0, keepdims=True)
        m_prev = m_ref[h][0:1, 0:1]
        log_d = jnp.where(tri, b_col - b_row + li_row, -jnp.inf)
        log_inter = b_col + m_prev
        m_t = jnp.maximum(log_inter, jnp.max(log_d, axis=1, keepdims=True))
        w_intra = jnp.exp(log_d - m_t)
        w_inter = jnp.exp(log_inter - m_t)
        s = lax.dot_general(q, k, (((1,), (1,)), ((), ())), preferred_element_type=F32) * w_intra
        vaug = jnp.concatenate([v, ones_blk], axis=1)
        r1 = jnp.dot(s.astype(BF16), vaug, preferred_element_type=F32)
        cst = c_ref[h]
        r2 = jnp.dot(q, cst.astype(BF16), preferred_element_type=F32)
        num = r1[:, :d] + w_inter * r2[:, :d]
        den = r1[:, d:d + 1] + w_inter * r2[:, d:d + 1]
        hv = num / jnp.maximum(jnp.abs(den), jnp.exp(-m_t))
        b_end = b_col[L - 1:L, :]
        log_w = b_end - b_col + li_col
        m_new = jnp.maximum(b_end + m_prev, jnp.max(log_w, axis=0, keepdims=True))
        w_s = jnp.exp(log_w - m_new)
        decay = jnp.exp(b_end + m_prev - m_new)
        kw = (k.astype(F32) * w_s).astype(BF16)
        upd = lax.dot_general(kw, vaug, (((0,), (0,)), ((), ())), preferred_element_type=F32)
        c_ref[h] = decay * cst + upd
        m_ref[h] = jnp.broadcast_to(m_new, (SUBLANES, LANES))
        hn = hv * lax.rsqrt(jnp.mean(hv * hv, axis=-1, keepdims=True) + RMS_EPS)
        gate = o_ref[:, h * d:(h + 1) * d].astype(F32)
        h_out[:, h * d:(h + 1) * d] = (hn * gate * og_ref[:, h * d:(h + 1) * d]).astype(BF16)


def _mlstm(mq, mk, mv, mo, gcol, grow, og, *, L):
    S = mq.shape[0]
    d = MLSTM_HEAD_DIM
    row = pl.BlockSpec((L, D_MLSTM), lambda i: (i, 0))
    return pl.pallas_call(
        functools.partial(_mlstm_kernel, L=L),
        grid=(S // L,),
        in_specs=[row, row, row, row,
                  pl.BlockSpec((L, 2 * MLSTM_HEADS), lambda i: (i, 0)),
                  pl.BlockSpec((2 * MLSTM_HEADS, L), lambda i: (0, i)),
                  pl.BlockSpec((1, D_MLSTM), lambda i: (0, 0))],
        out_specs=row,
        out_shape=jax.ShapeDtypeStruct((S, D_MLSTM), BF16),
        scratch_shapes=[pltpu.VMEM((MLSTM_HEADS, d, 2 * d), F32),
                        pltpu.VMEM((MLSTM_HEADS, SUBLANES, LANES), F32)],
        compiler_params=_cparams(("arbitrary",)),
        name="mlstm",
    )(mq, mk, mv, mo, gcol, grow, og)


def _layer_norm(z, g, b):
    mu = jnp.mean(z, axis=-1, keepdims=True)
    zc = z - mu
    var = jnp.mean(zc * zc, axis=-1, keepdims=True)
    return zc * lax.rsqrt(var + LN_EPS) * g + b


def _outproj_kernel(x_ref, a_ref, h_ref, wo_ref, g_ref, b_ref, wr_ref, x1_out, lg_out, *, alpha):
    mix = jnp.concatenate([a_ref[...], h_ref[...]], axis=1)
    y = jnp.dot(mix, wo_ref[...], preferred_element_type=F32)
    x1 = _layer_norm(alpha * x_ref[...] + y, g_ref[...], b_ref[...])
    x1_out[...] = x1
    lg_out[...] = jnp.dot(x1, wr_ref[...], preferred_element_type=F32,
                          precision=lax.Precision.HIGHEST)


def _outproj(x, a, hm, wo, g, b, wr, *, tm, alpha):
    S = x.shape[0]
    row = lambda w: pl.BlockSpec((tm, w), lambda i: (i, 0))
    full = lambda arr: pl.BlockSpec(arr.shape, lambda i: (0,) * arr.ndim)
    return pl.pallas_call(
        functools.partial(_outproj_kernel, alpha=alpha),
        grid=(S // tm,),
        in_specs=[row(D_MODEL), row(MLA_HEADS * MLA_V_DIM), row(D_MLSTM), full(wo), full(g), full(b),
                  full(wr)],
        out_specs=(row(D_MODEL), row(LANES)),
        out_shape=(jax.ShapeDtypeStruct((S, D_MODEL), F32), jax.ShapeDtypeStruct((S, LANES), F32)),
        compiler_params=_cparams(("arbitrary",)),
        name="outproj",
    )(x, a, hm, wo, g, b, wr)


def _moe_kernel(bexp_ref, nval_ref, tok_ref, tokn_ref, dst_ref, x_hbm, sw_ref, wg_ref, wu_ref, wd_ref,
                y_hbm, xbuf, ybuf, gsem, ssem):
    b = pl.program_id(0)
    nb = pl.num_programs(0)
    slot = lax.rem(b, 2)
    R = EXPERT_BLOCK

    def gather_copy(tok, r, sl):
        return pltpu.make_async_copy(x_hbm.at[pl.ds(tok, 1)], xbuf.at[sl, pl.ds(r, 1)], gsem.at[sl])

    def scatter_copy(dst, r, sl):
        return pltpu.make_async_copy(ybuf.at[sl, pl.ds(r, 1)], y_hbm.at[pl.ds(dst, 1)], ssem.at[sl])

    def gather_start(idx_ref, sl):
        def body(r, _):
            gather_copy(jnp.maximum(idx_ref[0, 0, r], 0), r, sl).start()
            return 0
        lax.fori_loop(0, R, body, 0)

    def gather_wait(sl):
        def body(r, _):
            gather_copy(0, r, sl).wait()
            return 0
        lax.fori_loop(0, R, body, 0)

    def scatter_wait(n, sl):
        def body(r, _):
            scatter_copy(0, r, sl).wait()
            return 0
        lax.fori_loop(0, n, body, 0)

    @pl.when(b == 0)
    def _():
        gather_start(tok_ref, 0)

    @pl.when(b + 1 < nb)
    def _():
        gather_start(tokn_ref, 1 - slot)

    gather_wait(slot)

    @pl.when(b >= 2)
    def _():
        scatter_wait(nval_ref[jnp.maximum(b - 2, 0)], slot)

    nv = nval_ref[b]

    @pl.when(nv > 0)
    def _():
        xs = xbuf[slot].astype(BF16)
        g = jnp.dot(xs, wg_ref[0].astype(BF16), preferred_element_type=F32)
        u = jnp.dot(xs, wu_ref[0].astype(BF16), preferred_element_type=F32)
        hid = (g * (1.0 / (1.0 + jnp.exp(-g)))) * u
        y = jnp.dot(hid.astype(BF16), wd_ref[0].astype(BF16), preferred_element_type=F32)
        ybuf[slot] = y * sw_ref[...]

    def sbody(r, _):
        scatter_copy(dst_ref[0, 0, r], r, slot).start()
        return 0
    lax.fori_loop(0, nv, sbody, 0)

    @pl.when(b == nb - 1)
    def _():
        @pl.when(nb >= 2)
        def _():
            scatter_wait(nval_ref[jnp.maximum(b - 1, 0)], 1 - slot)
        scatter_wait(nv, slot)


def _moe(x1, block_expert, nvalid, slot_tok, slot_dst, slot_w, wg, wu, wd):
    T = x1.shape[0]
    nb = block_expert.shape[0]
    R = EXPERT_BLOCK
    smem_blk = lambda f: pl.BlockSpec((1, 1, R), f, memory_space=pltpu.SMEM)
    wspec = lambda shp: pl.BlockSpec((1,) + shp, lambda b, be, nv: (be[b], 0, 0))
    grid_spec = pltpu.PrefetchScalarGridSpec(
        num_scalar_prefetch=2,
        grid=(nb,),
        in_specs=[smem_blk(lambda b, be, nv: (b, 0, 0)),
                  smem_blk(lambda b, be, nv: (jnp.minimum(b + 1, nb - 1), 0, 0)),
                  smem_blk(lambda b, be, nv: (b, 0, 0)),
                  pl.BlockSpec(memory_space=pl.ANY),
                  pl.BlockSpec((R, 1), lambda b, be, nv: (b, 0)),
                  wspec((D_MODEL, D_EXPERT)), wspec((D_MODEL, D_EXPERT)), wspec((D_EXPERT, D_MODEL))],
        out_specs=pl.BlockSpec(memory_space=pl.ANY),
        scratch_shapes=[pltpu.VMEM((2, R, D_MODEL), F32), pltpu.VMEM((2, R, D_MODEL), F32),
                        pltpu.SemaphoreType.DMA((2,)), pltpu.SemaphoreType.DMA((2,))])
    return pl.pallas_call(
        _moe_kernel,
        grid_spec=grid_spec,
        out_shape=jax.ShapeDtypeStruct((TOP_K * T, D_MODEL), F32),
        compiler_params=_cparams(("arbitrary",)),
        name="experts",
    )(block_expert, nvalid, slot_tok, slot_tok, slot_dst, x1, slot_w, wg, wu, wd)


def _combine_kernel(x_ref, y0_ref, y1_ref, g_ref, b_ref, o_ref, *, alpha):
    z = alpha * x_ref[...] + (y0_ref[...] + y1_ref[...])
    o_ref[...] = _layer_norm(z, g_ref[...], b_ref[...])


def _combine(x1, y, g, b, *, tm, alpha):
    S = x1.shape[0]
    nt = S // tm
    return pl.pallas_call(
        functools.partial(_combine_kernel, alpha=alpha),
        grid=(nt,),
        in_specs=[pl.BlockSpec((tm, D_MODEL), lambda i: (i, 0)),
                  pl.BlockSpec((tm, D_MODEL), lambda i: (i, 0)),
                  pl.BlockSpec((tm, D_MODEL), lambda i: (i + nt, 0)),
                  pl.BlockSpec((1, D_MODEL), lambda i: (0, 0)),
                  pl.BlockSpec((1, D_MODEL), lambda i: (0, 0))],
        out_specs=pl.BlockSpec((tm, D_MODEL), lambda i: (i, 0)),
        out_shape=jax.ShapeDtypeStruct((S, D_MODEL), F32),
        compiler_params=_cparams(("arbitrary",)),
        name="combine",
    )(x1, y, y, g, b)


def _route(logits, T):
    g_logits = logits[:, :N_GROUPS]
    g_prob = jax.nn.softmax(g_logits, axis=-1)
    g_top = jnp.argmax(g_prob, axis=-1)
    g_p = jnp.max(g_prob, axis=-1)
    e_logits = logits[:, N_GROUPS:N_GROUPS + N_EXPERTS].reshape(T, N_GROUPS, EXPERTS_PER_GROUP)
    e_in = jnp.take_along_axis(e_logits, g_top[:, None, None], axis=1)[:, 0, :]
    e_p, e_top = lax.top_k(jax.nn.softmax(e_in, axis=-1), TOP_K)
    e_p = e_p / jnp.sum(e_p, -1, keepdims=True)
    weights = (g_p[:, None] * e_p).reshape(-1)
    flat_e = (g_top[:, None] * EXPERTS_PER_GROUP + e_top).astype(jnp.int32).reshape(-1)
    n_assign = T * TOP_K
    onehot = (flat_e[:, None] == jnp.arange(N_EXPERTS, dtype=jnp.int32)[None, :]).astype(jnp.int32)
    csum = jnp.cumsum(onehot, axis=0)
    rank = jnp.take_along_axis(csum, flat_e[:, None], axis=1)[:, 0] - 1
    counts = csum[-1]
    padded = (counts + EXPERT_BLOCK - 1) // EXPERT_BLOCK * EXPERT_BLOCK
    pad_ends = jnp.cumsum(padded)
    pad_starts = pad_ends - padded
    dest = pad_starts[flat_e] + rank
    n_blocks = n_assign // EXPERT_BLOCK + N_EXPERTS
    n_slots = n_blocks * EXPERT_BLOCK
    a_idx = jnp.arange(n_assign, dtype=jnp.int32)
    tok = a_idx // TOP_K
    dst_row = (a_idx % TOP_K) * T + tok
    slot_tok = jnp.full((n_slots,), -1, jnp.int32).at[dest].set(tok)
    slot_dst = jnp.full((n_slots,), -1, jnp.int32).at[dest].set(dst_row)
    slot_w = jnp.zeros((n_slots,), F32).at[dest].set(weights)
    block_start = jnp.arange(n_blocks, dtype=jnp.int32) * EXPERT_BLOCK
    block_expert = jnp.minimum(jnp.searchsorted(pad_ends, block_start, side='right'),
                               N_EXPERTS - 1).astype(jnp.int32)
    nvalid = jnp.clip(pad_starts[block_expert] + counts[block_expert] - block_start, 0,
                      EXPERT_BLOCK).astype(jnp.int32)
    nvalid = jnp.where(block_start < pad_ends[-1], nvalid, 0)
    return (block_expert, nvalid, slot_tok.reshape(n_blocks, 1, EXPERT_BLOCK),
            slot_dst.reshape(n_blocks, 1, EXPERT_BLOCK), slot_w.reshape(n_slots, 1))


def _perm_in_cols():
    cq0, ckv0, kr0 = 0, MLA_Q_RANK, MLA_Q_RANK + MLA_KV_RANK
    mq0 = kr0 + MLA_ROPE_DIM
    i0 = mq0 + 4 * D_MLSTM
    f0 = i0 + MLSTM_HEADS
    zero = f0 + MLSTM_HEADS
    half = MLA_ROPE_DIM // 2
    cols = list(range(cq0, kr0)) + list(range(mq0, i0))
    small = (list(range(kr0, kr0 + MLA_ROPE_DIM))
             + list(range(kr0 + half, kr0 + MLA_ROPE_DIM)) + list(range(kr0, kr0 + half))
             + list(range(i0, i0 + MLSTM_HEADS)) + list(range(f0, f0 + MLSTM_HEADS)))
    small = small + [zero] * (LANES - len(small))
    return np.asarray(cols + small, np.int32)


def _q_cols():
    zero = MLA_HEADS * MLA_QK_DIM
    half = MLA_ROPE_DIM // 2
    main, swp = [], []
    for h in range(MLA_HEADS):
        base = h * MLA_QK_DIM
        rope0 = base + MLA_NOPE_DIM
        pad = [zero] * (HEAD_PAD - MLA_QK_DIM)
        main += list(range(base, base + MLA_QK_DIM)) + pad
        swp += ([zero] * MLA_NOPE_DIM + list(range(rope0 + half, rope0 + MLA_ROPE_DIM))
                + list(range(rope0, rope0 + half)) + pad)
    return np.asarray(main + swp, np.int32)


def _kv_weight(w_kvb_l):
    hw = MLA_HEADS * HEAD_PAD
    w = w_kvb_l.reshape(MLA_KV_RANK, MLA_HEADS, MLA_NOPE_DIM + MLA_V_DIM)
    top_k = jnp.pad(w[:, :, :MLA_NOPE_DIM], ((0, 0), (0, 0), (0, HEAD_PAD - MLA_NOPE_DIM)))
    top_v = jnp.pad(w[:, :, MLA_NOPE_DIM:], ((0, 0), (0, 0), (0, HEAD_PAD - MLA_V_DIM)))
    bot_k = np.zeros((LANES, MLA_HEADS, HEAD_PAD), np.float32)
    bot_v = np.zeros((LANES, MLA_HEADS, HEAD_PAD), np.float32)
    for r in range(MLA_ROPE_DIM):
        bot_k[r, :, MLA_NOPE_DIM + r] = 1.0
    bot_v[_S_ONE, :, MLA_V_DIM] = 1.0
    wk = jnp.concatenate([top_k.reshape(MLA_KV_RANK, hw), jnp.asarray(bot_k).reshape(LANES, hw)], 0)
    wv = jnp.concatenate([top_v.reshape(MLA_KV_RANK, hw), jnp.asarray(bot_v).reshape(LANES, hw)], 0)
    return jnp.concatenate([wk, wv], axis=1).astype(BF16)


def _rope_tables(positions):
    half = MLA_ROPE_DIM // 2
    inv_freq = ROPE_THETA ** (-jnp.arange(half, dtype=F32) / half)
    ang = positions.astype(F32)[:, None] * inv_freq
    cos, sin = jnp.cos(ang), jnp.sin(ang)
    ct = jnp.concatenate([cos, cos], -1)
    st = jnp.concatenate([-sin, sin], -1)
    S = positions.shape[0]
    z = lambda w: jnp.zeros((S, w), F32)
    scale = MLA_QK_DIM ** -0.5
    tabk = jnp.concatenate([ct, st, z(LANES - 2 * MLA_ROPE_DIM)], -1)
    cqt = jnp.concatenate([jnp.full((S, MLA_NOPE_DIM), scale, F32), ct * scale,
                           z(HEAD_PAD - MLA_QK_DIM)], -1)
    sqt = jnp.concatenate([z(MLA_NOPE_DIM), st * scale, z(HEAD_PAD - MLA_QK_DIM)], -1)
    return tabk, cqt, sqt


def _pick(S, pref):
    t = pref
    while S % t:
        t //= 2
    return t


def kernel(x, positions, w_in, conv_w, conv_b, gate_b, q_a_g, kv_a_g, w_qb, w_kvb, out_g, w_out, ln1_g, ln1_b, w_rg, w_re, w_gate, w_up, w_down, ln2_g, ln2_b):
    B, S, D = x.shape
    assert B == 1 and D == D_MODEL
    depth = w_in.shape[0]
    alpha = float((2 * depth) ** 0.25)
    tm = _pick(S, 512)
    bq = _pick(S, 512)
    L = _pick(S, 256)
    assert tm % SUBLANES == 0 and bq % LANES == 0 and L % LANES == 0

    tabk, cqt, sqt = _rope_tables(positions[0])
    zcol = jnp.zeros((depth, D_MODEL, 1), F32)
    win_all = jnp.take(jnp.concatenate([w_in, zcol], -1), _perm_in_cols(), axis=2).astype(BF16)
    zq = jnp.zeros((depth, MLA_Q_RANK, 1), F32)
    wq_all = jnp.take(jnp.concatenate([w_qb, zq], -1), _q_cols(), axis=2).astype(BF16)
    gb_tab = jnp.zeros((depth, 1, LANES), F32).at[:, 0, _S_I0:_S_I0 + 2 * MLSTM_HEADS].set(gate_b)
    wr_all = jnp.concatenate([w_rg, w_re, jnp.zeros((depth, D_MODEL, LANES - N_GROUPS - N_EXPERTS), F32)], -1)
    wo_all = w_out.astype(BF16)
    n_att = MLA_HEADS * MLA_V_DIM

    xs = x[0]
    for l in range(depth):
        q, k, v, mq, mk, mv, mo, gates = _proj(
            xs, win_all[l], wq_all[l], _kv_weight(w_kvb[l]), q_a_g[l][None], kv_a_g[l][None],
            conv_w[l], conv_b[l][None], gb_tab[l], tabk, cqt, sqt, tm=tm)
        a = _attention(q, k, v, out_g[l][None, :n_att], bq=bq)
        gcol = gates[:, _S_I0:_S_I0 + 2 * MLSTM_HEADS]
        hm = _mlstm(mq, mk, mv, mo, gcol, gcol.T, out_g[l][None, n_att:], L=L)
        x1, logits = _outproj(xs, a, hm, wo_all[l], ln1_g[l][None], ln1_b[l][None], wr_all[l],
                              tm=tm, alpha=alpha)
        bexp, nval, stok, sdst, sw = _route(logits, S)
        y = _moe(x1, bexp, nval, stok, sdst, sw, w_gate[l], w_up[l], w_down[l])
        xs = _combine(x1, y, ln2_g[l][None], ln2_b[l][None], tm=tm, alpha=alpha)
    return xs[None]
```

```python
import functools

import numpy as np
import jax
import jax.numpy as jnp
from jax import lax
from jax.experimental import pallas as pl
from jax.experimental.pallas import tpu as pltpu

F32 = jnp.float32
BF16 = jnp.bfloat16

D_MODEL = 1024
MLA_HEADS = 8
MLA_V_DIM = 64
MLA_NOPE_DIM = 64
MLA_ROPE_DIM = 32
MLA_QK_DIM = MLA_NOPE_DIM + MLA_ROPE_DIM
MLA_Q_RANK = 256
MLA_KV_RANK = 128
ROPE_THETA = 10000.0
MLSTM_HEADS = 4
MLSTM_HEAD_DIM = 128
D_MLSTM = MLSTM_HEADS * MLSTM_HEAD_DIM
CONV_WIDTH = 4
N_GROUPS = 8
EXPERTS_PER_GROUP = 8
N_EXPERTS = N_GROUPS * EXPERTS_PER_GROUP
D_EXPERT = 256
TOP_K = 2
EXPERT_BLOCK = 128
LN_EPS = 1e-5
RMS_EPS = 1e-6

LANES = 128
SUBLANES = 8
HEAD_PAD = 128
VMEM_LIMIT = 56 * 1024 * 1024

_C_Q0 = 0
_C_KV0 = MLA_Q_RANK
_C_MQK0 = _C_KV0 + MLA_KV_RANK
_C_MV0 = _C_MQK0 + 2 * D_MLSTM
_C_MO0 = _C_MV0 + D_MLSTM
_C_SMALL0 = _C_MO0 + D_MLSTM
D_IN_PAD = _C_SMALL0 + LANES
_S_I0 = 2 * MLA_ROPE_DIM
_S_F0 = _S_I0 + MLSTM_HEADS
_S_ONE = MLA_ROPE_DIM


def _cparams(sem, vmem=VMEM_LIMIT):
    return pltpu.CompilerParams(dimension_semantics=sem, vmem_limit_bytes=vmem)


def _proj_kernel(x_ref, win_ref, wq_ref, wkv_ref, qg_ref, kvg_ref, cw_ref, cb_ref, gb_ref,
                 tabk_ref, cq_ref, sq_ref,
                 q_out, k_out, v_out, mq_out, mk_out, mv_out, mo_out, g_out, carry_ref, *, tm):
    i = pl.program_id(0)

    @pl.when(i == 0)
    def _():
        carry_ref[...] = jnp.zeros_like(carry_ref)

    xb = x_ref[...].astype(BF16)
    p = jnp.dot(xb, win_ref[...], preferred_element_type=F32)

    def rms(v, g):
        return v * lax.rsqrt(jnp.mean(v * v, axis=-1, keepdims=True) + RMS_EPS) * g

    cqn = rms(p[:, _C_Q0:_C_KV0], qg_ref[...]).astype(BF16)
    ckvn = rms(p[:, _C_KV0:_C_MQK0], kvg_ref[...]).astype(BF16)

    qq = jnp.dot(cqn, wq_ref[...], preferred_element_type=F32)
    cq = cq_ref[...]
    sq = sq_ref[...]
    hw = MLA_HEADS * HEAD_PAD
    for h in range(MLA_HEADS):
        a = qq[:, h * HEAD_PAD:(h + 1) * HEAD_PAD]
        b = qq[:, hw + h * HEAD_PAD: hw + (h + 1) * HEAD_PAD]
        q_out[:, h * HEAD_PAD:(h + 1) * HEAD_PAD] = (a * cq + b * sq).astype(BF16)

    small = p[:, _C_SMALL0:D_IN_PAD]
    lane = lax.broadcasted_iota(jnp.int32, small.shape, 1)
    prod = small * tabk_ref[...]
    kr = prod + pltpu.roll(prod, LANES - MLA_ROPE_DIM, 1)
    kvs = jnp.where(lane < MLA_ROPE_DIM, kr, jnp.where(lane == _S_ONE, 1.0, 0.0))
    kvin = jnp.concatenate([ckvn, kvs.astype(BF16)], axis=1)
    kv = jnp.dot(kvin, wkv_ref[...], preferred_element_type=F32)
    k_out[...] = kv[:, :hw].astype(BF16)
    v_out[...] = kv[:, hw:].astype(BF16)

    g = small + gb_ref[...]
    lsig = jnp.minimum(g, 0.0) - jnp.log1p(jnp.exp(-jnp.abs(g)))
    is_f = (lane >= _S_F0) & (lane < _S_F0 + MLSTM_HEADS)
    g_out[...] = jnp.where(is_f, lsig, g)

    mqk = p[:, _C_MQK0:_C_MV0]
    ext = jnp.concatenate([carry_ref[...], mqk], axis=0)
    carry_ref[...] = mqk[tm - SUBLANES:tm, :]
    acc = cb_ref[...] + mqk * cw_ref[CONV_WIDTH - 1:CONV_WIDTH, :]
    for j in range(CONV_WIDTH - 1):
        off = SUBLANES - (CONV_WIDTH - 1) + j
        acc = acc + ext[off:off + tm, :] * cw_ref[j:j + 1, :]
    y = acc * (1.0 / (1.0 + jnp.exp(-acc)))
    mq_out[...] = (y[:, :D_MLSTM] * (MLSTM_HEAD_DIM ** -0.5)).astype(BF16)
    mk_out[...] = y[:, D_MLSTM:].astype(BF16)
    mv_out[...] = p[:, _C_MV0:_C_MO0].astype(BF16)
    mo = p[:, _C_MO0:_C_SMALL0]
    mo_out[...] = (1.0 / (1.0 + jnp.exp(-mo))).astype(BF16)


def _proj(x, win, wq, wkv, qg, kvg, cw, cb, gb, tabk, cqt, sqt, *, tm):
    S = x.shape[0]
    hw = MLA_HEADS * HEAD_PAD
    row = lambda w: pl.BlockSpec((tm, w), lambda i: (i, 0))
    full = lambda a: pl.BlockSpec(a.shape, lambda i: (0,) * a.ndim)
    out_shapes = (
        jax.ShapeDtypeStruct((S, hw), BF16), jax.ShapeDtypeStruct((S, hw), BF16),
        jax.ShapeDtypeStruct((S, hw), BF16),
        jax.ShapeDtypeStruct((S, D_MLSTM), BF16), jax.ShapeDtypeStruct((S, D_MLSTM), BF16),
        jax.ShapeDtypeStruct((S, D_MLSTM), BF16), jax.ShapeDtypeStruct((S, D_MLSTM), BF16),
        jax.ShapeDtypeStruct((S, LANES), F32))
    return pl.pallas_call(
        functools.partial(_proj_kernel, tm=tm),
        grid=(S // tm,),
        in_specs=[row(D_MODEL), full(win), full(wq), full(wkv), full(qg), full(kvg), full(cw),
                  full(cb), full(gb), row(LANES), row(LANES), row(LANES)],
        out_specs=(row(hw), row(hw), row(hw), row(D_MLSTM), row(D_MLSTM), row(D_MLSTM),
                   row(D_MLSTM), row(LANES)),
        out_shape=out_shapes,
        scratch_shapes=[pltpu.VMEM((SUBLANES, 2 * D_MLSTM), F32)],
        compiler_params=_cparams(("arbitrary",)),
        name="proj",
    )(x, win, wq, wkv, qg, kvg, cw, cb, gb, tabk, cqt, sqt)


def _attn_kernel(q_ref, k_ref, v_ref, og_ref, o_ref, s_ref, *, bq, heads_per_step):
    qi = pl.program_id(1)
    nh = heads_per_step
    cols = [(h * HEAD_PAD, (h + 1) * HEAD_PAD) for h in range(nh)]
    qs = [q_ref[:, c0:c1] for c0, c1 in cols]

    def scores(j, h):
        kt = k_ref[pl.ds(pl.multiple_of(j * bq, bq), bq), cols[h][0]:cols[h][1]]
        return lax.dot_general(qs[h], kt, (((1,), (1,)), ((), ())), preferred_element_type=F32)

    def accumulate(j, h, s, m, acc):
        vt = v_ref[pl.ds(pl.multiple_of(j * bq, bq), bq), cols[h][0]:cols[h][1]]
        m_new = jnp.maximum(m, jnp.max(s, axis=-1, keepdims=True))
        alpha = jnp.exp2(m - m_new)
        p = jnp.exp2(s - m_new)
        return m_new, alpha * acc + jnp.dot(p.astype(BF16), vt, preferred_element_type=F32)

    def step(j, carry, cur, diagonal=False, prefetch=True):
        new = []
        for h in range(nh):
            m, acc = carry[2 * h:2 * h + 2]
            if prefetch:
                s_ref[(1 - cur) * nh + h] = scores(j + 1, h)
            s = s_ref[cur * nh + h]
            if diagonal:
                r = lax.broadcasted_iota(jnp.int32, (bq, bq), 0)
                c = lax.broadcasted_iota(jnp.int32, (bq, bq), 1)
                s = jnp.where(c <= r, s, -jnp.inf)
            new += list(accumulate(j, h, s, m, acc))
        return tuple(new)

    def pair(i, carry):
        return step(2 * i + 1, step(2 * i, carry, 0), 1)

    init = []
    for h in range(nh):
        s_ref[h] = scores(0, h)
        init += [jnp.full((bq, 1), -jnp.inf, F32), jnp.zeros((bq, HEAD_PAD), F32)]
    carry = lax.fori_loop(0, qi // 2, pair, tuple(init))
    carry = lax.cond(
        qi % 2 == 1,
        lambda cr: step(qi, step(qi - 1, cr, 0), 1, diagonal=True, prefetch=False),
        lambda cr: step(qi, cr, 0, diagonal=True, prefetch=False),
        carry)
    outs = []
    for h in range(nh):
        m, acc = carry[2 * h:2 * h + 2]
        o = acc[:, :MLA_V_DIM] / acc[:, MLA_V_DIM:MLA_V_DIM + 1]
        outs.append(o * lax.rsqrt(jnp.mean(o * o, axis=-1, keepdims=True) + RMS_EPS))
    o_ref[...] = (jnp.concatenate(outs, axis=1) * og_ref[...]).astype(BF16)


def _attention(q, k, v, og, *, bq, heads_per_step=2):
    S = q.shape[0]
    n_hp = MLA_HEADS // heads_per_step
    wq = heads_per_step * HEAD_PAD
    wo = heads_per_step * MLA_V_DIM
    return pl.pallas_call(
        functools.partial(_attn_kernel, bq=bq, heads_per_step=heads_per_step),
        grid=(n_hp, S // bq),
        in_specs=[pl.BlockSpec((bq, wq), lambda h, i: (i, h)),
                  pl.BlockSpec((S, wq), lambda h, i: (0, h)),
                  pl.BlockSpec((S, wq), lambda h, i: (0, h)),
                  pl.BlockSpec((1, wo), lambda h, i: (0, h))],
        out_specs=pl.BlockSpec((bq, wo), lambda h, i: (i, h)),
        out_shape=jax.ShapeDtypeStruct((S, MLA_HEADS * MLA_V_DIM), BF16),
        scratch_shapes=[pltpu.VMEM((2 * heads_per_step, bq, bq), F32)],
        compiler_params=_cparams(("arbitrary", "arbitrary")),
        name="attention",
    )(q, k, v, og)


def _mlstm_kernel(q_ref, k_ref, v_ref, o_ref, gc_ref, gr_ref, og_ref, h_out, c_ref, m_ref, *, L):
    ci = pl.program_id(0)

    @pl.when(ci == 0)
    def _():
        c_ref[...] = jnp.zeros_like(c_ref)
        m_ref[...] = jnp.zeros_like(m_ref)

    r = lax.broadcasted_iota(jnp.int32, (L, L), 0)
    c = lax.broadcasted_iota(jnp.int32, (L, L), 1)
    tri = c <= r
    lane = lax.broadcasted_iota(jnp.int32, (L, LANES), 1)
    ones_blk = jnp.where(lane == 0, 1.0, 0.0).astype(BF16)
    gc = gc_ref[...]
    gr = gr_ref[...]
    d = MLSTM_HEAD_DIM
    for h in range(MLSTM_HEADS):
        q = q_ref[:, h * d:(h + 1) * d]
        k = k_ref[:, h * d:(h + 1) * d]
        v = v_ref[:, h * d:(h + 1) * d]
        li_col = gc[:, h:h + 1]
        lf_col = gc[:, MLSTM_HEADS + h:MLSTM_HEADS + h + 1]
        li_row = gr[h:h + 1, :]
        lf_row = gr[MLSTM_HEADS + h:MLSTM_HEADS + h + 1, :]
        b_col = jnp.sum(jnp.where(tri, lf_row, 0.0), axis=1, keepdims=True)
        b_row = jnp.sum(jnp.where(r <= c, lf_col, 0.0), axis=0, keepdims=True)
        m_prev = m_ref[h][0:1, 0:1]
        log_d = jnp.where(tri, b_col - b_row + li_row, -jnp.inf)
        log_inter = b_col + m_prev
        m_t = jnp.maximum(log_inter, jnp.max(log_d, axis=1, keepdims=True))
        w_intra = jnp.exp(log_d - m_t)
        w_inter = jnp.exp(log_inter - m_t)
        s = lax.dot_general(q, k, (((1,), (1,)), ((), ())), preferred_element_type=F32) * w_intra
        vaug = jnp.concatenate([v, ones_blk], axis=1)
        r1 = jnp.dot(s.astype(BF16), vaug, preferred_element_type=F32)
        cst = c_ref[h]
        r2 = jnp.dot(q, cst.astype(BF16), preferred_element_type=F32)
        num = r1[:, :d] + w_inter * r2[:, :d]
        den = r1[:, d:d + 1] + w_inter * r2[:, d:d + 1]
        hv = num / jnp.maximum(jnp.abs(den), jnp.exp(-m_t))
        b_end = b_col[L - 1:L, :]
        log_w = b_end - b_col + li_col
        m_new = jnp.maximum(b_end + m_prev, jnp.max(log_w, axis=0, keepdims=True))
        w_s = jnp.exp(log_w - m_new)
        decay = jnp.exp(b_end + m_prev - m_new)
        kw = (k.astype(F32) * w_s).astype(BF16)
        upd = lax.dot_general(kw, vaug, (((0,), (0,)), ((), ())), preferred_element_type=F32)
        c_ref[h] = decay * cst + upd
        m_ref[h] = jnp.broadcast_to(m_new, (SUBLANES, LANES))
        hn = hv * lax.rsqrt(jnp.mean(hv * hv, axis=-1, keepdims=True) + RMS_EPS)
        gate = o_ref[:, h * d:(h + 1) * d].astype(F32)
        h_out[:, h * d:(h + 1) * d] = (hn * gate * og_ref[:, h * d:(h + 1) * d]).astype(BF16)


def _mlstm(mq, mk, mv, mo, gcol, grow, og, *, L):
    S = mq.shape[0]
    d = MLSTM_HEAD_DIM
    row = pl.BlockSpec((L, D_MLSTM), lambda i: (i, 0))
    return pl.pallas_call(
        functools.partial(_mlstm_kernel, L=L),
        grid=(S // L,),
        in_specs=[row, row, row, row,
                  pl.BlockSpec((L, 2 * MLSTM_HEADS), lambda i: (i, 0)),
                  pl.BlockSpec((2 * MLSTM_HEADS, L), lambda i: (0, i)),
                  pl.BlockSpec((1, D_MLSTM), lambda i: (0, 0))],
        out_specs=row,
        out_shape=jax.ShapeDtypeStruct((S, D_MLSTM), BF16),
        scratch_shapes=[pltpu.VMEM((MLSTM_HEADS, d, 2 * d), F32),
                        pltpu.VMEM((MLSTM_HEADS, SUBLANES, LANES), F32)],
        compiler_params=_cparams(("arbitrary",)),
        name="mlstm",
    )(mq, mk, mv, mo, gcol, grow, og)


def _layer_norm(z, g, b):
    mu = jnp.mean(z, axis=-1, keepdims=True)
    zc = z - mu
    var = jnp.mean(zc * zc, axis=-1, keepdims=True)
    return zc * lax.rsqrt(var + LN_EPS) * g + b


def _outproj_kernel(x_ref, a_ref, h_ref, wo_ref, g_ref, b_ref, wr_ref, x1_out, lg_out, *, alpha):
    mix = jnp.concatenate([a_ref[...], h_ref[...]], axis=1)
    y = jnp.dot(mix, wo_ref[...], preferred_element_type=F32)
    x1 = _layer_norm(alpha * x_ref[...] + y, g_ref[...], b_ref[...])
    x1_out[...] = x1
    lg_out[...] = jnp.dot(x1, wr_ref[...], preferred_element_type=F32,
                          precision=lax.Precision.HIGHEST)


def _outproj(x, a, hm, wo, g, b, wr, *, tm, alpha):
    S = x.shape[0]
    row = lambda w: pl.BlockSpec((tm, w), lambda i: (i, 0))
    full = lambda arr: pl.BlockSpec(arr.shape, lambda i: (0,) * arr.ndim)
    return pl.pallas_call(
        functools.partial(_outproj_kernel, alpha=alpha),
        grid=(S // tm,),
        in_specs=[row(D_MODEL), row(MLA_HEADS * MLA_V_DIM), row(D_MLSTM), full(wo), full(g), full(b),
                  full(wr)],
        out_specs=(row(D_MODEL), row(LANES)),
        out_shape=(jax.ShapeDtypeStruct((S, D_MODEL), F32), jax.ShapeDtypeStruct((S, LANES), F32)),
        compiler_params=_cparams(("arbitrary",)),
        name="outproj",
    )(x, a, hm, wo, g, b, wr)


def _moe_kernel(bexp_ref, nval_ref, a_ref, an_ref, x_hbm, wg_ref, wu_ref, wd_ref,
                y_hbm, xbuf, ybuf, wgb, wub, wdb, gsem, ssem, *, T):
    b = pl.program_id(0)
    nb = pl.num_programs(0)
    slot = lax.rem(b, 2)
    R = EXPERT_BLOCK

    def gather_copy(tok, r, n, sl):
        return pltpu.make_async_copy(x_hbm.at[pl.ds(tok, n)], xbuf.at[sl, pl.ds(r, n)], gsem.at[sl])

    def scatter_copy(dst, r, n, sl):
        return pltpu.make_async_copy(ybuf.at[sl, pl.ds(r, n)], y_hbm.at[pl.ds(dst, n)], ssem.at[sl])

    def gather_start(idx_ref, sl):
        for r in range(R):
            tok = lax.shift_right_logical(jnp.maximum(idx_ref[0, 0, r], 0), 1)
            gather_copy(tok, r, 1, sl).start()

    def scatter_wait(n, sl):
        bit = R
        while bit:
            @pl.when((n & bit) != 0)
            def _(bit=bit):
                scatter_copy(0, 0, bit, sl).wait()
            bit //= 2

    nv = nval_ref[b]
    nv_next = nval_ref[jnp.minimum(b + 1, nb - 1)]

    @pl.when((b == 0) & (nv > 0))
    def _():
        gather_start(a_ref, 0)

    @pl.when((b + 1 < nb) & (nv_next > 0))
    def _():
        gather_start(an_ref, 1 - slot)

    @pl.when(b >= 2)
    def _():
        scatter_wait(nval_ref[jnp.maximum(b - 2, 0)], slot)

    @pl.when((b == 0) | (bexp_ref[b] != bexp_ref[jnp.maximum(b - 1, 0)]))
    def _():
        wgb[...] = wg_ref[0].astype(BF16)
        wub[...] = wu_ref[0].astype(BF16)
        wdb[...] = wd_ref[0].astype(BF16)

    @pl.when(nv > 0)
    def _():
        gather_copy(0, 0, R, slot).wait()
        xs = xbuf[slot].astype(BF16)
        g = jnp.dot(xs, wgb[...], preferred_element_type=F32)
        u = jnp.dot(xs, wub[...], preferred_element_type=F32)
        hid = (g * (1.0 / (1.0 + jnp.exp(-g)))) * u
        ybuf[slot] = jnp.dot(hid.astype(BF16), wdb[...], preferred_element_type=F32)

    def dst_row(r):
        a = a_ref[0, 0, r]
        return (a & 1) * T + lax.shift_right_logical(a, 1)

    @pl.when(nv == R)
    def _():
        for r in range(R):
            scatter_copy(dst_row(r), r, 1, slot).start()

    @pl.when((nv > 0) & (nv < R))
    def _():
        def sbody(r, _):
            scatter_copy(dst_row(r), r, 1, slot).start()
            return 0
        lax.fori_loop(0, nv, sbody, 0)

    @pl.when(b == nb - 1)
    def _():
        @pl.when(nb >= 2)
        def _():
            scatter_wait(nval_ref[jnp.maximum(b - 1, 0)], 1 - slot)
        scatter_wait(nv, slot)


def _moe(x1, block_expert, nvalid, slot_a, wg, wu, wd):
    T = x1.shape[0]
    nb = block_expert.shape[0]
    R = EXPERT_BLOCK
    smem_blk = lambda f: pl.BlockSpec((1, 1, R), f, memory_space=pltpu.SMEM)
    wspec = lambda shp: pl.BlockSpec((1,) + shp, lambda b, be, nv: (be[b], 0, 0))
    grid_spec = pltpu.PrefetchScalarGridSpec(
        num_scalar_prefetch=2,
        grid=(nb,),
        in_specs=[smem_blk(lambda b, be, nv: (b, 0, 0)),
                  smem_blk(lambda b, be, nv: (jnp.minimum(b + 1, nb - 1), 0, 0)),
                  pl.BlockSpec(memory_space=pl.ANY),
                  wspec((D_MODEL, D_EXPERT)), wspec((D_MODEL, D_EXPERT)), wspec((D_EXPERT, D_MODEL))],
        out_specs=pl.BlockSpec(memory_space=pl.ANY),
        scratch_shapes=[pltpu.VMEM((2, R, D_MODEL), F32), pltpu.VMEM((2, R, D_MODEL), F32),
                        pltpu.VMEM((D_MODEL, D_EXPERT), BF16), pltpu.VMEM((D_MODEL, D_EXPERT), BF16),
                        pltpu.VMEM((D_EXPERT, D_MODEL), BF16),
                        pltpu.SemaphoreType.DMA((2,)), pltpu.SemaphoreType.DMA((2,))])
    return pl.pallas_call(
        functools.partial(_moe_kernel, T=T),
        grid_spec=grid_spec,
        out_shape=jax.ShapeDtypeStruct((TOP_K * T, D_MODEL), F32),
        compiler_params=_cparams(("arbitrary",)),
        name="experts",
    )(block_expert, nvalid, slot_a, slot_a, x1, wg, wu, wd)


def _combine_kernel(x_ref, y0_ref, y1_ref, w_ref, g_ref, b_ref, o_ref, *, alpha):
    w = w_ref[...]
    z = alpha * x_ref[...] + (y0_ref[...] * w[:, 0:1] + y1_ref[...] * w[:, 1:2])
    o_ref[...] = _layer_norm(z, g_ref[...], b_ref[...])


def _combine(x1, y, w, g, b, *, tm, alpha):
    S = x1.shape[0]
    nt = S // tm
    return pl.pallas_call(
        functools.partial(_combine_kernel, alpha=alpha),
        grid=(nt,),
        in_specs=[pl.BlockSpec((tm, D_MODEL), lambda i: (i, 0)),
                  pl.BlockSpec((tm, D_MODEL), lambda i: (i, 0)),
                  pl.BlockSpec((tm, D_MODEL), lambda i: (i + nt, 0)),
                  pl.BlockSpec((tm, TOP_K), lambda i: (i, 0)),
                  pl.BlockSpec((1, D_MODEL), lambda i: (0, 0)),
                  pl.BlockSpec((1, D_MODEL), lambda i: (0, 0))],
        out_specs=pl.BlockSpec((tm, D_MODEL), lambda i: (i, 0)),
        out_shape=jax.ShapeDtypeStruct((S, D_MODEL), F32),
        compiler_params=_cparams(("arbitrary",)),
        name="combine",
    )(x1, y, y, w, g, b)


def _route(logits, T):
    g_logits = logits[:, :N_GROUPS]
    g_prob = jax.nn.softmax(g_logits, axis=-1)
    g_top = jnp.argmax(g_prob, axis=-1)
    g_p = jnp.max(g_prob, axis=-1)
    e_logits = logits[:, N_GROUPS:N_GROUPS + N_EXPERTS].reshape(T, N_GROUPS, EXPERTS_PER_GROUP)
    e_in = jnp.take_along_axis(e_logits, g_top[:, None, None], axis=1)[:, 0, :]
    p = jax.nn.softmax(e_in, axis=-1)
    i1 = jnp.argmax(p, axis=-1)
    pm = jnp.where(jnp.arange(EXPERTS_PER_GROUP)[None, :] == i1[:, None], -jnp.inf, p)
    i2 = jnp.argmax(pm, axis=-1)
    e_top = jnp.stack([i1, i2], axis=-1)
    e_p = jnp.stack([jnp.max(p, axis=-1), jnp.max(pm, axis=-1)], axis=-1)
    e_p = e_p / jnp.sum(e_p, -1, keepdims=True)
    weights = g_p[:, None] * e_p
    flat_e = (g_top[:, None] * EXPERTS_PER_GROUP + e_top).astype(jnp.int32).reshape(-1)
    n_assign = T * TOP_K
    onehot = (flat_e[:, None] == jnp.arange(N_EXPERTS, dtype=jnp.int32)[None, :]).astype(jnp.int32)
    csum = jnp.cumsum(onehot, axis=0)
    rank = jnp.take_along_axis(csum, flat_e[:, None], axis=1)[:, 0] - 1
    counts = csum[-1]
    padded = (counts + EXPERT_BLOCK - 1) // EXPERT_BLOCK * EXPERT_BLOCK
    pad_ends = jnp.cumsum(padded)
    pad_starts = pad_ends - padded
    dest = pad_starts[flat_e] + rank
    n_blocks = n_assign // EXPERT_BLOCK + N_EXPERTS
    n_slots = n_blocks * EXPERT_BLOCK
    slot_a = jnp.full((n_slots,), -1, jnp.int32).at[dest].set(jnp.arange(n_assign, dtype=jnp.int32))
    block_start = jnp.arange(n_blocks, dtype=jnp.int32) * EXPERT_BLOCK
    block_expert = jnp.minimum(jnp.sum((pad_ends[None, :] <= block_start[:, None]).astype(jnp.int32), axis=1),
                               N_EXPERTS - 1).astype(jnp.int32)
    nvalid = jnp.clip(pad_starts[block_expert] + counts[block_expert] - block_start, 0,
                      EXPERT_BLOCK).astype(jnp.int32)
    nvalid = jnp.where(block_start < pad_ends[-1], nvalid, 0)
    return block_expert, nvalid, slot_a.reshape(n_blocks, 1, EXPERT_BLOCK), weights


def _in_weight(w_in):
    kr0 = MLA_Q_RANK + MLA_KV_RANK
    mq0 = kr0 + MLA_ROPE_DIM
    i0 = mq0 + 4 * D_MLSTM
    half = MLA_ROPE_DIM // 2
    n_small = 2 * MLA_ROPE_DIM + 2 * MLSTM_HEADS
    pad = jnp.zeros(w_in.shape[:-1] + (LANES - n_small,), w_in.dtype)
    return jnp.concatenate(
        [w_in[..., :kr0], w_in[..., mq0:i0], w_in[..., kr0:mq0], w_in[..., kr0 + half:mq0],
         w_in[..., kr0:kr0 + half], w_in[..., i0:i0 + 2 * MLSTM_HEADS], pad], axis=-1).astype(BF16)


def _q_weight(w_qb):
    lead = w_qb.shape[:-1]
    w = w_qb.reshape(lead + (MLA_HEADS, MLA_QK_DIM))
    half = MLA_ROPE_DIM // 2
    z = lambda n: jnp.zeros(lead + (MLA_HEADS, n), w_qb.dtype)
    main = jnp.concatenate([w, z(HEAD_PAD - MLA_QK_DIM)], axis=-1)
    swp = jnp.concatenate([z(MLA_NOPE_DIM), w[..., MLA_NOPE_DIM + half:], w[..., MLA_NOPE_DIM:MLA_NOPE_DIM + half],
                           z(HEAD_PAD - MLA_QK_DIM)], axis=-1)
    hw = MLA_HEADS * HEAD_PAD
    return jnp.concatenate([main.reshape(lead + (hw,)), swp.reshape(lead + (hw,))], axis=-1).astype(BF16)


def _kv_weight(w_kvb_l):
    hw = MLA_HEADS * HEAD_PAD
    w = w_kvb_l.reshape(MLA_KV_RANK, MLA_HEADS, MLA_NOPE_DIM + MLA_V_DIM)
    top_k = jnp.pad(w[:, :, :MLA_NOPE_DIM], ((0, 0), (0, 0), (0, HEAD_PAD - MLA_NOPE_DIM)))
    top_v = jnp.pad(w[:, :, MLA_NOPE_DIM:], ((0, 0), (0, 0), (0, HEAD_PAD - MLA_V_DIM)))
    bot_k = np.zeros((LANES, MLA_HEADS, HEAD_PAD), np.float32)
    bot_v = np.zeros((LANES, MLA_HEADS, HEAD_PAD), np.float32)
    for r in range(MLA_ROPE_DIM):
        bot_k[r, :, MLA_NOPE_DIM + r] = 1.0
    bot_v[_S_ONE, :, MLA_V_DIM] = 1.0
    wk = jnp.concatenate([top_k.reshape(MLA_KV_RANK, hw), jnp.asarray(bot_k).reshape(LANES, hw)], 0)
    wv = jnp.concatenate([top_v.reshape(MLA_KV_RANK, hw), jnp.asarray(bot_v).reshape(LANES, hw)], 0)
    return jnp.concatenate([wk, wv], axis=1).astype(BF16)


def _rope_tables(positions):
    half = MLA_ROPE_DIM // 2
    inv_freq = ROPE_THETA ** (-jnp.arange(half, dtype=F32) / half)
    ang = positions.astype(F32)[:, None] * inv_freq
    cos, sin = jnp.cos(ang), jnp.sin(ang)
    ct = jnp.concatenate([cos, cos], -1)
    st = jnp.concatenate([-sin, sin], -1)
    S = positions.shape[0]
    z = lambda w: jnp.zeros((S, w), F32)
    scale = MLA_QK_DIM ** -0.5 * float(np.log2(np.e))
    tabk = jnp.concatenate([ct, st, z(LANES - 2 * MLA_ROPE_DIM)], -1)
    cqt = jnp.concatenate([jnp.full((S, MLA_NOPE_DIM), scale, F32), ct * scale,
                           z(HEAD_PAD - MLA_QK_DIM)], -1)
    sqt = jnp.concatenate([z(MLA_NOPE_DIM), st * scale, z(HEAD_PAD - MLA_QK_DIM)], -1)
    return tabk, cqt, sqt


def _pick(S, pref):
    t = pref
    while S % t:
        t //= 2
    return t


def kernel(x, positions, w_in, conv_w, conv_b, gate_b, q_a_g, kv_a_g, w_qb, w_kvb, out_g, w_out, ln1_g, ln1_b, w_rg, w_re, w_gate, w_up, w_down, ln2_g, ln2_b):
    B, S, D = x.shape
    assert B == 1 and D == D_MODEL
    depth = w_in.shape[0]
    alpha = float((2 * depth) ** 0.25)
    tm = _pick(S, 512)
    bq = _pick(S, 512)
    L = _pick(S, 256)
    assert tm % SUBLANES == 0 and bq % LANES == 0 and L % LANES == 0

    tabk, cqt, sqt = _rope_tables(positions[0])
    win_all = _in_weight(w_in)
    wq_all = _q_weight(w_qb)
    gb_tab = jnp.zeros((depth, 1, LANES), F32).at[:, 0, _S_I0:_S_I0 + 2 * MLSTM_HEADS].set(gate_b)
    wr_all = jnp.concatenate([w_rg, w_re, jnp.zeros((depth, D_MODEL, LANES - N_GROUPS - N_EXPERTS), F32)], -1)
    wo_all = w_out.astype(BF16)
    n_att = MLA_HEADS * MLA_V_DIM

    xs = x[0]
    for l in range(depth):
        q, k, v, mq, mk, mv, mo, gates = _proj(
            xs, win_all[l], wq_all[l], _kv_weight(w_kvb[l]), q_a_g[l][None], kv_a_g[l][None],
            conv_w[l], conv_b[l][None], gb_tab[l], tabk, cqt, sqt, tm=tm)
        a = _attention(q, k, v, out_g[l][None, :n_att], bq=bq)
        gcol = gates[:, _S_I0:_S_I0 + 2 * MLSTM_HEADS]
        hm = _mlstm(mq, mk, mv, mo, gcol, gcol.T, out_g[l][None, n_att:], L=L)
        x1, logits = _outproj(xs, a, hm, wo_all[l], ln1_g[l][None], ln1_b[l][None], wr_all[l],
                              tm=tm, alpha=alpha)
        bexp, nval, slot_a, rw = _route(logits, S)
        y = _moe(x1, bexp, nval, slot_a, w_gate[l], w_up[l], w_down[l])
        xs = _combine(x1, y, rw, ln2_g[l][None], ln2_b[l][None], tm=tm, alpha=alpha)
    return xs[None]
```

```python
import functools

import numpy as np
import jax
import jax.numpy as jnp
from jax import lax
from jax.experimental import pallas as pl
from jax.experimental.pallas import tpu as pltpu

F32 = jnp.float32
BF16 = jnp.bfloat16

D_MODEL = 1024
MLA_HEADS = 8
MLA_V_DIM = 64
MLA_NOPE_DIM = 64
MLA_ROPE_DIM = 32
MLA_QK_DIM = MLA_NOPE_DIM + MLA_ROPE_DIM
MLA_Q_RANK = 256
MLA_KV_RANK = 128
ROPE_THETA = 10000.0
MLSTM_HEADS = 4
MLSTM_HEAD_DIM = 128
D_MLSTM = MLSTM_HEADS * MLSTM_HEAD_DIM
CONV_WIDTH = 4
N_GROUPS = 8
EXPERTS_PER_GROUP = 8
N_EXPERTS = N_GROUPS * EXPERTS_PER_GROUP
D_EXPERT = 256
TOP_K = 2
EXPERT_BLOCK = 128
LN_EPS = 1e-5
RMS_EPS = 1e-6

LANES = 128
SUBLANES = 8
HEAD_PAD = 128
VMEM_LIMIT = 56 * 1024 * 1024

_C_Q0 = 0
_C_KV0 = MLA_Q_RANK
_C_MQK0 = _C_KV0 + MLA_KV_RANK
_C_MV0 = _C_MQK0 + 2 * D_MLSTM
_C_MO0 = _C_MV0 + D_MLSTM
_C_SMALL0 = _C_MO0 + D_MLSTM
D_IN_PAD = _C_SMALL0 + LANES
_S_I0 = 2 * MLA_ROPE_DIM
_S_F0 = _S_I0 + MLSTM_HEADS
_S_ONE = MLA_ROPE_DIM


def _cparams(sem, vmem=VMEM_LIMIT):
    return pltpu.CompilerParams(dimension_semantics=sem, vmem_limit_bytes=vmem)


def _proj_kernel(x_ref, win_ref, wq_ref, wkv_ref, qg_ref, kvg_ref, cw_ref, cb_ref, gb_ref,
                 tabk_ref, cq_ref, sq_ref,
                 q_out, k_out, v_out, mq_out, mk_out, mv_out, mo_out, g_out, carry_ref, *, tm):
    i = pl.program_id(0)

    @pl.when(i == 0)
    def _():
        carry_ref[...] = jnp.zeros_like(carry_ref)

    xb = x_ref[...].astype(BF16)
    p = jnp.dot(xb, win_ref[...], preferred_element_type=F32)

    def rms(v, g):
        return v * lax.rsqrt(jnp.mean(v * v, axis=-1, keepdims=True) + RMS_EPS) * g

    cqn = rms(p[:, _C_Q0:_C_KV0], qg_ref[...]).astype(BF16)
    ckvn = rms(p[:, _C_KV0:_C_MQK0], kvg_ref[...]).astype(BF16)

    qq = jnp.dot(cqn, wq_ref[...], preferred_element_type=F32)
    cq = cq_ref[...]
    sq = sq_ref[...]
    hw = MLA_HEADS * HEAD_PAD
    for h in range(MLA_HEADS):
        a = qq[:, h * HEAD_PAD:(h + 1) * HEAD_PAD]
        b = qq[:, hw + h * HEAD_PAD: hw + (h + 1) * HEAD_PAD]
        q_out[:, h * HEAD_PAD:(h + 1) * HEAD_PAD] = (a * cq + b * sq).astype(BF16)

    small = p[:, _C_SMALL0:D_IN_PAD]
    lane = lax.broadcasted_iota(jnp.int32, small.shape, 1)
    prod = small * tabk_ref[...]
    kr = prod + pltpu.roll(prod, LANES - MLA_ROPE_DIM, 1)
    kvs = jnp.where(lane < MLA_ROPE_DIM, kr, jnp.where(lane == _S_ONE, 1.0, 0.0))
    kvin = jnp.concatenate([ckvn, kvs.astype(BF16)], axis=1)
    kv = jnp.dot(kvin, wkv_ref[...], preferred_element_type=F32)
    k_out[...] = kv[:, :hw].astype(BF16)
    v_out[...] = kv[:, hw:].T.astype(BF16)

    g = small + gb_ref[...]
    lsig = jnp.minimum(g, 0.0) - jnp.log1p(jnp.exp(-jnp.abs(g)))
    is_f = (lane >= _S_F0) & (lane < _S_F0 + MLSTM_HEADS)
    g_out[...] = jnp.where(is_f, lsig, g)

    mqk = p[:, _C_MQK0:_C_MV0]
    ext = jnp.concatenate([carry_ref[...], mqk], axis=0)
    carry_ref[...] = mqk[tm - SUBLANES:tm, :]
    acc = cb_ref[...] + mqk * cw_ref[CONV_WIDTH - 1:CONV_WIDTH, :]
    for j in range(CONV_WIDTH - 1):
        off = SUBLANES - (CONV_WIDTH - 1) + j
        acc = acc + ext[off:off + tm, :] * cw_ref[j:j + 1, :]
    y = acc * (1.0 / (1.0 + jnp.exp(-acc)))
    mq_out[...] = (y[:, :D_MLSTM] * (MLSTM_HEAD_DIM ** -0.5)).astype(BF16)
    mk_out[...] = y[:, D_MLSTM:].astype(BF16)
    mv_out[...] = p[:, _C_MV0:_C_MO0].astype(BF16)
    mo = p[:, _C_MO0:_C_SMALL0]
    mo_out[...] = (1.0 / (1.0 + jnp.exp(-mo))).astype(BF16)


def _proj(x, win, wq, wkv, qg, kvg, cw, cb, gb, tabk, cqt, sqt, *, tm):
    S = x.shape[0]
    hw = MLA_HEADS * HEAD_PAD
    row = lambda w: pl.BlockSpec((tm, w), lambda i: (i, 0))
    full = lambda a: pl.BlockSpec(a.shape, lambda i: (0,) * a.ndim)
    out_shapes = (
        jax.ShapeDtypeStruct((S, hw), BF16), jax.ShapeDtypeStruct((S, hw), BF16),
        jax.ShapeDtypeStruct((hw, S), BF16),
        jax.ShapeDtypeStruct((S, D_MLSTM), BF16), jax.ShapeDtypeStruct((S, D_MLSTM), BF16),
        jax.ShapeDtypeStruct((S, D_MLSTM), BF16), jax.ShapeDtypeStruct((S, D_MLSTM), BF16),
        jax.ShapeDtypeStruct((S, LANES), F32))
    return pl.pallas_call(
        functools.partial(_proj_kernel, tm=tm),
        grid=(S // tm,),
        in_specs=[row(D_MODEL), full(win), full(wq), full(wkv), full(qg), full(kvg), full(cw),
                  full(cb), full(gb), row(LANES), row(LANES), row(LANES)],
        out_specs=(row(hw), row(hw), pl.BlockSpec((hw, tm), lambda i: (0, i)), row(D_MLSTM), row(D_MLSTM), row(D_MLSTM),
                   row(D_MLSTM), row(LANES)),
        out_shape=out_shapes,
        scratch_shapes=[pltpu.VMEM((SUBLANES, 2 * D_MLSTM), F32)],
        compiler_params=_cparams(("arbitrary",)),
        name="proj",
    )(x, win, wq, wkv, qg, kvg, cw, cb, gb, tabk, cqt, sqt)


VT_ROWS = 80


def _attn_kernel(q_ref, k_ref, vt_ref, og_ref, o_ref, s_ref, *, bq, heads_per_step):
    qi = pl.program_id(1)
    nh = heads_per_step
    cols = [(h * HEAD_PAD, (h + 1) * HEAD_PAD) for h in range(nh)]
    qs = [q_ref[:, c0:c1] for c0, c1 in cols]

    def scores(j, h):
        kt = k_ref[pl.ds(pl.multiple_of(j * bq, bq), bq), cols[h][0]:cols[h][1]]
        return lax.dot_general(kt, qs[h], (((1,), (1,)), ((), ())), preferred_element_type=F32)

    def accumulate(j, h, s, m, acc):
        vt = vt_ref[h * HEAD_PAD:h * HEAD_PAD + VT_ROWS, pl.ds(pl.multiple_of(j * bq, bq), bq)]
        m_new = jnp.maximum(m, jnp.max(s, axis=0, keepdims=True))
        alpha = jnp.exp2(m - m_new)
        p = jnp.exp2(s - m_new)
        return m_new, alpha * acc + jnp.dot(vt, p.astype(BF16), preferred_element_type=F32)

    def step(j, carry, cur, diagonal=False, prefetch=True):
        new = []
        for h in range(nh):
            m, acc = carry[2 * h:2 * h + 2]
            if prefetch:
                s_ref[(1 - cur) * nh + h] = scores(j + 1, h)
            s = s_ref[cur * nh + h]
            if diagonal:
                key = lax.broadcasted_iota(jnp.int32, (bq, bq), 0)
                qry = lax.broadcasted_iota(jnp.int32, (bq, bq), 1)
                s = jnp.where(key <= qry, s, -jnp.inf)
            new += list(accumulate(j, h, s, m, acc))
        return tuple(new)

    def pair(i, carry):
        return step(2 * i + 1, step(2 * i, carry, 0), 1)

    init = []
    for h in range(nh):
        s_ref[h] = scores(0, h)
        init += [jnp.full((1, bq), -jnp.inf, F32), jnp.zeros((VT_ROWS, bq), F32)]
    carry = lax.fori_loop(0, qi // 2, pair, tuple(init))
    carry = lax.cond(
        qi % 2 == 1,
        lambda cr: step(qi, step(qi - 1, cr, 0), 1, diagonal=True, prefetch=False),
        lambda cr: step(qi, cr, 0, diagonal=True, prefetch=False),
        carry)
    outs = []
    for h in range(nh):
        acc = carry[2 * h + 1]
        o = acc[:MLA_V_DIM, :] / acc[MLA_V_DIM:MLA_V_DIM + 1, :]
        o = o * lax.rsqrt(jnp.mean(o * o, axis=0, keepdims=True) + RMS_EPS)
        outs.append(o.T)
    o_ref[...] = (jnp.concatenate(outs, axis=1) * og_ref[...]).astype(BF16)


def _attention(q, k, vt, og, *, bq, heads_per_step=2):
    S = q.shape[0]
    n_hp = MLA_HEADS // heads_per_step
    wq = heads_per_step * HEAD_PAD
    wo = heads_per_step * MLA_V_DIM
    return pl.pallas_call(
        functools.partial(_attn_kernel, bq=bq, heads_per_step=heads_per_step),
        grid=(n_hp, S // bq),
        in_specs=[pl.BlockSpec((bq, wq), lambda h, i: (i, h)),
                  pl.BlockSpec((S, wq), lambda h, i: (0, h)),
                  pl.BlockSpec((wq, S), lambda h, i: (h, 0)),
                  pl.BlockSpec((1, wo), lambda h, i: (0, h))],
        out_specs=pl.BlockSpec((bq, wo), lambda h, i: (i, h)),
        out_shape=jax.ShapeDtypeStruct((S, MLA_HEADS * MLA_V_DIM), BF16),
        scratch_shapes=[pltpu.VMEM((2 * heads_per_step, bq, bq), F32)],
        compiler_params=_cparams(("arbitrary", "arbitrary")),
        name="attention",
    )(q, k, vt, og)


def _mlstm_kernel(q_ref, k_ref, v_ref, o_ref, gc_ref, gr_ref, og_ref, h_out, c_ref, m_ref, *, L):
    ci = pl.program_id(0)

    @pl.when(ci == 0)
    def _():
        c_ref[...] = jnp.zeros_like(c_ref)
        m_ref[...] = jnp.zeros_like(m_ref)

    r = lax.broadcasted_iota(jnp.int32, (L, L), 0)
    c = lax.broadcasted_iota(jnp.int32, (L, L), 1)
    tri = c <= r
    lane = lax.broadcasted_iota(jnp.int32, (L, LANES), 1)
    ones_blk = jnp.where(lane == 0, 1.0, 0.0).astype(BF16)
    gc = gc_ref[...]
    gr = gr_ref[...]
    d = MLSTM_HEAD_DIM
    for h in range(MLSTM_HEADS):
        q = q_ref[:, h * d:(h + 1) * d]
        k = k_ref[:, h * d:(h + 1) * d]
        v = v_ref[:, h * d:(h + 1) * d]
        li_col = gc[:, h:h + 1]
        lf_col = gc[:, MLSTM_HEADS + h:MLSTM_HEADS + h + 1]
        li_row = gr[h:h + 1, :]
        lf_row = gr[MLSTM_HEADS + h:MLSTM_HEADS + h + 1, :]
        b_col = jnp.sum(jnp.where(tri, lf_row, 0.0), axis=1, keepdims=True)
        b_row = jnp.sum(jnp.where(r <= c, lf_col, 0.0), axis=0, keepdims=True)
        m_prev = m_ref[h][0:1, 0:1]
        log_d = jnp.where(tri, b_col - b_row + li_row, -jnp.inf)
        log_inter = b_col + m_prev
        m_t = jnp.maximum(log_inter, jnp.max(log_d, axis=1, keepdims=True))
        w_intra = jnp.exp(log_d - m_t)
        w_inter = jnp.exp(log_inter - m_t)
        s = lax.dot_general(q, k, (((1,), (1,)), ((), ())), preferred_element_type=F32) * w_intra
        vaug = jnp.concatenate([v, ones_blk], axis=1)
        r1 = jnp.dot(s.astype(BF16), vaug, preferred_element_type=F32)
        cst = c_ref[h]
        r2 = jnp.dot(q, cst.astype(BF16), preferred_element_type=F32)
        num = r1[:, :d] + w_inter * r2[:, :d]
        den = r1[:, d:d + 1] + w_inter * r2[:, d:d + 1]
        hv = num / jnp.maximum(jnp.abs(den), jnp.exp(-m_t))
        b_end = b_col[L - 1:L, :]
        log_w = b_end - b_col + li_col
        m_new = jnp.maximum(b_end + m_prev, jnp.max(log_w, axis=0, keepdims=True))
        w_s = jnp.exp(log_w - m_new)
        decay = jnp.exp(b_end + m_prev - m_new)
        kw = (k.astype(F32) * w_s).astype(BF16)
        upd = lax.dot_general(kw, vaug, (((0,), (0,)), ((), ())), preferred_element_type=F32)
        c_ref[h] = decay * cst + upd
        m_ref[h] = jnp.broadcast_to(m_new, (SUBLANES, LANES))
        hn = hv * lax.rsqrt(jnp.mean(hv * hv, axis=-1, keepdims=True) + RMS_EPS)
        gate = o_ref[:, h * d:(h + 1) * d].astype(F32)
        h_out[:, h * d:(h + 1) * d] = (hn * gate * og_ref[:, h * d:(h + 1) * d]).astype(BF16)


def _mlstm(mq, mk, mv, mo, gcol, grow, og, *, L):
    S = mq.shape[0]
    d = MLSTM_HEAD_DIM
    row = pl.BlockSpec((L, D_MLSTM), lambda i: (i, 0))
    return pl.pallas_call(
        functools.partial(_mlstm_kernel, L=L),
        grid=(S // L,),
        in_specs=[row, row, row, row,
                  pl.BlockSpec((L, 2 * MLSTM_HEADS), lambda i: (i, 0)),
                  pl.BlockSpec((2 * MLSTM_HEADS, L), lambda i: (0, i)),
                  pl.BlockSpec((1, D_MLSTM), lambda i: (0, 0))],
        out_specs=row,
        out_shape=jax.ShapeDtypeStruct((S, D_MLSTM), BF16),
        scratch_shapes=[pltpu.VMEM((MLSTM_HEADS, d, 2 * d), F32),
                        pltpu.VMEM((MLSTM_HEADS, SUBLANES, LANES), F32)],
        compiler_params=_cparams(("arbitrary",)),
        name="mlstm",
    )(mq, mk, mv, mo, gcol, grow, og)


def _layer_norm(z, g, b):
    mu = jnp.mean(z, axis=-1, keepdims=True)
    zc = z - mu
    var = jnp.mean(zc * zc, axis=-1, keepdims=True)
    return zc * lax.rsqrt(var + LN_EPS) * g + b


def _outproj_kernel(x_ref, a_ref, h_ref, wo_ref, g_ref, b_ref, wr_ref, x1_out, x1t_out, lg_out, *, alpha, tm):
    mix = jnp.concatenate([a_ref[...], h_ref[...]], axis=1)
    y = jnp.dot(mix, wo_ref[...], preferred_element_type=F32)
    x1 = _layer_norm(alpha * x_ref[...] + y, g_ref[...], b_ref[...])
    x1_out[...] = x1
    for c in range(SUBLANES):
        x1t_out[pl.ds(c, tm, stride=SUBLANES), :] = x1[:, c * LANES:(c + 1) * LANES]
    lg_out[...] = jnp.dot(x1, wr_ref[...], preferred_element_type=F32,
                          precision=lax.Precision.HIGHEST)


def _outproj(x, a, hm, wo, g, b, wr, *, tm, alpha):
    S = x.shape[0]
    row = lambda w: pl.BlockSpec((tm, w), lambda i: (i, 0))
    full = lambda arr: pl.BlockSpec(arr.shape, lambda i: (0,) * arr.ndim)
    return pl.pallas_call(
        functools.partial(_outproj_kernel, alpha=alpha, tm=tm),
        grid=(S // tm,),
        in_specs=[row(D_MODEL), row(MLA_HEADS * MLA_V_DIM), row(D_MLSTM), full(wo), full(g), full(b),
                  full(wr)],
        out_specs=(row(D_MODEL), pl.BlockSpec((tm * SUBLANES, LANES), lambda i: (i, 0)), row(LANES)),
        out_shape=(jax.ShapeDtypeStruct((S, D_MODEL), F32),
                   jax.ShapeDtypeStruct((S * SUBLANES, LANES), F32),
                   jax.ShapeDtypeStruct((S, LANES), F32)),
        compiler_params=_cparams(("arbitrary",)),
        name="outproj",
    )(x, a, hm, wo, g, b, wr)


def _moe_kernel(bexp_ref, nval_ref, a_ref, an_ref, x_hbm, wg_ref, wu_ref, wd_ref,
                y_hbm, xbuf, ybuf, wgb, wub, wdb, gsem, ssem, *, T):
    b = pl.program_id(0)
    nb = pl.num_programs(0)
    slot = lax.rem(b, 2)
    R = EXPERT_BLOCK

    G = SUBLANES

    def gather_copy(tok, r, n, sl):
        return pltpu.make_async_copy(x_hbm.at[pl.ds(tok * G, n * G)], xbuf.at[sl, pl.ds(r * G, n * G)],
                                     gsem.at[sl])

    def scatter_copy(dst, r, n, sl):
        return pltpu.make_async_copy(ybuf.at[sl, pl.ds(r * G, n * G)], y_hbm.at[pl.ds(dst * G, n * G)],
                                     ssem.at[sl])

    def gather_start(idx_ref, sl):
        for r in range(R):
            gather_copy(lax.shift_right_logical(idx_ref[0, 0, r], 1), r, 1, sl).start()

    def scatter_wait(n, sl):
        bit = R
        while bit:
            @pl.when((n & bit) != 0)
            def _(bit=bit):
                scatter_copy(0, 0, bit, sl).wait()
            bit //= 2

    nv = nval_ref[b]
    nv_prev = nval_ref[jnp.maximum(b - 1, 0)]

    @pl.when((b == 0) & (nv > 0))
    def _():
        gather_start(a_ref, 0)

    @pl.when(b >= 2)
    def _():
        scatter_wait(nval_ref[jnp.maximum(b - 2, 0)], slot)

    @pl.when((b == 0) | (bexp_ref[b] != bexp_ref[jnp.maximum(b - 1, 0)]))
    def _():
        wgb[...] = wg_ref[0, 0].astype(BF16)
        wub[...] = wu_ref[0, 0].astype(BF16)
        wdb[...] = wd_ref[0, 0].astype(BF16)

    def dst_row(r):
        a = a_ref[0, 0, r]
        return (a & 1) * T + lax.shift_right_logical(a, 1)

    for sl in range(2):
        @pl.when((slot == sl) & (nv > 0))
        def _(sl=sl):
            gather_copy(0, 0, R, sl).wait()
            gather_start(an_ref, 1 - sl)
            xs = jnp.concatenate([xbuf[sl, pl.ds(c, R, stride=G), :] for c in range(G)], axis=1)
            xs = xs.astype(BF16)
            g = jnp.dot(xs, wgb[...], preferred_element_type=F32)
            u = jnp.dot(xs, wub[...], preferred_element_type=F32)
            hid = (g * (1.0 / (1.0 + jnp.exp(-g)))) * u
            y = jnp.dot(hid.astype(BF16), wdb[...], preferred_element_type=F32)
            for c in range(G):
                ybuf[sl, pl.ds(c, R, stride=G), :] = y[:, c * LANES:(c + 1) * LANES]

        @pl.when((slot == sl) & (nv == R))
        def _(sl=sl):
            for r in range(R):
                scatter_copy(dst_row(r), r, 1, sl).start()

    @pl.when((nv > 0) & (nv < R))
    def _():
        def sbody(r, _):
            scatter_copy(dst_row(r), r, 1, slot).start()
            return 0
        lax.fori_loop(0, nv, sbody, 0)

    @pl.when((nv == 0) & (b >= 1) & (nv_prev > 0))
    def _():
        gather_copy(0, 0, R, slot).wait()

    @pl.when(b == nb - 1)
    def _():
        scatter_wait(nv_prev, 1 - slot)
        scatter_wait(nv, slot)


def _moe(x1t, block_expert, nvalid, slot_a, wg, wu, wd, layer):
    G = SUBLANES
    assert G * LANES == D_MODEL
    T = x1t.shape[0] // G
    nb = block_expert.shape[0]
    assert nb >= 2
    R = EXPERT_BLOCK
    smem_blk = lambda f: pl.BlockSpec((1, 1, R), f, memory_space=pltpu.SMEM)
    wspec = lambda shp: pl.BlockSpec((1, 1) + shp, lambda b, be, nv: (layer, be[b], 0, 0))
    grid_spec = pltpu.PrefetchScalarGridSpec(
        num_scalar_prefetch=2,
        grid=(nb,),
        in_specs=[smem_blk(lambda b, be, nv: (b, 0, 0)),
                  smem_blk(lambda b, be, nv: (jnp.minimum(b + 1, nb - 1), 0, 0)),
                  pl.BlockSpec(memory_space=pl.ANY),
                  wspec((D_MODEL, D_EXPERT)), wspec((D_MODEL, D_EXPERT)), wspec((D_EXPERT, D_MODEL))],
        out_specs=pl.BlockSpec(memory_space=pl.ANY),
        scratch_shapes=[pltpu.VMEM((2, R * G, LANES), F32), pltpu.VMEM((2, R * G, LANES), F32),
                        pltpu.VMEM((D_MODEL, D_EXPERT), BF16), pltpu.VMEM((D_MODEL, D_EXPERT), BF16),
                        pltpu.VMEM((D_EXPERT, D_MODEL), BF16),
                        pltpu.SemaphoreType.DMA((2,)), pltpu.SemaphoreType.DMA((2,))])
    return pl.pallas_call(
        functools.partial(_moe_kernel, T=T),
        grid_spec=grid_spec,
        out_shape=jax.ShapeDtypeStruct((TOP_K * T * G, LANES), F32),
        compiler_params=_cparams(("arbitrary",)),
        name="experts",
    )(block_expert, nvalid, slot_a, slot_a, x1t, wg, wu, wd)


def _rows_from_tiles(ref, n):
    return jnp.concatenate([ref[pl.ds(c, n, stride=SUBLANES), :] for c in range(SUBLANES)], axis=1)


def _combine_kernel(x_ref, y0_ref, y1_ref, w_ref, g_ref, b_ref, o_ref, *, alpha, tm):
    w = w_ref[...]
    y = _rows_from_tiles(y0_ref, tm) * w[:, 0:1] + _rows_from_tiles(y1_ref, tm) * w[:, 1:2]
    o_ref[...] = _layer_norm(alpha * x_ref[...] + y, g_ref[...], b_ref[...])


def _combine(x1, y, w, g, b, *, tm, alpha):
    S = x1.shape[0]
    nt = S // tm
    G = SUBLANES
    return pl.pallas_call(
        functools.partial(_combine_kernel, alpha=alpha, tm=tm),
        grid=(nt,),
        in_specs=[pl.BlockSpec((tm, D_MODEL), lambda i: (i, 0)),
                  pl.BlockSpec((tm * G, LANES), lambda i: (i, 0)),
                  pl.BlockSpec((tm * G, LANES), lambda i: (i + nt, 0)),
                  pl.BlockSpec((tm, TOP_K), lambda i: (i, 0)),
                  pl.BlockSpec((1, D_MODEL), lambda i: (0, 0)),
                  pl.BlockSpec((1, D_MODEL), lambda i: (0, 0))],
        out_specs=pl.BlockSpec((tm, D_MODEL), lambda i: (i, 0)),
        out_shape=jax.ShapeDtypeStruct((S, D_MODEL), F32),
        compiler_params=_cparams(("arbitrary",)),
        name="combine",
    )(x1, y, y, w, g, b)


def _route(logits, T):
    g_logits = logits[:, :N_GROUPS]
    g_prob = jax.nn.softmax(g_logits, axis=-1)
    g_top = jnp.argmax(g_prob, axis=-1)
    g_p = jnp.max(g_prob, axis=-1)
    e_logits = logits[:, N_GROUPS:N_GROUPS + N_EXPERTS].reshape(T, N_GROUPS, EXPERTS_PER_GROUP)
    e_in = jnp.take_along_axis(e_logits, g_top[:, None, None], axis=1)[:, 0, :]
    p = jax.nn.softmax(e_in, axis=-1)
    i1 = jnp.argmax(p, axis=-1)
    pm = jnp.where(jnp.arange(EXPERTS_PER_GROUP)[None, :] == i1[:, None], -jnp.inf, p)
    i2 = jnp.argmax(pm, axis=-1)
    e_top = jnp.stack([i1, i2], axis=-1)
    e_p = jnp.stack([jnp.max(p, axis=-1), jnp.max(pm, axis=-1)], axis=-1)
    e_p = e_p / jnp.sum(e_p, -1, keepdims=True)
    weights = g_p[:, None] * e_p
    flat_e = (g_top[:, None] * EXPERTS_PER_GROUP + e_top).astype(jnp.int32).reshape(-1)
    n_assign = T * TOP_K
    onehot = (flat_e[:, None] == jnp.arange(N_EXPERTS, dtype=jnp.int32)[None, :]).astype(jnp.int32)
    csum = jnp.cumsum(onehot, axis=0)
    rank = jnp.take_along_axis(csum, flat_e[:, None], axis=1)[:, 0] - 1
    counts = csum[-1]
    padded = (counts + EXPERT_BLOCK - 1) // EXPERT_BLOCK * EXPERT_BLOCK
    pad_ends = jnp.cumsum(padded)
    pad_starts = pad_ends - padded
    dest = pad_starts[flat_e] + rank
    n_blocks = n_assign // EXPERT_BLOCK + N_EXPERTS + 1
    n_slots = n_blocks * EXPERT_BLOCK
    slot_a = jnp.zeros((n_slots,), jnp.int32).at[dest].set(jnp.arange(n_assign, dtype=jnp.int32))
    block_start = jnp.arange(n_blocks, dtype=jnp.int32) * EXPERT_BLOCK
    block_expert = jnp.minimum(jnp.sum((pad_ends[None, :] <= block_start[:, None]).astype(jnp.int32), axis=1),
                               N_EXPERTS - 1).astype(jnp.int32)
    nvalid = jnp.clip(pad_starts[block_expert] + counts[block_expert] - block_start, 0,
                      EXPERT_BLOCK).astype(jnp.int32)
    nvalid = jnp.where(block_start < pad_ends[-1], nvalid, 0)
    return block_expert, nvalid, slot_a.reshape(n_blocks, 1, EXPERT_BLOCK), weights


def _in_weight(w_in):
    kr0 = MLA_Q_RANK + MLA_KV_RANK
    mq0 = kr0 + MLA_ROPE_DIM
    i0 = mq0 + 4 * D_MLSTM
    half = MLA_ROPE_DIM // 2
    n_small = 2 * MLA_ROPE_DIM + 2 * MLSTM_HEADS
    pad = jnp.zeros(w_in.shape[:-1] + (LANES - n_small,), w_in.dtype)
    return jnp.concatenate(
        [w_in[..., :kr0], w_in[..., mq0:i0], w_in[..., kr0:mq0], w_in[..., kr0 + half:mq0],
         w_in[..., kr0:kr0 + half], w_in[..., i0:i0 + 2 * MLSTM_HEADS], pad], axis=-1).astype(BF16)


def _q_weight(w_qb):
    lead = w_qb.shape[:-1]
    w = w_qb.reshape(lead + (MLA_HEADS, MLA_QK_DIM))
    half = MLA_ROPE_DIM // 2
    z = lambda n: jnp.zeros(lead + (MLA_HEADS, n), w_qb.dtype)
    main = jnp.concatenate([w, z(HEAD_PAD - MLA_QK_DIM)], axis=-1)
    swp = jnp.concatenate([z(MLA_NOPE_DIM), w[..., MLA_NOPE_DIM + half:], w[..., MLA_NOPE_DIM:MLA_NOPE_DIM + half],
                           z(HEAD_PAD - MLA_QK_DIM)], axis=-1)
    hw = MLA_HEADS * HEAD_PAD
    return jnp.concatenate([main.reshape(lead + (hw,)), swp.reshape(lead + (hw,))], axis=-1).astype(BF16)


def _kv_weight(w_kvb_l):
    hw = MLA_HEADS * HEAD_PAD
    w = w_kvb_l.reshape(MLA_KV_RANK, MLA_HEADS, MLA_NOPE_DIM + MLA_V_DIM)
    top_k = jnp.pad(w[:, :, :MLA_NOPE_DIM], ((0, 0), (0, 0), (0, HEAD_PAD - MLA_NOPE_DIM)))
    top_v = jnp.pad(w[:, :, MLA_NOPE_DIM:], ((0, 0), (0, 0), (0, HEAD_PAD - MLA_V_DIM)))
    bot_k = np.zeros((LANES, MLA_HEADS, HEAD_PAD), np.float32)
    bot_v = np.zeros((LANES, MLA_HEADS, HEAD_PAD), np.float32)
    for r in range(MLA_ROPE_DIM):
        bot_k[r, :, MLA_NOPE_DIM + r] = 1.0
    bot_v[_S_ONE, :, MLA_V_DIM] = 1.0
    wk = jnp.concatenate([top_k.reshape(MLA_KV_RANK, hw), jnp.asarray(bot_k).reshape(LANES, hw)], 0)
    wv = jnp.concatenate([top_v.reshape(MLA_KV_RANK, hw), jnp.asarray(bot_v).reshape(LANES, hw)], 0)
    return jnp.concatenate([wk, wv], axis=1).astype(BF16)


def _rope_tables(positions):
    half = MLA_ROPE_DIM // 2
    inv_freq = ROPE_THETA ** (-jnp.arange(half, dtype=F32) / half)
    ang = positions.astype(F32)[:, None] * inv_freq
    cos, sin = jnp.cos(ang), jnp.sin(ang)
    ct = jnp.concatenate([cos, cos], -1)
    st = jnp.concatenate([-sin, sin], -1)
    S = positions.shape[0]
    z = lambda w: jnp.zeros((S, w), F32)
    scale = MLA_QK_DIM ** -0.5 * float(np.log2(np.e))
    tabk = jnp.concatenate([ct, st, z(LANES - 2 * MLA_ROPE_DIM)], -1)
    cqt = jnp.concatenate([jnp.full((S, MLA_NOPE_DIM), scale, F32), ct * scale,
                           z(HEAD_PAD - MLA_QK_DIM)], -1)
    sqt = jnp.concatenate([z(MLA_NOPE_DIM), st * scale, z(HEAD_PAD - MLA_QK_DIM)], -1)
    return tabk, cqt, sqt


def _pick(S, pref):
    t = pref
    while S % t:
        t //= 2
    return t


def kernel(x, positions, w_in, conv_w, conv_b, gate_b, q_a_g, kv_a_g, w_qb, w_kvb, out_g, w_out, ln1_g, ln1_b, w_rg, w_re, w_gate, w_up, w_down, ln2_g, ln2_b):
    B, S, D = x.shape
    assert B == 1 and D == D_MODEL
    depth = w_in.shape[0]
    alpha = float((2 * depth) ** 0.25)
    tm = _pick(S, 512)
    bq = _pick(S, 512)
    L = _pick(S, 256)
    assert tm % SUBLANES == 0 and bq % LANES == 0 and L % LANES == 0

    tabk, cqt, sqt = _rope_tables(positions[0])
    win_all = _in_weight(w_in)
    wq_all = _q_weight(w_qb)
    gb_tab = jnp.zeros((depth, 1, LANES), F32).at[:, 0, _S_I0:_S_I0 + 2 * MLSTM_HEADS].set(gate_b)
    wr_all = jnp.concatenate([w_rg, w_re, jnp.zeros((depth, D_MODEL, LANES - N_GROUPS - N_EXPERTS), F32)], -1)
    wo_all = w_out.astype(BF16)
    n_att = MLA_HEADS * MLA_V_DIM

    xs = x[0]
    for l in range(depth):
        q, k, v, mq, mk, mv, mo, gates = _proj(
            xs, win_all[l], wq_all[l], _kv_weight(w_kvb[l]), q_a_g[l][None], kv_a_g[l][None],
            conv_w[l], conv_b[l][None], gb_tab[l], tabk, cqt, sqt, tm=tm)
        a = _attention(q, k, v, out_g[l][None, :n_att], bq=bq)
        gcol = gates[:, _S_I0:_S_I0 + 2 * MLSTM_HEADS]
        hm = _mlstm(mq, mk, mv, mo, gcol, gcol.T, out_g[l][None, n_att:], L=L)
        x1, x1t, logits = _outproj(xs, a, hm, wo_all[l], ln1_g[l][None], ln1_b[l][None], wr_all[l],
                                   tm=tm, alpha=alpha)
        bexp, nval, slot_a, rw = _route(logits, S)
        y = _moe(x1t, bexp, nval, slot_a, w_gate, w_up, w_down, l)
        xs = _combine(x1, y, rw, ln2_g[l][None], ln2_b[l][None], tm=tm, alpha=alpha)
    return xs[None]
```

```python
import functools

import numpy as np
import jax
import jax.numpy as jnp
from jax import lax
from jax.experimental import pallas as pl
from jax.experimental.pallas import tpu as pltpu

F32 = jnp.float32
BF16 = jnp.bfloat16

D_MODEL = 1024
MLA_HEADS = 8
MLA_V_DIM = 64
MLA_NOPE_DIM = 64
MLA_ROPE_DIM = 32
MLA_QK_DIM = MLA_NOPE_DIM + MLA_ROPE_DIM
MLA_Q_RANK = 256
MLA_KV_RANK = 128
ROPE_THETA = 10000.0
MLSTM_HEADS = 4
MLSTM_HEAD_DIM = 128
D_MLSTM = MLSTM_HEADS * MLSTM_HEAD_DIM
CONV_WIDTH = 4
N_GROUPS = 8
EXPERTS_PER_GROUP = 8
N_EXPERTS = N_GROUPS * EXPERTS_PER_GROUP
D_EXPERT = 256
TOP_K = 2
EXPERT_BLOCK = 128
LN_EPS = 1e-5
RMS_EPS = 1e-6

LANES = 128
SUBLANES = 8
HEAD_PAD = 128
VMEM_LIMIT = 56 * 1024 * 1024

_C_Q0 = 0
_C_KV0 = MLA_Q_RANK
_C_MQK0 = _C_KV0 + MLA_KV_RANK
_C_MV0 = _C_MQK0 + 2 * D_MLSTM
_C_MO0 = _C_MV0 + D_MLSTM
_C_SMALL0 = _C_MO0 + D_MLSTM
D_IN_PAD = _C_SMALL0 + LANES
_S_I0 = 2 * MLA_ROPE_DIM
_S_F0 = _S_I0 + MLSTM_HEADS
_S_ONE = MLA_ROPE_DIM


def _cparams(sem, vmem=VMEM_LIMIT):
    return pltpu.CompilerParams(dimension_semantics=sem, vmem_limit_bytes=vmem)


def _proj_kernel(x_ref, win_ref, wq_ref, wkv_ref, qg_ref, kvg_ref, cw_ref, cb_ref, gb_ref,
                 tabk_ref, cq_ref, sq_ref,
                 q_out, k_out, v_out, mq_out, mk_out, mv_out, mo_out, g_out, carry_ref, *, tm):
    i = pl.program_id(0)

    @pl.when(i == 0)
    def _():
        carry_ref[...] = jnp.zeros_like(carry_ref)

    xb = x_ref[...].astype(BF16)
    p = jnp.dot(xb, win_ref[...], preferred_element_type=F32)

    def rms(v, g):
        return v * lax.rsqrt(jnp.mean(v * v, axis=-1, keepdims=True) + RMS_EPS) * g

    cqn = rms(p[:, _C_Q0:_C_KV0], qg_ref[...]).astype(BF16)
    ckvn = rms(p[:, _C_KV0:_C_MQK0], kvg_ref[...]).astype(BF16)

    qq = jnp.dot(cqn, wq_ref[...], preferred_element_type=F32)
    cq = cq_ref[...]
    sq = sq_ref[...]
    hw = MLA_HEADS * HEAD_PAD
    for h in range(MLA_HEADS):
        a = qq[:, h * HEAD_PAD:(h + 1) * HEAD_PAD]
        b = qq[:, hw + h * HEAD_PAD: hw + (h + 1) * HEAD_PAD]
        q_out[:, h * HEAD_PAD:(h + 1) * HEAD_PAD] = (a * cq + b * sq).astype(BF16)

    small = p[:, _C_SMALL0:D_IN_PAD]
    lane = lax.broadcasted_iota(jnp.int32, small.shape, 1)
    prod = small * tabk_ref[...]
    kr = prod + pltpu.roll(prod, LANES - MLA_ROPE_DIM, 1)
    kvs = jnp.where(lane < MLA_ROPE_DIM, kr, jnp.where(lane == _S_ONE, 1.0, 0.0))
    kvin = jnp.concatenate([ckvn, kvs.astype(BF16)], axis=1)
    kv = jnp.dot(kvin, wkv_ref[...], preferred_element_type=F32)
    k_out[...] = kv[:, :hw].astype(BF16)
    v_out[...] = kv[:, hw:].T.astype(BF16)

    g = small + gb_ref[...]
    lsig = jnp.minimum(g, 0.0) - jnp.log1p(jnp.exp(-jnp.abs(g)))
    is_f = (lane >= _S_F0) & (lane < _S_F0 + MLSTM_HEADS)
    g_out[...] = jnp.where(is_f, lsig, g)

    mqk = p[:, _C_MQK0:_C_MV0]
    ext = jnp.concatenate([carry_ref[...], mqk], axis=0)
    carry_ref[...] = mqk[tm - SUBLANES:tm, :]
    acc = cb_ref[...] + mqk * cw_ref[CONV_WIDTH - 1:CONV_WIDTH, :]
    for j in range(CONV_WIDTH - 1):
        off = SUBLANES - (CONV_WIDTH - 1) + j
        acc = acc + ext[off:off + tm, :] * cw_ref[j:j + 1, :]
    y = acc * (1.0 / (1.0 + jnp.exp(-acc)))
    mq_out[...] = (y[:, :D_MLSTM] * (MLSTM_HEAD_DIM ** -0.5)).astype(BF16)
    mk_out[...] = y[:, D_MLSTM:].astype(BF16)
    mv_out[...] = p[:, _C_MV0:_C_MO0].astype(BF16)
    mo = p[:, _C_MO0:_C_SMALL0]
    mo_out[...] = (1.0 / (1.0 + jnp.exp(-mo))).astype(BF16)


def _proj(x, win, wq, wkv, qg, kvg, cw, cb, gb, tabk, cqt, sqt, *, tm):
    S = x.shape[0]
    hw = MLA_HEADS * HEAD_PAD
    row = lambda w: pl.BlockSpec((tm, w), lambda i: (i, 0))
    full = lambda a: pl.BlockSpec(a.shape, lambda i: (0,) * a.ndim)
    out_shapes = (
        jax.ShapeDtypeStruct((S, hw), BF16), jax.ShapeDtypeStruct((S, hw), BF16),
        jax.ShapeDtypeStruct((hw, S), BF16),
        jax.ShapeDtypeStruct((S, D_MLSTM), BF16), jax.ShapeDtypeStruct((S, D_MLSTM), BF16),
        jax.ShapeDtypeStruct((S, D_MLSTM), BF16), jax.ShapeDtypeStruct((S, D_MLSTM), BF16),
        jax.ShapeDtypeStruct((S, LANES), F32))
    return pl.pallas_call(
        functools.partial(_proj_kernel, tm=tm),
        grid=(S // tm,),
        in_specs=[row(D_MODEL), full(win), full(wq), full(wkv), full(qg), full(kvg), full(cw),
                  full(cb), full(gb), row(LANES), row(LANES), row(LANES)],
        out_specs=(row(hw), row(hw), pl.BlockSpec((hw, tm), lambda i: (0, i)), row(D_MLSTM), row(D_MLSTM), row(D_MLSTM),
                   row(D_MLSTM), row(LANES)),
        out_shape=out_shapes,
        scratch_shapes=[pltpu.VMEM((SUBLANES, 2 * D_MLSTM), F32)],
        compiler_params=_cparams(("arbitrary",)),
        name="proj",
    )(x, win, wq, wkv, qg, kvg, cw, cb, gb, tabk, cqt, sqt)


VT_ROWS = 80


def _attn_kernel(q_ref, k_ref, vt_ref, og_ref, o_ref, s_ref, *, bq, heads_per_step):
    qi = pl.program_id(1)
    nh = heads_per_step
    cols = [(h * HEAD_PAD, (h + 1) * HEAD_PAD) for h in range(nh)]
    qs = [q_ref[:, c0:c1] for c0, c1 in cols]

    def scores(j, h):
        kt = k_ref[pl.ds(pl.multiple_of(j * bq, bq), bq), cols[h][0]:cols[h][1]]
        return lax.dot_general(kt, qs[h], (((1,), (1,)), ((), ())), preferred_element_type=F32)

    def accumulate(j, h, s, m, acc):
        vt = vt_ref[h * HEAD_PAD:h * HEAD_PAD + VT_ROWS, pl.ds(pl.multiple_of(j * bq, bq), bq)]
        m_new = jnp.maximum(m, jnp.max(s, axis=0, keepdims=True))
        alpha = jnp.exp2(m - m_new)
        p = jnp.exp2(s - m_new)
        return m_new, alpha * acc + jnp.dot(vt, p.astype(BF16), preferred_element_type=F32)

    def step(j, carry, cur, diagonal=False, prefetch=True):
        new = []
        for h in range(nh):
            m, acc = carry[2 * h:2 * h + 2]
            if prefetch:
                s_ref[(1 - cur) * nh + h] = scores(j + 1, h)
            s = s_ref[cur * nh + h]
            if diagonal:
                key = lax.broadcasted_iota(jnp.int32, (bq, bq), 0)
                qry = lax.broadcasted_iota(jnp.int32, (bq, bq), 1)
                s = jnp.where(key <= qry, s, -jnp.inf)
            new += list(accumulate(j, h, s, m, acc))
        return tuple(new)

    def pair(i, carry):
        return step(2 * i + 1, step(2 * i, carry, 0), 1)

    init = []
    for h in range(nh):
        s_ref[h] = scores(0, h)
        init += [jnp.full((1, bq), -jnp.inf, F32), jnp.zeros((VT_ROWS, bq), F32)]
    carry = lax.fori_loop(0, qi // 2, pair, tuple(init))
    carry = lax.cond(
        qi % 2 == 1,
        lambda cr: step(qi, step(qi - 1, cr, 0), 1, diagonal=True, prefetch=False),
        lambda cr: step(qi, cr, 0, diagonal=True, prefetch=False),
        carry)
    outs = []
    for h in range(nh):
        acc = carry[2 * h + 1]
        o = acc[:MLA_V_DIM, :] / acc[MLA_V_DIM:MLA_V_DIM + 1, :]
        o = o * lax.rsqrt(jnp.mean(o * o, axis=0, keepdims=True) + RMS_EPS)
        outs.append(o.T)
    o_ref[...] = (jnp.concatenate(outs, axis=1) * og_ref[...]).astype(BF16)


def _attention(q, k, vt, og, *, bq, heads_per_step=2):
    S = q.shape[0]
    n_hp = MLA_HEADS // heads_per_step
    wq = heads_per_step * HEAD_PAD
    wo = heads_per_step * MLA_V_DIM
    return pl.pallas_call(
        functools.partial(_attn_kernel, bq=bq, heads_per_step=heads_per_step),
        grid=(n_hp, S // bq),
        in_specs=[pl.BlockSpec((bq, wq), lambda h, i: (i, h)),
                  pl.BlockSpec((S, wq), lambda h, i: (0, h)),
                  pl.BlockSpec((wq, S), lambda h, i: (h, 0)),
                  pl.BlockSpec((1, wo), lambda h, i: (0, h))],
        out_specs=pl.BlockSpec((bq, wo), lambda h, i: (i, h)),
        out_shape=jax.ShapeDtypeStruct((S, MLA_HEADS * MLA_V_DIM), BF16),
        scratch_shapes=[pltpu.VMEM((2 * heads_per_step, bq, bq), F32)],
        compiler_params=_cparams(("arbitrary", "arbitrary")),
        name="attention",
    )(q, k, vt, og)


def _mlstm_kernel(q_ref, k_ref, v_ref, o_ref, gc_ref, gr_ref, og_ref, h_out, c_ref, m_ref, *, L):
    ci = pl.program_id(0)

    @pl.when(ci == 0)
    def _():
        c_ref[...] = jnp.zeros_like(c_ref)
        m_ref[...] = jnp.zeros_like(m_ref)

    r = lax.broadcasted_iota(jnp.int32, (L, L), 0)
    c = lax.broadcasted_iota(jnp.int32, (L, L), 1)
    tri = c <= r
    lane = lax.broadcasted_iota(jnp.int32, (L, LANES), 1)
    ones_blk = jnp.where(lane == 0, 1.0, 0.0).astype(BF16)
    gc = gc_ref[...]
    gr = gr_ref[...]
    d = MLSTM_HEAD_DIM
    for h in range(MLSTM_HEADS):
        q = q_ref[:, h * d:(h + 1) * d]
        k = k_ref[:, h * d:(h + 1) * d]
        v = v_ref[:, h * d:(h + 1) * d]
        li_col = gc[:, h:h + 1]
        lf_col = gc[:, MLSTM_HEADS + h:MLSTM_HEADS + h + 1]
        li_row = gr[h:h + 1, :]
        lf_row = gr[MLSTM_HEADS + h:MLSTM_HEADS + h + 1, :]
        b_col = jnp.sum(jnp.where(tri, lf_row, 0.0), axis=1, keepdims=True)
        b_row = jnp.sum(jnp.where(r <= c, lf_col, 0.0), axis=0, keepdims=True)
        m_prev = m_ref[h][0:1, 0:1]
        log_d = jnp.where(tri, b_col - b_row + li_row, -jnp.inf)
        log_inter = b_col + m_prev
        m_t = jnp.maximum(log_inter, jnp.max(log_d, axis=1, keepdims=True))
        w_intra = jnp.exp(log_d - m_t)
        w_inter = jnp.exp(log_inter - m_t)
        s = lax.dot_general(q, k, (((1,), (1,)), ((), ())), preferred_element_type=F32) * w_intra
        vaug = jnp.concatenate([v, ones_blk], axis=1)
        r1 = jnp.dot(s.astype(BF16), vaug, preferred_element_type=F32)
        cst = c_ref[h]
        r2 = jnp.dot(q, cst.astype(BF16), preferred_element_type=F32)
        num = r1[:, :d] + w_inter * r2[:, :d]
        den = r1[:, d:d + 1] + w_inter * r2[:, d:d + 1]
        hv = num / jnp.maximum(jnp.abs(den), jnp.exp(-m_t))
        b_end = b_col[L - 1:L, :]
        log_w = b_end - b_col + li_col
        m_new = jnp.maximum(b_end + m_prev, jnp.max(log_w, axis=0, keepdims=True))
        w_s = jnp.exp(log_w - m_new)
        decay = jnp.exp(b_end + m_prev - m_new)
        kw = (k.astype(F32) * w_s).astype(BF16)
        upd = lax.dot_general(kw, vaug, (((0,), (0,)), ((), ())), preferred_element_type=F32)
        c_ref[h] = decay * cst + upd
        m_ref[h] = jnp.broadcast_to(m_new, (SUBLANES, LANES))
        hn = hv * lax.rsqrt(jnp.mean(hv * hv, axis=-1, keepdims=True) + RMS_EPS)
        gate = o_ref[:, h * d:(h + 1) * d].astype(F32)
        h_out[:, h * d:(h + 1) * d] = (hn * gate * og_ref[:, h * d:(h + 1) * d]).astype(BF16)


def _mlstm(mq, mk, mv, mo, gcol, grow, og, *, L):
    S = mq.shape[0]
    d = MLSTM_HEAD_DIM
    row = pl.BlockSpec((L, D_MLSTM), lambda i: (i, 0))
    return pl.pallas_call(
        functools.partial(_mlstm_kernel, L=L),
        grid=(S // L,),
        in_specs=[row, row, row, row,
                  pl.BlockSpec((L, 2 * MLSTM_HEADS), lambda i: (i, 0)),
                  pl.BlockSpec((2 * MLSTM_HEADS, L), lambda i: (0, i)),
                  pl.BlockSpec((1, D_MLSTM), lambda i: (0, 0))],
        out_specs=row,
        out_shape=jax.ShapeDtypeStruct((S, D_MLSTM), BF16),
        scratch_shapes=[pltpu.VMEM((MLSTM_HEADS, d, 2 * d), F32),
                        pltpu.VMEM((MLSTM_HEADS, SUBLANES, LANES), F32)],
        compiler_params=_cparams(("arbitrary",)),
        name="mlstm",
    )(mq, mk, mv, mo, gcol, grow, og)


def _layer_norm(z, g, b):
    mu = jnp.mean(z, axis=-1, keepdims=True)
    zc = z - mu
    var = jnp.mean(zc * zc, axis=-1, keepdims=True)
    return zc * lax.rsqrt(var + LN_EPS) * g + b


def _outproj_kernel(x_ref, a_ref, h_ref, wo_ref, g_ref, b_ref, wr_ref, x1_out, x1t_out, lg_out, *, alpha, tm):
    mix = jnp.concatenate([a_ref[...], h_ref[...]], axis=1)
    y = jnp.dot(mix, wo_ref[...], preferred_element_type=F32)
    x1 = _layer_norm(alpha * x_ref[...] + y, g_ref[...], b_ref[...])
    x1_out[...] = x1
    for c in range(SUBLANES):
        x1t_out[pl.ds(c, tm, stride=SUBLANES), :] = x1[:, c * LANES:(c + 1) * LANES]
    lg_out[...] = jnp.dot(x1, wr_ref[...], preferred_element_type=F32,
                          precision=lax.Precision.HIGHEST)


def _outproj(x, a, hm, wo, g, b, wr, *, tm, alpha):
    S = x.shape[0]
    row = lambda w: pl.BlockSpec((tm, w), lambda i: (i, 0))
    full = lambda arr: pl.BlockSpec(arr.shape, lambda i: (0,) * arr.ndim)
    return pl.pallas_call(
        functools.partial(_outproj_kernel, alpha=alpha, tm=tm),
        grid=(S // tm,),
        in_specs=[row(D_MODEL), row(MLA_HEADS * MLA_V_DIM), row(D_MLSTM), full(wo), full(g), full(b),
                  full(wr)],
        out_specs=(row(D_MODEL), pl.BlockSpec((tm * SUBLANES, LANES), lambda i: (i, 0)), row(LANES)),
        out_shape=(jax.ShapeDtypeStruct((S, D_MODEL), F32),
                   jax.ShapeDtypeStruct((S * SUBLANES, LANES), F32),
                   jax.ShapeDtypeStruct((S, LANES), F32)),
        compiler_params=_cparams(("arbitrary",)),
        name="outproj",
    )(x, a, hm, wo, g, b, wr)


TOKEN_TILE = SUBLANES
DMA_GROUP = 8


def _rows_from_tiles(ref, n):
    return jnp.concatenate([ref[pl.ds(c, n, stride=TOKEN_TILE), :] for c in range(TOKEN_TILE)], axis=1)


def _rows_to_tiles(ref, v):
    for c in range(TOKEN_TILE):
        ref[pl.ds(c, v.shape[0], stride=TOKEN_TILE), :] = v[:, c * LANES:(c + 1) * LANES]


def _pow2_chunks(n, top, fn):
    off = 0
    bit = top
    while bit:
        @pl.when((n & bit) != 0)
        def _(bit=bit, off=off):
            fn(off, bit)
        off = off + (n & bit)
        bit //= 2


def _dispatch_kernel(cnt_ref, pst_ref, d_ref, x_hbm, xs_hbm, zbuf, sem, zsem, *, tt):
    i = pl.program_id(0)
    n = pl.num_programs(0)
    G = TOKEN_TILE
    slot = lax.rem(i, 2)

    def row_copy(src_row, dst_row, rows, s):
        return pltpu.make_async_copy(x_hbm.at[pl.ds(src_row * G, rows * G)],
                                     xs_hbm.at[pl.ds(dst_row * G, rows * G)], s)

    def zero_copy(dst_row, rows):
        return pltpu.make_async_copy(zbuf.at[pl.ds(0, rows * G)], xs_hbm.at[pl.ds(dst_row * G, rows * G)], zsem)

    @pl.when(i == 0)
    def _():
        zbuf[...] = jnp.zeros_like(zbuf)

        def fill(e, wait):
            pad = (-cnt_ref[e]) & (EXPERT_BLOCK - 1)
            first = pst_ref[e] + cnt_ref[e]
            if wait:
                _pow2_chunks(pad, EXPERT_BLOCK // 2, lambda off, size: zero_copy(0, size).wait())
            else:
                _pow2_chunks(pad, EXPERT_BLOCK // 2, lambda off, size: zero_copy(first + off, size).start())
            return 0
        lax.fori_loop(0, N_EXPERTS, lambda e, c: fill(e, False), 0)
        lax.fori_loop(0, N_EXPERTS, lambda e, c: fill(e, True), 0)
        half = EXPERT_BLOCK // 2
        used = (pst_ref[N_EXPERTS - 1] + cnt_ref[N_EXPERTS - 1] + EXPERT_BLOCK - 1) // EXPERT_BLOCK * 2
        n_half = xs_hbm.shape[0] // (half * G)

        def tail(c, wait):
            if wait:
                zero_copy(0, half).wait()
            else:
                zero_copy(c * half, half).start()
            return 0
        lax.fori_loop(used, n_half, lambda c, _: tail(c, False), 0)
        lax.fori_loop(used, n_half, lambda c, _: tail(c, True), 0)

    base = i * tt
    for t0 in range(0, tt, DMA_GROUP):
        idx = [d_ref[0, 0, j] for j in range(TOP_K * t0, TOP_K * (t0 + DMA_GROUP))]
        for j, dst in enumerate(idx):
            row_copy(base + t0 + j // TOP_K, dst, 1, sem.at[slot]).start()

    @pl.when(i >= 1)
    def _():
        row_copy(0, 0, TOP_K * tt, sem.at[1 - slot]).wait()

    @pl.when(i == n - 1)
    def _():
        row_copy(0, 0, TOP_K * tt, sem.at[slot]).wait()


def _dispatch(x1t, dest, counts, pad_starts, n_slots, *, tt):
    G = TOKEN_TILE
    T = x1t.shape[0] // G
    nt = T // tt
    grid_spec = pltpu.PrefetchScalarGridSpec(
        num_scalar_prefetch=2,
        grid=(nt,),
        in_specs=[pl.BlockSpec((1, 1, TOP_K * tt), lambda i, c, p: (i, 0, 0), memory_space=pltpu.SMEM),
                  pl.BlockSpec(memory_space=pl.ANY)],
        out_specs=pl.BlockSpec(memory_space=pl.ANY),
        scratch_shapes=[pltpu.VMEM((EXPERT_BLOCK // 2 * G, LANES), F32),
                        pltpu.SemaphoreType.DMA((2,)), pltpu.SemaphoreType.DMA])
    return pl.pallas_call(
        functools.partial(_dispatch_kernel, tt=tt),
        grid_spec=grid_spec,
        out_shape=jax.ShapeDtypeStruct((n_slots * G, LANES), F32),
        compiler_params=_cparams(("arbitrary",)),
        name="dispatch",
    )(counts, pad_starts, dest.reshape(nt, 1, TOP_K * tt), x1t)


def _moe_kernel(bexp_ref, nvb_ref, xs_ref, wg_ref, wu_ref, wd_ref, ys_ref, wgb, wub, wdb):
    b = pl.program_id(0)

    @pl.when((b == 0) | (bexp_ref[b] != bexp_ref[jnp.maximum(b - 1, 0)]))
    def _():
        wgb[...] = wg_ref[0, 0].astype(BF16)
        wub[...] = wu_ref[0, 0].astype(BF16)
        wdb[...] = wd_ref[0, 0].astype(BF16)

    @pl.when(b < nvb_ref[0])
    def _():
        xs = _rows_from_tiles(xs_ref, EXPERT_BLOCK).astype(BF16)
        g = jnp.dot(xs, wgb[...], preferred_element_type=F32)
        u = jnp.dot(xs, wub[...], preferred_element_type=F32)
        hid = (g * (1.0 / (1.0 + jnp.exp(-g)))) * u
        _rows_to_tiles(ys_ref, jnp.dot(hid.astype(BF16), wdb[...], preferred_element_type=F32))

    @pl.when(b >= nvb_ref[0])
    def _():
        ys_ref[...] = jnp.zeros_like(ys_ref)


def _moe(xs, block_expert, n_valid_blocks, wg, wu, wd, layer):
    G = TOKEN_TILE
    assert G * LANES == D_MODEL
    R = EXPERT_BLOCK
    nb = xs.shape[0] // (R * G)
    blk = lambda b, be, nvb: (b, 0)
    wspec = lambda shp: pl.BlockSpec((1, 1) + shp, lambda b, be, nvb: (layer, be[b], 0, 0))
    grid_spec = pltpu.PrefetchScalarGridSpec(
        num_scalar_prefetch=2,
        grid=(nb,),
        in_specs=[pl.BlockSpec((R * G, LANES), blk),
                  wspec((D_MODEL, D_EXPERT)), wspec((D_MODEL, D_EXPERT)), wspec((D_EXPERT, D_MODEL))],
        out_specs=pl.BlockSpec((R * G, LANES), blk),
        scratch_shapes=[pltpu.VMEM((D_MODEL, D_EXPERT), BF16), pltpu.VMEM((D_MODEL, D_EXPERT), BF16),
                        pltpu.VMEM((D_EXPERT, D_MODEL), BF16)])
    return pl.pallas_call(
        _moe_kernel,
        grid_spec=grid_spec,
        out_shape=jax.ShapeDtypeStruct(xs.shape, F32),
        compiler_params=_cparams(("arbitrary",)),
        name="experts",
    )(block_expert, n_valid_blocks, xs, wg, wu, wd)


def _combine_kernel(d_ref, dn_ref, x_ref, w_ref, g_ref, b_ref, ys_hbm, o_ref, ybuf, sem, *, alpha, tt):
    i = pl.program_id(0)
    n = pl.num_programs(0)
    G = TOKEN_TILE
    slot = lax.rem(i, 2)

    def row_copy(src_row, r, rows, sl):
        return pltpu.make_async_copy(ys_hbm.at[pl.ds(src_row * G, rows * G)],
                                     ybuf.at[sl, pl.ds(r * G, rows * G)], sem.at[sl])

    def fetch(idx_ref, sl):
        for t0 in range(0, tt, DMA_GROUP):
            idx = [idx_ref[0, 0, j] for j in range(TOP_K * t0, TOP_K * (t0 + DMA_GROUP))]
            for j, src in enumerate(idx):
                row_copy(src, (j % TOP_K) * tt + t0 + j // TOP_K, 1, sl).start()

    @pl.when(i == 0)
    def _():
        fetch(d_ref, 0)

    for sl in range(2):
        @pl.when(slot == sl)
        def _(sl=sl):
            row_copy(0, 0, TOP_K * tt, sl).wait()
            fetch(dn_ref, 1 - sl)
            w = w_ref[...]
            y = (_rows_from_tiles(ybuf.at[sl, pl.ds(0, tt * G)], tt) * w[:, 0:1]
                 + _rows_from_tiles(ybuf.at[sl, pl.ds(tt * G, tt * G)], tt) * w[:, 1:2])
            o_ref[...] = _layer_norm(alpha * x_ref[...] + y, g_ref[...], b_ref[...])

    @pl.when(i == n - 1)
    def _():
        row_copy(0, 0, TOP_K * tt, 1 - slot).wait()


def _combine(x1, ys, dest, w, g, b, *, tt, alpha):
    S = x1.shape[0]
    nt = S // tt
    G = TOKEN_TILE
    d3 = dest.reshape(nt, 1, TOP_K * tt)
    smem_blk = lambda f: pl.BlockSpec((1, 1, TOP_K * tt), f, memory_space=pltpu.SMEM)
    return pl.pallas_call(
        functools.partial(_combine_kernel, alpha=alpha, tt=tt),
        grid=(nt,),
        in_specs=[smem_blk(lambda i: (i, 0, 0)),
                  smem_blk(lambda i: (jnp.minimum(i + 1, nt - 1), 0, 0)),
                  pl.BlockSpec((tt, D_MODEL), lambda i: (i, 0)),
                  pl.BlockSpec((tt, TOP_K), lambda i: (i, 0)),
                  pl.BlockSpec((1, D_MODEL), lambda i: (0, 0)),
                  pl.BlockSpec((1, D_MODEL), lambda i: (0, 0)),
                  pl.BlockSpec(memory_space=pl.ANY)],
        out_specs=pl.BlockSpec((tt, D_MODEL), lambda i: (i, 0)),
        out_shape=jax.ShapeDtypeStruct((S, D_MODEL), F32),
        scratch_shapes=[pltpu.VMEM((2, TOP_K * tt * G, LANES), F32), pltpu.SemaphoreType.DMA((2,))],
        compiler_params=_cparams(("arbitrary",)),
        name="combine",
    )(d3, d3, x1, w, g, b, ys)


def _route(logits, T):
    g_logits = logits[:, :N_GROUPS]
    g_prob = jax.nn.softmax(g_logits, axis=-1)
    g_top = jnp.argmax(g_prob, axis=-1)
    g_p = jnp.max(g_prob, axis=-1)
    e_logits = logits[:, N_GROUPS:N_GROUPS + N_EXPERTS].reshape(T, N_GROUPS, EXPERTS_PER_GROUP)
    e_in = jnp.take_along_axis(e_logits, g_top[:, None, None], axis=1)[:, 0, :]
    p = jax.nn.softmax(e_in, axis=-1)
    i1 = jnp.argmax(p, axis=-1)
    pm = jnp.where(jnp.arange(EXPERTS_PER_GROUP)[None, :] == i1[:, None], -jnp.inf, p)
    i2 = jnp.argmax(pm, axis=-1)
    e_top = jnp.stack([i1, i2], axis=-1)
    e_p = jnp.stack([jnp.max(p, axis=-1), jnp.max(pm, axis=-1)], axis=-1)
    e_p = e_p / jnp.sum(e_p, -1, keepdims=True)
    weights = g_p[:, None] * e_p
    flat_e = (g_top[:, None] * EXPERTS_PER_GROUP + e_top).astype(jnp.int32).reshape(-1)
    n_assign = T * TOP_K
    onehot = (flat_e[:, None] == jnp.arange(N_EXPERTS, dtype=jnp.int32)[None, :]).astype(jnp.int32)
    csum = jnp.cumsum(onehot, axis=0)
    rank = jnp.take_along_axis(csum, flat_e[:, None], axis=1)[:, 0] - 1
    counts = csum[-1]
    padded = (counts + EXPERT_BLOCK - 1) // EXPERT_BLOCK * EXPERT_BLOCK
    pad_ends = jnp.cumsum(padded)
    pad_starts = pad_ends - padded
    dest = (pad_starts[flat_e] + rank).astype(jnp.int32)
    n_blocks = n_assign // EXPERT_BLOCK + N_EXPERTS
    block_start = jnp.arange(n_blocks, dtype=jnp.int32) * EXPERT_BLOCK
    block_expert = jnp.minimum(jnp.sum((pad_ends[None, :] <= block_start[:, None]).astype(jnp.int32), axis=1),
                               N_EXPERTS - 1).astype(jnp.int32)
    n_valid_blocks = (pad_ends[-1:] // EXPERT_BLOCK).astype(jnp.int32)
    return (dest, counts.astype(jnp.int32), pad_starts.astype(jnp.int32), block_expert, n_valid_blocks,
            n_blocks * EXPERT_BLOCK, weights)


def _in_weight(w_in):
    kr0 = MLA_Q_RANK + MLA_KV_RANK
    mq0 = kr0 + MLA_ROPE_DIM
    i0 = mq0 + 4 * D_MLSTM
    half = MLA_ROPE_DIM // 2
    n_small = 2 * MLA_ROPE_DIM + 2 * MLSTM_HEADS
    pad = jnp.zeros(w_in.shape[:-1] + (LANES - n_small,), w_in.dtype)
    return jnp.concatenate(
        [w_in[..., :kr0], w_in[..., mq0:i0], w_in[..., kr0:mq0], w_in[..., kr0 + half:mq0],
         w_in[..., kr0:kr0 + half], w_in[..., i0:i0 + 2 * MLSTM_HEADS], pad], axis=-1).astype(BF16)


def _q_weight(w_qb):
    lead = w_qb.shape[:-1]
    w = w_qb.reshape(lead + (MLA_HEADS, MLA_QK_DIM))
    half = MLA_ROPE_DIM // 2
    z = lambda n: jnp.zeros(lead + (MLA_HEADS, n), w_qb.dtype)
    main = jnp.concatenate([w, z(HEAD_PAD - MLA_QK_DIM)], axis=-1)
    swp = jnp.concatenate([z(MLA_NOPE_DIM), w[..., MLA_NOPE_DIM + half:], w[..., MLA_NOPE_DIM:MLA_NOPE_DIM + half],
                           z(HEAD_PAD - MLA_QK_DIM)], axis=-1)
    hw = MLA_HEADS * HEAD_PAD
    return jnp.concatenate([main.reshape(lead + (hw,)), swp.reshape(lead + (hw,))], axis=-1).astype(BF16)


def _kv_weight(w_kvb_l):
    hw = MLA_HEADS * HEAD_PAD
    w = w_kvb_l.reshape(MLA_KV_RANK, MLA_HEADS, MLA_NOPE_DIM + MLA_V_DIM)
    top_k = jnp.pad(w[:, :, :MLA_NOPE_DIM], ((0, 0), (0, 0), (0, HEAD_PAD - MLA_NOPE_DIM)))
    top_v = jnp.pad(w[:, :, MLA_NOPE_DIM:], ((0, 0), (0, 0), (0, HEAD_PAD - MLA_V_DIM)))
    bot_k = np.zeros((LANES, MLA_HEADS, HEAD_PAD), np.float32)
    bot_v = np.zeros((LANES, MLA_HEADS, HEAD_PAD), np.float32)
    for r in range(MLA_ROPE_DIM):
        bot_k[r, :, MLA_NOPE_DIM + r] = 1.0
    bot_v[_S_ONE, :, MLA_V_DIM] = 1.0
    wk = jnp.concatenate([top_k.reshape(MLA_KV_RANK, hw), jnp.asarray(bot_k).reshape(LANES, hw)], 0)
    wv = jnp.concatenate([top_v.reshape(MLA_KV_RANK, hw), jnp.asarray(bot_v).reshape(LANES, hw)], 0)
    return jnp.concatenate([wk, wv], axis=1).astype(BF16)


def _rope_tables(positions):
    half = MLA_ROPE_DIM // 2
    inv_freq = ROPE_THETA ** (-jnp.arange(half, dtype=F32) / half)
    ang = positions.astype(F32)[:, None] * inv_freq
    cos, sin = jnp.cos(ang), jnp.sin(ang)
    ct = jnp.concatenate([cos, cos], -1)
    st = jnp.concatenate([-sin, sin], -1)
    S = positions.shape[0]
    z = lambda w: jnp.zeros((S, w), F32)
    scale = MLA_QK_DIM ** -0.5 * float(np.log2(np.e))
    tabk = jnp.concatenate([ct, st, z(LANES - 2 * MLA_ROPE_DIM)], -1)
    cqt = jnp.concatenate([jnp.full((S, MLA_NOPE_DIM), scale, F32), ct * scale,
                           z(HEAD_PAD - MLA_QK_DIM)], -1)
    sqt = jnp.concatenate([z(MLA_NOPE_DIM), st * scale, z(HEAD_PAD - MLA_QK_DIM)], -1)
    return tabk, cqt, sqt


def _pick(S, pref):
    t = pref
    while S % t:
        t //= 2
    return t


def kernel(x, positions, w_in, conv_w, conv_b, gate_b, q_a_g, kv_a_g, w_qb, w_kvb, out_g, w_out, ln1_g, ln1_b, w_rg, w_re, w_gate, w_up, w_down, ln2_g, ln2_b):
    B, S, D = x.shape
    assert B == 1 and D == D_MODEL
    depth = w_in.shape[0]
    alpha = float((2 * depth) ** 0.25)
    tm = _pick(S, 512)
    bq = _pick(S, 512)
    L = _pick(S, 256)
    tt = _pick(S, 256)
    assert tm % SUBLANES == 0 and bq % LANES == 0 and L % LANES == 0

    tabk, cqt, sqt = _rope_tables(positions[0])
    win_all = _in_weight(w_in)
    wq_all = _q_weight(w_qb)
    gb_tab = jnp.zeros((depth, 1, LANES), F32).at[:, 0, _S_I0:_S_I0 + 2 * MLSTM_HEADS].set(gate_b)
    wr_all = jnp.concatenate([w_rg, w_re, jnp.zeros((depth, D_MODEL, LANES - N_GROUPS - N_EXPERTS), F32)], -1)
    wo_all = w_out.astype(BF16)
    n_att = MLA_HEADS * MLA_V_DIM

    xs = x[0]
    for l in range(depth):
        q, k, v, mq, mk, mv, mo, gates = _proj(
            xs, win_all[l], wq_all[l], _kv_weight(w_kvb[l]), q_a_g[l][None], kv_a_g[l][None],
            conv_w[l], conv_b[l][None], gb_tab[l], tabk, cqt, sqt, tm=tm)
        a = _attention(q, k, v, out_g[l][None, :n_att], bq=bq)
        gcol = gates[:, _S_I0:_S_I0 + 2 * MLSTM_HEADS]
        hm = _mlstm(mq, mk, mv, mo, gcol, gcol.T, out_g[l][None, n_att:], L=L)
        x1, x1t, logits = _outproj(xs, a, hm, wo_all[l], ln1_g[l][None], ln1_b[l][None], wr_all[l],
                                   tm=tm, alpha=alpha)
        dest, counts, pad_starts, bexp, nvb, n_slots, rw = _route(logits, S)
        xsort = _dispatch(x1t, dest, counts, pad_starts, n_slots, tt=tt)
        ysort = _moe(xsort, bexp, nvb, w_gate, w_up, w_down, l)
        xs = _combine(x1, ysort, dest, rw, ln2_g[l][None], ln2_b[l][None], tt=tt, alpha=alpha)
    return xs[None]
```

```python
import functools

import numpy as np
import jax
import jax.numpy as jnp
from jax import lax
from jax.experimental import pallas as pl
from jax.experimental.pallas import tpu as pltpu

F32 = jnp.float32
BF16 = jnp.bfloat16

D_MODEL = 1024
MLA_HEADS = 8
MLA_V_DIM = 64
MLA_NOPE_DIM = 64
MLA_ROPE_DIM = 32
MLA_QK_DIM = MLA_NOPE_DIM + MLA_ROPE_DIM
MLA_Q_RANK = 256
MLA_KV_RANK = 128
ROPE_THETA = 10000.0
MLSTM_HEADS = 4
MLSTM_HEAD_DIM = 128
D_MLSTM = MLSTM_HEADS * MLSTM_HEAD_DIM
CONV_WIDTH = 4
N_GROUPS = 8
EXPERTS_PER_GROUP = 8
N_EXPERTS = N_GROUPS * EXPERTS_PER_GROUP
D_EXPERT = 256
TOP_K = 2
EXPERT_BLOCK = 128
LN_EPS = 1e-5
RMS_EPS = 1e-6

LANES = 128
SUBLANES = 8
HEAD_PAD = 128
VMEM_LIMIT = 56 * 1024 * 1024

_C_Q0 = 0
_C_KV0 = MLA_Q_RANK
_C_MQK0 = _C_KV0 + MLA_KV_RANK
_C_MV0 = _C_MQK0 + 2 * D_MLSTM
_C_MO0 = _C_MV0 + D_MLSTM
_C_SMALL0 = _C_MO0 + D_MLSTM
D_IN_PAD = _C_SMALL0 + LANES
_S_I0 = 2 * MLA_ROPE_DIM
_S_F0 = _S_I0 + MLSTM_HEADS
_S_ONE = MLA_ROPE_DIM


def _cparams(sem, vmem=VMEM_LIMIT):
    return pltpu.CompilerParams(dimension_semantics=sem, vmem_limit_bytes=vmem)


def _proj_kernel(x_ref, win_ref, wq_ref, wkv_ref, qg_ref, kvg_ref, cw_ref, cb_ref, gb_ref,
                 tabk_ref, cq_ref, sq_ref,
                 q_out, k_out, v_out, mq_out, mk_out, mv_out, mo_out, g_out, carry_ref, *, tm):
    i = pl.program_id(0)

    @pl.when(i == 0)
    def _():
        carry_ref[...] = jnp.zeros_like(carry_ref)

    xb = x_ref[...].astype(BF16)
    p = jnp.dot(xb, win_ref[...], preferred_element_type=F32)

    def rms(v, g):
        return v * lax.rsqrt(jnp.mean(v * v, axis=-1, keepdims=True) + RMS_EPS) * g

    cqn = rms(p[:, _C_Q0:_C_KV0], qg_ref[...]).astype(BF16)
    ckvn = rms(p[:, _C_KV0:_C_MQK0], kvg_ref[...]).astype(BF16)

    qq = jnp.dot(cqn, wq_ref[...], preferred_element_type=F32)
    cq = cq_ref[...]
    sq = sq_ref[...]
    hw = MLA_HEADS * HEAD_PAD
    for h in range(MLA_HEADS):
        a = qq[:, h * HEAD_PAD:(h + 1) * HEAD_PAD]
        b = qq[:, hw + h * HEAD_PAD: hw + (h + 1) * HEAD_PAD]
        q_out[:, h * HEAD_PAD:(h + 1) * HEAD_PAD] = (a * cq + b * sq).astype(BF16)

    small = p[:, _C_SMALL0:D_IN_PAD]
    lane = lax.broadcasted_iota(jnp.int32, small.shape, 1)
    prod = small * tabk_ref[...]
    kr = prod + pltpu.roll(prod, LANES - MLA_ROPE_DIM, 1)
    kvs = jnp.where(lane < MLA_ROPE_DIM, kr, jnp.where(lane == _S_ONE, 1.0, 0.0))
    kvin = jnp.concatenate([ckvn, kvs.astype(BF16)], axis=1)
    kv = jnp.dot(kvin, wkv_ref[...], preferred_element_type=F32)
    k_out[...] = kv[:, :hw].astype(BF16)
    v_out[...] = kv[:, hw:].T.astype(BF16)

    g = small + gb_ref[...]
    lsig = jnp.minimum(g, 0.0) - jnp.log1p(jnp.exp(-jnp.abs(g)))
    is_f = (lane >= _S_F0) & (lane < _S_F0 + MLSTM_HEADS)
    g_out[...] = jnp.where(is_f, lsig, g)

    mqk = p[:, _C_MQK0:_C_MV0]
    ext = jnp.concatenate([carry_ref[...], mqk], axis=0)
    carry_ref[...] = mqk[tm - SUBLANES:tm, :]
    acc = cb_ref[...] + mqk * cw_ref[CONV_WIDTH - 1:CONV_WIDTH, :]
    for j in range(CONV_WIDTH - 1):
        off = SUBLANES - (CONV_WIDTH - 1) + j
        acc = acc + ext[off:off + tm, :] * cw_ref[j:j + 1, :]
    y = acc * (1.0 / (1.0 + jnp.exp(-acc)))
    mq_out[...] = (y[:, :D_MLSTM] * (MLSTM_HEAD_DIM ** -0.5)).astype(BF16)
    mk_out[...] = y[:, D_MLSTM:].astype(BF16)
    mv_out[...] = p[:, _C_MV0:_C_MO0].astype(BF16)
    mo = p[:, _C_MO0:_C_SMALL0]
    mo_out[...] = (1.0 / (1.0 + jnp.exp(-mo))).astype(BF16)


def _proj(x, win, wq, wkv, qg, kvg, cw, cb, gb, tabk, cqt, sqt, *, tm):
    S = x.shape[0]
    hw = MLA_HEADS * HEAD_PAD
    row = lambda w: pl.BlockSpec((tm, w), lambda i: (i, 0))
    full = lambda a: pl.BlockSpec(a.shape, lambda i: (0,) * a.ndim)
    out_shapes = (
        jax.ShapeDtypeStruct((S, hw), BF16), jax.ShapeDtypeStruct((S, hw), BF16),
        jax.ShapeDtypeStruct((hw, S), BF16),
        jax.ShapeDtypeStruct((S, D_MLSTM), BF16), jax.ShapeDtypeStruct((S, D_MLSTM), BF16),
        jax.ShapeDtypeStruct((S, D_MLSTM), BF16), jax.ShapeDtypeStruct((S, D_MLSTM), BF16),
        jax.ShapeDtypeStruct((S, LANES), F32))
    return pl.pallas_call(
        functools.partial(_proj_kernel, tm=tm),
        grid=(S // tm,),
        in_specs=[row(D_MODEL), full(win), full(wq), full(wkv), full(qg), full(kvg), full(cw),
                  full(cb), full(gb), row(LANES), row(LANES), row(LANES)],
        out_specs=(row(hw), row(hw), pl.BlockSpec((hw, tm), lambda i: (0, i)), row(D_MLSTM), row(D_MLSTM), row(D_MLSTM),
                   row(D_MLSTM), row(LANES)),
        out_shape=out_shapes,
        scratch_shapes=[pltpu.VMEM((SUBLANES, 2 * D_MLSTM), F32)],
        compiler_params=_cparams(("arbitrary",)),
        name="proj",
    )(x, win, wq, wkv, qg, kvg, cw, cb, gb, tabk, cqt, sqt)


VT_ROWS = 80


def _attn_kernel(q_ref, k_ref, vt_ref, og_ref, o_ref, s_ref, *, bq, heads_per_step):
    qi = pl.program_id(1)
    nh = heads_per_step
    cols = [(h * HEAD_PAD, (h + 1) * HEAD_PAD) for h in range(nh)]
    qs = [q_ref[:, c0:c1] for c0, c1 in cols]

    def scores(j, h):
        kt = k_ref[pl.ds(pl.multiple_of(j * bq, bq), bq), cols[h][0]:cols[h][1]]
        return lax.dot_general(kt, qs[h], (((1,), (1,)), ((), ())), preferred_element_type=F32)

    def accumulate(j, h, s, m, acc):
        vt = vt_ref[h * HEAD_PAD:h * HEAD_PAD + VT_ROWS, pl.ds(pl.multiple_of(j * bq, bq), bq)]
        m_new = jnp.maximum(m, jnp.max(s, axis=0, keepdims=True))
        alpha = jnp.exp2(m - m_new)
        p = jnp.exp2(s - m_new)
        return m_new, alpha * acc + jnp.dot(vt, p.astype(BF16), preferred_element_type=F32)

    def step(j, carry, cur, diagonal=False, prefetch=True):
        new = []
        for h in range(nh):
            m, acc = carry[2 * h:2 * h + 2]
            if prefetch:
                s_ref[(1 - cur) * nh + h] = scores(j + 1, h)
            s = s_ref[cur * nh + h]
            if diagonal:
                key = lax.broadcasted_iota(jnp.int32, (bq, bq), 0)
                qry = lax.broadcasted_iota(jnp.int32, (bq, bq), 1)
                s = jnp.where(key <= qry, s, -jnp.inf)
            new += list(accumulate(j, h, s, m, acc))
        return tuple(new)

    def pair(i, carry):
        return step(2 * i + 1, step(2 * i, carry, 0), 1)

    init = []
    for h in range(nh):
        s_ref[h] = scores(0, h)
        init += [jnp.full((1, bq), -jnp.inf, F32), jnp.zeros((VT_ROWS, bq), F32)]
    n_pairs = lax.shift_right_logical(qi, 1)
    n_quads = lax.shift_right_logical(qi, 2)
    carry = lax.fori_loop(0, n_quads, lambda i, cr: pair(2 * i + 1, pair(2 * i, cr)), tuple(init))
    carry = lax.fori_loop(2 * n_quads, n_pairs, pair, carry)
    carry = lax.cond(
        qi % 2 == 1,
        lambda cr: step(qi, step(qi - 1, cr, 0), 1, diagonal=True, prefetch=False),
        lambda cr: step(qi, cr, 0, diagonal=True, prefetch=False),
        carry)
    outs = []
    for h in range(nh):
        acc = carry[2 * h + 1]
        o = acc[:MLA_V_DIM, :] / acc[MLA_V_DIM:MLA_V_DIM + 1, :]
        o = o * lax.rsqrt(jnp.mean(o * o, axis=0, keepdims=True) + RMS_EPS)
        outs.append(o.T)
    o_ref[...] = (jnp.concatenate(outs, axis=1) * og_ref[...]).astype(BF16)


def _attention(q, k, vt, og, *, bq, heads_per_step=2):
    S = q.shape[0]
    n_hp = MLA_HEADS // heads_per_step
    wq = heads_per_step * HEAD_PAD
    wo = heads_per_step * MLA_V_DIM
    return pl.pallas_call(
        functools.partial(_attn_kernel, bq=bq, heads_per_step=heads_per_step),
        grid=(n_hp, S // bq),
        in_specs=[pl.BlockSpec((bq, wq), lambda h, i: (i, h)),
                  pl.BlockSpec((S, wq), lambda h, i: (0, h)),
                  pl.BlockSpec((wq, S), lambda h, i: (h, 0)),
                  pl.BlockSpec((1, wo), lambda h, i: (0, h))],
        out_specs=pl.BlockSpec((bq, wo), lambda h, i: (i, h)),
        out_shape=jax.ShapeDtypeStruct((S, MLA_HEADS * MLA_V_DIM), BF16),
        scratch_shapes=[pltpu.VMEM((2 * heads_per_step, bq, bq), F32)],
        compiler_params=_cparams(("arbitrary", "arbitrary")),
        name="attention",
    )(q, k, vt, og)


def _mlstm_kernel(q_ref, k_ref, v_ref, o_ref, gc_ref, gr_ref, og_ref, h_out, c_ref, m_ref, *, L):
    ci = pl.program_id(0)

    @pl.when(ci == 0)
    def _():
        c_ref[...] = jnp.zeros_like(c_ref)
        m_ref[...] = jnp.zeros_like(m_ref)

    r = lax.broadcasted_iota(jnp.int32, (L, L), 0)
    c = lax.broadcasted_iota(jnp.int32, (L, L), 1)
    tri = c <= r
    lane = lax.broadcasted_iota(jnp.int32, (L, LANES), 1)
    ones_blk = jnp.where(lane == 0, 1.0, 0.0).astype(BF16)
    gc = gc_ref[...]
    gr = gr_ref[...]
    d = MLSTM_HEAD_DIM
    for h in range(MLSTM_HEADS):
        q = q_ref[:, h * d:(h + 1) * d]
        k = k_ref[:, h * d:(h + 1) * d]
        v = v_ref[:, h * d:(h + 1) * d]
        li_col = gc[:, h:h + 1]
        lf_col = gc[:, MLSTM_HEADS + h:MLSTM_HEADS + h + 1]
        li_row = gr[h:h + 1, :]
        lf_row = gr[MLSTM_HEADS + h:MLSTM_HEADS + h + 1, :]
        b_col = jnp.sum(jnp.where(tri, lf_row, 0.0), axis=1, keepdims=True)
        b_row = jnp.sum(jnp.where(r <= c, lf_col, 0.0), axis=0, keepdims=True)
        m_prev = m_ref[h][0:1, 0:1]
        log_d = jnp.where(tri, b_col - b_row + li_row, -jnp.inf)
        log_inter = b_col + m_prev
        m_t = jnp.maximum(log_inter, jnp.max(log_d, axis=1, keepdims=True))
        w_intra = jnp.exp(log_d - m_t)
        w_inter = jnp.exp(log_inter - m_t)
        s = lax.dot_general(q, k, (((1,), (1,)), ((), ())), preferred_element_type=F32) * w_intra
        vaug = jnp.concatenate([v, ones_blk], axis=1)
        r1 = jnp.dot(s.astype(BF16), vaug, preferred_element_type=F32)
        cst = c_ref[h]
        r2 = jnp.dot(q, cst.astype(BF16), preferred_element_type=F32)
        num = r1[:, :d] + w_inter * r2[:, :d]
        den = r1[:, d:d + 1] + w_inter * r2[:, d:d + 1]
        hv = num / jnp.maximum(jnp.abs(den), jnp.exp(-m_t))
        b_end = b_col[L - 1:L, :]
        log_w = b_end - b_col + li_col
        m_new = jnp.maximum(b_end + m_prev, jnp.max(log_w, axis=0, keepdims=True))
        w_s = jnp.exp(log_w - m_new)
        decay = jnp.exp(b_end + m_prev - m_new)
        kw = (k.astype(F32) * w_s).astype(BF16)
        upd = lax.dot_general(kw, vaug, (((0,), (0,)), ((), ())), preferred_element_type=F32)
        c_ref[h] = decay * cst + upd
        m_ref[h] = jnp.broadcast_to(m_new, (SUBLANES, LANES))
        hn = hv * lax.rsqrt(jnp.mean(hv * hv, axis=-1, keepdims=True) + RMS_EPS)
        gate = o_ref[:, h * d:(h + 1) * d].astype(F32)
        h_out[:, h * d:(h + 1) * d] = (hn * gate * og_ref[:, h * d:(h + 1) * d]).astype(BF16)


def _mlstm(mq, mk, mv, mo, gcol, grow, og, *, L):
    S = mq.shape[0]
    d = MLSTM_HEAD_DIM
    row = pl.BlockSpec((L, D_MLSTM), lambda i: (i, 0))
    return pl.pallas_call(
        functools.partial(_mlstm_kernel, L=L),
        grid=(S // L,),
        in_specs=[row, row, row, row,
                  pl.BlockSpec((L, 2 * MLSTM_HEADS), lambda i: (i, 0)),
                  pl.BlockSpec((2 * MLSTM_HEADS, L), lambda i: (0, i)),
                  pl.BlockSpec((1, D_MLSTM), lambda i: (0, 0))],
        out_specs=row,
        out_shape=jax.ShapeDtypeStruct((S, D_MLSTM), BF16),
        scratch_shapes=[pltpu.VMEM((MLSTM_HEADS, d, 2 * d), F32),
                        pltpu.VMEM((MLSTM_HEADS, SUBLANES, LANES), F32)],
        compiler_params=_cparams(("arbitrary",)),
        name="mlstm",
    )(mq, mk, mv, mo, gcol, grow, og)


def _layer_norm(z, g, b):
    mu = jnp.mean(z, axis=-1, keepdims=True)
    zc = z - mu
    var = jnp.mean(zc * zc, axis=-1, keepdims=True)
    return zc * lax.rsqrt(var + LN_EPS) * g + b


def _outproj_kernel(x_ref, a_ref, h_ref, wo_ref, g_ref, b_ref, wr_ref, x1_out, x1t_out, lg_out, *, alpha, tm):
    mix = jnp.concatenate([a_ref[...], h_ref[...]], axis=1)
    y = jnp.dot(mix, wo_ref[...], preferred_element_type=F32)
    x1 = _layer_norm(alpha * x_ref[...] + y, g_ref[...], b_ref[...])
    x1_out[...] = x1
    for c in range(SUBLANES):
        x1t_out[pl.ds(c, tm, stride=SUBLANES), :] = x1[:, c * LANES:(c + 1) * LANES]
    lg_out[...] = jnp.dot(x1, wr_ref[...], preferred_element_type=F32,
                          precision=lax.Precision.HIGHEST)


def _outproj(x, a, hm, wo, g, b, wr, *, tm, alpha):
    S = x.shape[0]
    row = lambda w: pl.BlockSpec((tm, w), lambda i: (i, 0))
    full = lambda arr: pl.BlockSpec(arr.shape, lambda i: (0,) * arr.ndim)
    return pl.pallas_call(
        functools.partial(_outproj_kernel, alpha=alpha, tm=tm),
        grid=(S // tm,),
        in_specs=[row(D_MODEL), row(MLA_HEADS * MLA_V_DIM), row(D_MLSTM), full(wo), full(g), full(b),
                  full(wr)],
        out_specs=(row(D_MODEL), pl.BlockSpec((tm * SUBLANES, LANES), lambda i: (i, 0)), row(LANES)),
        out_shape=(jax.ShapeDtypeStruct((S, D_MODEL), F32),
                   jax.ShapeDtypeStruct((S * SUBLANES, LANES), F32),
                   jax.ShapeDtypeStruct((S, LANES), F32)),
        compiler_params=_cparams(("arbitrary",)),
        name="outproj",
    )(x, a, hm, wo, g, b, wr)


TOKEN_TILE = SUBLANES
DMA_GROUP = 8


def _rows_from_tiles(ref, n):
    return jnp.concatenate([ref[pl.ds(c, n, stride=TOKEN_TILE), :] for c in range(TOKEN_TILE)], axis=1)


def _rows_to_tiles(ref, v):
    for c in range(TOKEN_TILE):
        ref[pl.ds(c, v.shape[0], stride=TOKEN_TILE), :] = v[:, c * LANES:(c + 1) * LANES]


def _pow2_chunks(n, top, fn):
    off = 0
    bit = top
    while bit:
        @pl.when((n & bit) != 0)
        def _(bit=bit, off=off):
            fn(off, bit)
        off = off + (n & bit)
        bit //= 2


def _dispatch_kernel(cnt_ref, pst_ref, d_ref, x_ref, xs_hbm, xbuf, zbuf, sem, zsem, *, tt):
    i = pl.program_id(0)
    n = pl.num_programs(0)
    G = TOKEN_TILE
    slot = lax.rem(i, 2)

    def row_copy(src_row, dst_row, rows, sl):
        return pltpu.make_async_copy(xbuf.at[sl, pl.ds(src_row * G, rows * G)],
                                     xs_hbm.at[pl.ds(dst_row * G, rows * G)], sem.at[sl])

    def zero_copy(dst_row, rows):
        return pltpu.make_async_copy(zbuf.at[pl.ds(0, rows * G)], xs_hbm.at[pl.ds(dst_row * G, rows * G)], zsem)

    @pl.when(i == 0)
    def _():
        zbuf[...] = jnp.zeros_like(zbuf)

        def fill(e, wait):
            pad = (-cnt_ref[e]) & (EXPERT_BLOCK - 1)
            first = pst_ref[e] + cnt_ref[e]
            if wait:
                _pow2_chunks(pad, EXPERT_BLOCK // 2, lambda off, size: zero_copy(0, size).wait())
            else:
                _pow2_chunks(pad, EXPERT_BLOCK // 2, lambda off, size: zero_copy(first + off, size).start())
            return 0
        lax.fori_loop(0, N_EXPERTS, lambda e, c: fill(e, False), 0)
        lax.fori_loop(0, N_EXPERTS, lambda e, c: fill(e, True), 0)
        half = EXPERT_BLOCK // 2
        used = (pst_ref[N_EXPERTS - 1] + cnt_ref[N_EXPERTS - 1] + EXPERT_BLOCK - 1) // EXPERT_BLOCK * 2
        n_half = xs_hbm.shape[0] // (half * G)

        def tail(c, wait):
            if wait:
                zero_copy(0, half).wait()
            else:
                zero_copy(c * half, half).start()
            return 0
        lax.fori_loop(used, n_half, lambda c, _: tail(c, False), 0)
        lax.fori_loop(used, n_half, lambda c, _: tail(c, True), 0)

    def wait_step(sl):
        for _ in range(TOP_K):
            row_copy(0, 0, tt, sl).wait()

    for sl in range(2):
        @pl.when(slot == sl)
        def _(sl=sl):
            @pl.when(i >= 2)
            def _():
                wait_step(sl)
            xbuf[sl] = x_ref[...]
            for t0 in range(0, tt, DMA_GROUP):
                idx = [d_ref[0, 0, j] for j in range(TOP_K * t0, TOP_K * (t0 + DMA_GROUP))]
                for j, dst in enumerate(idx):
                    row_copy(t0 + j // TOP_K, dst, 1, sl).start()

    @pl.when(i == n - 1)
    def _():
        @pl.when(n >= 2)
        def _():
            wait_step(1 - slot)
        wait_step(slot)


def _dispatch(x1t, dest, counts, pad_starts, n_slots, *, tt):
    G = TOKEN_TILE
    T = x1t.shape[0] // G
    nt = T // tt
    grid_spec = pltpu.PrefetchScalarGridSpec(
        num_scalar_prefetch=2,
        grid=(nt,),
        in_specs=[pl.BlockSpec((1, 1, TOP_K * tt), lambda i, c, p: (i, 0, 0), memory_space=pltpu.SMEM),
                  pl.BlockSpec((tt * G, LANES), lambda i, c, p: (i, 0))],
        out_specs=pl.BlockSpec(memory_space=pl.ANY),
        scratch_shapes=[pltpu.VMEM((2, tt * G, LANES), F32),
                        pltpu.VMEM((EXPERT_BLOCK // 2 * G, LANES), F32),
                        pltpu.SemaphoreType.DMA((2,)), pltpu.SemaphoreType.DMA])
    return pl.pallas_call(
        functools.partial(_dispatch_kernel, tt=tt),
        grid_spec=grid_spec,
        out_shape=jax.ShapeDtypeStruct((n_slots * G, LANES), F32),
        compiler_params=_cparams(("arbitrary",)),
        name="dispatch",
    )(counts, pad_starts, dest.reshape(nt, 1, TOP_K * tt), x1t)


def _moe_kernel(bexp_ref, nvb_ref, xs_ref, wg_ref, wu_ref, wd_ref, ys_ref, wgb, wub, wdb):
    b = pl.program_id(0)

    @pl.when((b == 0) | (bexp_ref[b] != bexp_ref[jnp.maximum(b - 1, 0)]))
    def _():
        wgb[...] = wg_ref[0, 0].astype(BF16)
        wub[...] = wu_ref[0, 0].astype(BF16)
        wdb[...] = wd_ref[0, 0].astype(BF16)

    @pl.when(b < nvb_ref[0])
    def _():
        xs = _rows_from_tiles(xs_ref, EXPERT_BLOCK).astype(BF16)
        g = jnp.dot(xs, wgb[...], preferred_element_type=F32)
        u = jnp.dot(xs, wub[...], preferred_element_type=F32)
        hid = (g * (1.0 / (1.0 + jnp.exp(-g)))) * u
        _rows_to_tiles(ys_ref, jnp.dot(hid.astype(BF16), wdb[...], preferred_element_type=F32))

    @pl.when(b >= nvb_ref[0])
    def _():
        ys_ref[...] = jnp.zeros_like(ys_ref)


def _moe(xs, block_expert, n_valid_blocks, wg, wu, wd, layer):
    G = TOKEN_TILE
    assert G * LANES == D_MODEL
    R = EXPERT_BLOCK
    nb = xs.shape[0] // (R * G)
    blk = lambda b, be, nvb: (b, 0)
    wspec = lambda shp: pl.BlockSpec((1, 1) + shp, lambda b, be, nvb: (layer, be[b], 0, 0))
    grid_spec = pltpu.PrefetchScalarGridSpec(
        num_scalar_prefetch=2,
        grid=(nb,),
        in_specs=[pl.BlockSpec((R * G, LANES), blk),
                  wspec((D_MODEL, D_EXPERT)), wspec((D_MODEL, D_EXPERT)), wspec((D_EXPERT, D_MODEL))],
        out_specs=pl.BlockSpec((R * G, LANES), blk),
        scratch_shapes=[pltpu.VMEM((D_MODEL, D_EXPERT), BF16), pltpu.VMEM((D_MODEL, D_EXPERT), BF16),
                        pltpu.VMEM((D_EXPERT, D_MODEL), BF16)])
    return pl.pallas_call(
        _moe_kernel,
        grid_spec=grid_spec,
        out_shape=jax.ShapeDtypeStruct(xs.shape, F32),
        compiler_params=_cparams(("arbitrary",)),
        name="experts",
    )(block_expert, n_valid_blocks, xs, wg, wu, wd)


def _combine_kernel(d_ref, dn_ref, x_ref, w_ref, g_ref, b_ref, ys_hbm, o_ref, ybuf, sem, *, alpha, tt):
    i = pl.program_id(0)
    n = pl.num_programs(0)
    G = TOKEN_TILE
    slot = lax.rem(i, 2)

    def row_copy(src_row, r, rows, sl):
        return pltpu.make_async_copy(ys_hbm.at[pl.ds(src_row * G, rows * G)],
                                     ybuf.at[sl, pl.ds(r * G, rows * G)], sem.at[sl])

    def fetch(idx_ref, sl):
        for t0 in range(0, tt, DMA_GROUP):
            idx = [idx_ref[0, 0, j] for j in range(TOP_K * t0, TOP_K * (t0 + DMA_GROUP))]
            for j, src in enumerate(idx):
                row_copy(src, (j % TOP_K) * tt + t0 + j // TOP_K, 1, sl).start()

    @pl.when(i == 0)
    def _():
        fetch(d_ref, 0)

    for sl in range(2):
        @pl.when(slot == sl)
        def _(sl=sl):
            row_copy(0, 0, TOP_K * tt, sl).wait()
            fetch(dn_ref, 1 - sl)
            w = w_ref[...]
            y = (_rows_from_tiles(ybuf.at[sl, pl.ds(0, tt * G)], tt) * w[:, 0:1]
                 + _rows_from_tiles(ybuf.at[sl, pl.ds(tt * G, tt * G)], tt) * w[:, 1:2])
            o_ref[...] = _layer_norm(alpha * x_ref[...] + y, g_ref[...], b_ref[...])

    @pl.when(i == n - 1)
    def _():
        row_copy(0, 0, TOP_K * tt, 1 - slot).wait()


def _combine(x1, ys, dest, w, g, b, *, tt, alpha):
    S = x1.shape[0]
    nt = S // tt
    G = TOKEN_TILE
    d3 = dest.reshape(nt, 1, TOP_K * tt)
    smem_blk = lambda f: pl.BlockSpec((1, 1, TOP_K * tt), f, memory_space=pltpu.SMEM)
    return pl.pallas_call(
        functools.partial(_combine_kernel, alpha=alpha, tt=tt),
        grid=(nt,),
        in_specs=[smem_blk(lambda i: (i, 0, 0)),
                  smem_blk(lambda i: (jnp.minimum(i + 1, nt - 1), 0, 0)),
                  pl.BlockSpec((tt, D_MODEL), lambda i: (i, 0)),
                  pl.BlockSpec((tt, TOP_K), lambda i: (i, 0)),
                  pl.BlockSpec((1, D_MODEL), lambda i: (0, 0)),
                  pl.BlockSpec((1, D_MODEL), lambda i: (0, 0)),
                  pl.BlockSpec(memory_space=pl.ANY)],
        out_specs=pl.BlockSpec((tt, D_MODEL), lambda i: (i, 0)),
        out_shape=jax.ShapeDtypeStruct((S, D_MODEL), F32),
        scratch_shapes=[pltpu.VMEM((2, TOP_K * tt * G, LANES), F32), pltpu.SemaphoreType.DMA((2,))],
        compiler_params=_cparams(("arbitrary",)),
        name="combine",
    )(d3, d3, x1, w, g, b, ys)


def _route(logits, T):
    g_logits = logits[:, :N_GROUPS]
    g_prob = jax.nn.softmax(g_logits, axis=-1)
    g_top = jnp.argmax(g_prob, axis=-1)
    g_p = jnp.max(g_prob, axis=-1)
    e_logits = logits[:, N_GROUPS:N_GROUPS + N_EXPERTS].reshape(T, N_GROUPS, EXPERTS_PER_GROUP)
    e_in = jnp.take_along_axis(e_logits, g_top[:, None, None], axis=1)[:, 0, :]
    p = jax.nn.softmax(e_in, axis=-1)
    i1 = jnp.argmax(p, axis=-1)
    pm = jnp.where(jnp.arange(EXPERTS_PER_GROUP)[None, :] == i1[:, None], -jnp.inf, p)
    i2 = jnp.argmax(pm, axis=-1)
    e_top = jnp.stack([i1, i2], axis=-1)
    e_p = jnp.stack([jnp.max(p, axis=-1), jnp.max(pm, axis=-1)], axis=-1)
    e_p = e_p / jnp.sum(e_p, -1, keepdims=True)
    weights = g_p[:, None] * e_p
    flat_e = (g_top[:, None] * EXPERTS_PER_GROUP + e_top).astype(jnp.int32).reshape(-1)
    n_assign = T * TOP_K
    onehot = (flat_e[:, None] == jnp.arange(N_EXPERTS, dtype=jnp.int32)[None, :]).astype(jnp.int32)
    csum = jnp.cumsum(onehot, axis=0)
    rank = jnp.take_along_axis(csum, flat_e[:, None], axis=1)[:, 0] - 1
    counts = csum[-1]
    padded = (counts + EXPERT_BLOCK - 1) // EXPERT_BLOCK * EXPERT_BLOCK
    pad_ends = jnp.cumsum(padded)
    pad_starts = pad_ends - padded
    dest = (pad_starts[flat_e] + rank).astype(jnp.int32)
    n_blocks = n_assign // EXPERT_BLOCK + N_EXPERTS
    block_start = jnp.arange(n_blocks, dtype=jnp.int32) * EXPERT_BLOCK
    block_expert = jnp.minimum(jnp.sum((pad_ends[None, :] <= block_start[:, None]).astype(jnp.int32), axis=1),
                               N_EXPERTS - 1).astype(jnp.int32)
    n_valid_blocks = (pad_ends[-1:] // EXPERT_BLOCK).astype(jnp.int32)
    return (dest, counts.astype(jnp.int32), pad_starts.astype(jnp.int32), block_expert, n_valid_blocks,
            n_blocks * EXPERT_BLOCK, weights)


def _in_weight(w_in):
    kr0 = MLA_Q_RANK + MLA_KV_RANK
    mq0 = kr0 + MLA_ROPE_DIM
    i0 = mq0 + 4 * D_MLSTM
    half = MLA_ROPE_DIM // 2
    n_small = 2 * MLA_ROPE_DIM + 2 * MLSTM_HEADS
    pad = jnp.zeros(w_in.shape[:-1] + (LANES - n_small,), w_in.dtype)
    return jnp.concatenate(
        [w_in[..., :kr0], w_in[..., mq0:i0], w_in[..., kr0:mq0], w_in[..., kr0 + half:mq0],
         w_in[..., kr0:kr0 + half], w_in[..., i0:i0 + 2 * MLSTM_HEADS], pad], axis=-1).astype(BF16)


def _q_weight(w_qb):
    lead = w_qb.shape[:-1]
    w = w_qb.reshape(lead + (MLA_HEADS, MLA_QK_DIM))
    half = MLA_ROPE_DIM // 2
    z = lambda n: jnp.zeros(lead + (MLA_HEADS, n), w_qb.dtype)
    main = jnp.concatenate([w, z(HEAD_PAD - MLA_QK_DIM)], axis=-1)
    swp = jnp.concatenate([z(MLA_NOPE_DIM), w[..., MLA_NOPE_DIM + half:], w[..., MLA_NOPE_DIM:MLA_NOPE_DIM + half],
                           z(HEAD_PAD - MLA_QK_DIM)], axis=-1)
    hw = MLA_HEADS * HEAD_PAD
    return jnp.concatenate([main.reshape(lead + (hw,)), swp.reshape(lead + (hw,))], axis=-1).astype(BF16)


def _kv_weight(w_kvb_l):
    hw = MLA_HEADS * HEAD_PAD
    w = w_kvb_l.reshape(MLA_KV_RANK, MLA_HEADS, MLA_NOPE_DIM + MLA_V_DIM)
    top_k = jnp.pad(w[:, :, :MLA_NOPE_DIM], ((0, 0), (0, 0), (0, HEAD_PAD - MLA_NOPE_DIM)))
    top_v = jnp.pad(w[:, :, MLA_NOPE_DIM:], ((0, 0), (0, 0), (0, HEAD_PAD - MLA_V_DIM)))
    bot_k = np.zeros((LANES, MLA_HEADS, HEAD_PAD), np.float32)
    bot_v = np.zeros((LANES, MLA_HEADS, HEAD_PAD), np.float32)
    for r in range(MLA_ROPE_DIM):
        bot_k[r, :, MLA_NOPE_DIM + r] = 1.0
    bot_v[_S_ONE, :, MLA_V_DIM] = 1.0
    wk = jnp.concatenate([top_k.reshape(MLA_KV_RANK, hw), jnp.asarray(bot_k).reshape(LANES, hw)], 0)
    wv = jnp.concatenate([top_v.reshape(MLA_KV_RANK, hw), jnp.asarray(bot_v).reshape(LANES, hw)], 0)
    return jnp.concatenate([wk, wv], axis=1).astype(BF16)


def _rope_tables(positions):
    half = MLA_ROPE_DIM // 2
    inv_freq = ROPE_THETA ** (-jnp.arange(half, dtype=F32) / half)
    ang = positions.astype(F32)[:, None] * inv_freq
    cos, sin = jnp.cos(ang), jnp.sin(ang)
    ct = jnp.concatenate([cos, cos], -1)
    st = jnp.concatenate([-sin, sin], -1)
    S = positions.shape[0]
    z = lambda w: jnp.zeros((S, w), F32)
    scale = MLA_QK_DIM ** -0.5 * float(np.log2(np.e))
    tabk = jnp.concatenate([ct, st, z(LANES - 2 * MLA_ROPE_DIM)], -1)
    cqt = jnp.concatenate([jnp.full((S, MLA_NOPE_DIM), scale, F32), ct * scale,
                           z(HEAD_PAD - MLA_QK_DIM)], -1)
    sqt = jnp.concatenate([z(MLA_NOPE_DIM), st * scale, z(HEAD_PAD - MLA_QK_DIM)], -1)
    return tabk, cqt, sqt


def _pick(S, pref):
    t = pref
    while S % t:
        t //= 2
    return t


def kernel(x, positions, w_in, conv_w, conv_b, gate_b, q_a_g, kv_a_g, w_qb, w_kvb, out_g, w_out, ln1_g, ln1_b, w_rg, w_re, w_gate, w_up, w_down, ln2_g, ln2_b):
    B, S, D = x.shape
    assert B == 1 and D == D_MODEL
    depth = w_in.shape[0]
    alpha = float((2 * depth) ** 0.25)
    tm = _pick(S, 512)
    bq = _pick(S, 512)
    L = _pick(S, 256)
    tt = _pick(S, 256)
    assert tm % SUBLANES == 0 and bq % LANES == 0 and L % LANES == 0

    tabk, cqt, sqt = _rope_tables(positions[0])
    win_all = _in_weight(w_in)
    wq_all = _q_weight(w_qb)
    gb_tab = jnp.zeros((depth, 1, LANES), F32).at[:, 0, _S_I0:_S_I0 + 2 * MLSTM_HEADS].set(gate_b)
    wr_all = jnp.concatenate([w_rg, w_re, jnp.zeros((depth, D_MODEL, LANES - N_GROUPS - N_EXPERTS), F32)], -1)
    wo_all = w_out.astype(BF16)
    n_att = MLA_HEADS * MLA_V_DIM

    xs = x[0]
    for l in range(depth):
        q, k, v, mq, mk, mv, mo, gates = _proj(
            xs, win_all[l], wq_all[l], _kv_weight(w_kvb[l]), q_a_g[l][None], kv_a_g[l][None],
            conv_w[l], conv_b[l][None], gb_tab[l], tabk, cqt, sqt, tm=tm)
        a = _attention(q, k, v, out_g[l][None, :n_att], bq=bq)
        gcol = gates[:, _S_I0:_S_I0 + 2 * MLSTM_HEADS]
        hm = _mlstm(mq, mk, mv, mo, gcol, gcol.T, out_g[l][None, n_att:], L=L)
        x1, x1t, logits = _outproj(xs, a, hm, wo_all[l], ln1_g[l][None], ln1_b[l][None], wr_all[l],
                                   tm=tm, alpha=alpha)
        dest, counts, pad_starts, bexp, nvb, n_slots, rw = _route(logits, S)
        xsort = _dispatch(x1t, dest, counts, pad_starts, n_slots, tt=tt)
        ysort = _moe(xsort, bexp, nvb, w_gate, w_up, w_down, l)
        xs = _combine(x1, ysort, dest, rw, ln2_g[l][None], ln2_b[l][None], tt=tt, alpha=alpha)
    return xs[None]
```

```python
import functools

import numpy as np
import jax
import jax.numpy as jnp
from jax import lax
from jax.experimental import pallas as pl
from jax.experimental.pallas import tpu as pltpu

F32 = jnp.float32
BF16 = jnp.bfloat16

D_MODEL = 1024
MLA_HEADS = 8
MLA_V_DIM = 64
MLA_NOPE_DIM = 64
MLA_ROPE_DIM = 32
MLA_QK_DIM = MLA_NOPE_DIM + MLA_ROPE_DIM
MLA_Q_RANK = 256
MLA_KV_RANK = 128
ROPE_THETA = 10000.0
MLSTM_HEADS = 4
MLSTM_HEAD_DIM = 128
D_MLSTM = MLSTM_HEADS * MLSTM_HEAD_DIM
CONV_WIDTH = 4
N_GROUPS = 8
EXPERTS_PER_GROUP = 8
N_EXPERTS = N_GROUPS * EXPERTS_PER_GROUP
D_EXPERT = 256
TOP_K = 2
EXPERT_BLOCK = 128
LN_EPS = 1e-5
RMS_EPS = 1e-6

LANES = 128
SUBLANES = 8
HEAD_PAD = 128
VMEM_LIMIT = 56 * 1024 * 1024

_C_Q0 = 0
_C_KV0 = MLA_Q_RANK
_C_MQK0 = _C_KV0 + MLA_KV_RANK
_C_MV0 = _C_MQK0 + 2 * D_MLSTM
_C_MO0 = _C_MV0 + D_MLSTM
_C_SMALL0 = _C_MO0 + D_MLSTM
D_IN_PAD = _C_SMALL0 + LANES
_S_I0 = 2 * MLA_ROPE_DIM
_S_F0 = _S_I0 + MLSTM_HEADS
_S_ONE = MLA_ROPE_DIM


def _cparams(sem, vmem=VMEM_LIMIT):
    return pltpu.CompilerParams(dimension_semantics=sem, vmem_limit_bytes=vmem)


def _proj_kernel(x_ref, win_ref, wq_ref, wkv_ref, qg_ref, kvg_ref, cw_ref, cb_ref, gb_ref,
                 tabk_ref, cq_ref, sq_ref,
                 q_out, k_out, v_out, mq_out, mk_out, mv_out, mo_out, g_out, carry_ref, *, tm):
    i = pl.program_id(0)

    @pl.when(i == 0)
    def _():
        carry_ref[...] = jnp.zeros_like(carry_ref)

    xb = x_ref[...].astype(BF16)
    p = jnp.dot(xb, win_ref[...], preferred_element_type=F32)

    def rms(v, g):
        return v * lax.rsqrt(jnp.mean(v * v, axis=-1, keepdims=True) + RMS_EPS) * g

    cqn = rms(p[:, _C_Q0:_C_KV0], qg_ref[...]).astype(BF16)
    ckvn = rms(p[:, _C_KV0:_C_MQK0], kvg_ref[...]).astype(BF16)

    qq = jnp.dot(cqn, wq_ref[...], preferred_element_type=F32)
    cq = cq_ref[...]
    sq = sq_ref[...]
    hw = MLA_HEADS * HEAD_PAD
    for h in range(MLA_HEADS):
        a = qq[:, h * HEAD_PAD:(h + 1) * HEAD_PAD]
        b = qq[:, hw + h * HEAD_PAD: hw + (h + 1) * HEAD_PAD]
        q_out[:, h * HEAD_PAD:(h + 1) * HEAD_PAD] = (a * cq + b * sq).astype(BF16)

    small = p[:, _C_SMALL0:D_IN_PAD]
    lane = lax.broadcasted_iota(jnp.int32, small.shape, 1)
    prod = small * tabk_ref[...]
    kr = prod + pltpu.roll(prod, LANES - MLA_ROPE_DIM, 1)
    kvs = jnp.where(lane < MLA_ROPE_DIM, kr, jnp.where(lane == _S_ONE, 1.0, 0.0))
    kvin = jnp.concatenate([ckvn, kvs.astype(BF16)], axis=1)
    kv = jnp.dot(kvin, wkv_ref[...], preferred_element_type=F32)
    k_out[...] = kv[:, :hw].astype(BF16)
    v_out[...] = kv[:, hw:].T.astype(BF16)

    g = small + gb_ref[...]
    lsig = jnp.minimum(g, 0.0) - jnp.log1p(jnp.exp(-jnp.abs(g)))
    is_f = (lane >= _S_F0) & (lane < _S_F0 + MLSTM_HEADS)
    g_out[...] = jnp.where(is_f, lsig, g)

    mqk = p[:, _C_MQK0:_C_MV0]
    ext = jnp.concatenate([carry_ref[...], mqk], axis=0)
    carry_ref[...] = mqk[tm - SUBLANES:tm, :]
    acc = cb_ref[...] + mqk * cw_ref[CONV_WIDTH - 1:CONV_WIDTH, :]
    for j in range(CONV_WIDTH - 1):
        off = SUBLANES - (CONV_WIDTH - 1) + j
        acc = acc + ext[off:off + tm, :] * cw_ref[j:j + 1, :]
    y = acc * (1.0 / (1.0 + jnp.exp(-acc)))
    mq_out[...] = (y[:, :D_MLSTM] * (MLSTM_HEAD_DIM ** -0.5)).astype(BF16)
    mk_out[...] = y[:, D_MLSTM:].astype(BF16)
    mv_out[...] = p[:, _C_MV0:_C_MO0].astype(BF16)
    mo = p[:, _C_MO0:_C_SMALL0]
    mo_out[...] = (1.0 / (1.0 + jnp.exp(-mo))).astype(BF16)


def _proj(x, win, wq, wkv, qg, kvg, cw, cb, gb, tabk, cqt, sqt, *, tm):
    S = x.shape[0]
    hw = MLA_HEADS * HEAD_PAD
    row = lambda w: pl.BlockSpec((tm, w), lambda i: (i, 0))
    full = lambda a: pl.BlockSpec(a.shape, lambda i: (0,) * a.ndim)
    out_shapes = (
        jax.ShapeDtypeStruct((S, hw), BF16), jax.ShapeDtypeStruct((S, hw), BF16),
        jax.ShapeDtypeStruct((hw, S), BF16),
        jax.ShapeDtypeStruct((S, D_MLSTM), BF16), jax.ShapeDtypeStruct((S, D_MLSTM), BF16),
        jax.ShapeDtypeStruct((S, D_MLSTM), BF16), jax.ShapeDtypeStruct((S, D_MLSTM), BF16),
        jax.ShapeDtypeStruct((S, LANES), F32))
    return pl.pallas_call(
        functools.partial(_proj_kernel, tm=tm),
        grid=(S // tm,),
        in_specs=[row(D_MODEL), full(win), full(wq), full(wkv), full(qg), full(kvg), full(cw),
                  full(cb), full(gb), row(LANES), row(LANES), row(LANES)],
        out_specs=(row(hw), row(hw), pl.BlockSpec((hw, tm), lambda i: (0, i)), row(D_MLSTM), row(D_MLSTM), row(D_MLSTM),
                   row(D_MLSTM), row(LANES)),
        out_shape=out_shapes,
        scratch_shapes=[pltpu.VMEM((SUBLANES, 2 * D_MLSTM), F32)],
        compiler_params=_cparams(("arbitrary",)),
        name="proj",
    )(x, win, wq, wkv, qg, kvg, cw, cb, gb, tabk, cqt, sqt)


VT_ROWS = 80


def _attn_kernel(q_ref, k_ref, vt_ref, og_ref, o_ref, s_ref, *, bq, heads_per_step):
    qi = pl.program_id(1)
    nh = heads_per_step
    cols = [(h * HEAD_PAD, (h + 1) * HEAD_PAD) for h in range(nh)]
    qs = [q_ref[:, c0:c1] for c0, c1 in cols]

    def scores(j, h):
        kt = k_ref[pl.ds(pl.multiple_of(j * bq, bq), bq), cols[h][0]:cols[h][1]]
        return lax.dot_general(kt, qs[h], (((1,), (1,)), ((), ())), preferred_element_type=F32)

    def accumulate(j, h, s, m, acc):
        vt = vt_ref[h * HEAD_PAD:h * HEAD_PAD + VT_ROWS, pl.ds(pl.multiple_of(j * bq, bq), bq)]
        m_new = jnp.maximum(m, jnp.max(s, axis=0, keepdims=True))
        alpha = jnp.exp2(m - m_new)
        p = jnp.exp2(s - m_new)
        return m_new, alpha * acc + jnp.dot(vt, p.astype(BF16), preferred_element_type=F32)

    def step(j, carry, cur, diagonal=False, prefetch=True):
        new = []
        for h in range(nh):
            m, acc = carry[2 * h:2 * h + 2]
            if prefetch:
                s_ref[(1 - cur) * nh + h] = scores(j + 1, h)
            s = s_ref[cur * nh + h]
            if diagonal:
                key = lax.broadcasted_iota(jnp.int32, (bq, bq), 0)
                qry = lax.broadcasted_iota(jnp.int32, (bq, bq), 1)
                s = jnp.where(key <= qry, s, -jnp.inf)
            new += list(accumulate(j, h, s, m, acc))
        return tuple(new)

    def pair(i, carry):
        return step(2 * i + 1, step(2 * i, carry, 0), 1)

    init = []
    for h in range(nh):
        s_ref[h] = scores(0, h)
        init += [jnp.full((1, bq), -jnp.inf, F32), jnp.zeros((VT_ROWS, bq), F32)]
    n_pairs = lax.shift_right_logical(qi, 1)
    n_quads = lax.shift_right_logical(qi, 2)
    carry = lax.fori_loop(0, n_quads, lambda i, cr: pair(2 * i + 1, pair(2 * i, cr)), tuple(init))
    carry = lax.fori_loop(2 * n_quads, n_pairs, pair, carry)
    carry = lax.cond(
        qi % 2 == 1,
        lambda cr: step(qi, step(qi - 1, cr, 0), 1, diagonal=True, prefetch=False),
        lambda cr: step(qi, cr, 0, diagonal=True, prefetch=False),
        carry)
    outs = []
    for h in range(nh):
        acc = carry[2 * h + 1]
        o = acc[:MLA_V_DIM, :] / acc[MLA_V_DIM:MLA_V_DIM + 1, :]
        o = o * lax.rsqrt(jnp.mean(o * o, axis=0, keepdims=True) + RMS_EPS)
        outs.append(o.T)
    o_ref[...] = (jnp.concatenate(outs, axis=1) * og_ref[...]).astype(BF16)


def _attention(q, k, vt, og, *, bq, heads_per_step=2):
    S = q.shape[0]
    n_hp = MLA_HEADS // heads_per_step
    wq = heads_per_step * HEAD_PAD
    wo = heads_per_step * MLA_V_DIM
    return pl.pallas_call(
        functools.partial(_attn_kernel, bq=bq, heads_per_step=heads_per_step),
        grid=(n_hp, S // bq),
        in_specs=[pl.BlockSpec((bq, wq), lambda h, i: (i, h)),
                  pl.BlockSpec((S, wq), lambda h, i: (0, h)),
                  pl.BlockSpec((wq, S), lambda h, i: (h, 0)),
                  pl.BlockSpec((1, wo), lambda h, i: (0, h))],
        out_specs=pl.BlockSpec((bq, wo), lambda h, i: (i, h)),
        out_shape=jax.ShapeDtypeStruct((S, MLA_HEADS * MLA_V_DIM), BF16),
        scratch_shapes=[pltpu.VMEM((2 * heads_per_step, bq, bq), F32)],
        compiler_params=_cparams(("arbitrary", "arbitrary")),
        name="attention",
    )(q, k, vt, og)


def _mlstm_kernel(q_ref, k_ref, v_ref, o_ref, gc_ref, gr_ref, og_ref, h_out, c_ref, m_ref, *, L):
    ci = pl.program_id(0)

    @pl.when(ci == 0)
    def _():
        c_ref[...] = jnp.zeros_like(c_ref)
        m_ref[...] = jnp.zeros_like(m_ref)

    r = lax.broadcasted_iota(jnp.int32, (L, L), 0)
    c = lax.broadcasted_iota(jnp.int32, (L, L), 1)
    tri = c <= r
    lane = lax.broadcasted_iota(jnp.int32, (L, LANES), 1)
    ones_blk = jnp.where(lane == 0, 1.0, 0.0).astype(BF16)
    gc = gc_ref[...]
    gr = gr_ref[...]
    d = MLSTM_HEAD_DIM
    for h in range(MLSTM_HEADS):
        q = q_ref[:, h * d:(h + 1) * d]
        k = k_ref[:, h * d:(h + 1) * d]
        v = v_ref[:, h * d:(h + 1) * d]
        li_col = gc[:, h:h + 1]
        lf_col = gc[:, MLSTM_HEADS + h:MLSTM_HEADS + h + 1]
        li_row = gr[h:h + 1, :]
        lf_row = gr[MLSTM_HEADS + h:MLSTM_HEADS + h + 1, :]
        b_col = jnp.sum(jnp.where(tri, lf_row, 0.0), axis=1, keepdims=True)
        b_row = jnp.sum(jnp.where(r <= c, lf_col, 0.0), axis=0, keepdims=True)
        m_prev = m_ref[h][0:1, 0:1]
        log_d = jnp.where(tri, b_col - b_row + li_row, -jnp.inf)
        log_inter = b_col + m_prev
        m_t = jnp.maximum(log_inter, jnp.max(log_d, axis=1, keepdims=True))
        w_intra = jnp.exp(log_d - m_t)
        w_inter = jnp.exp(log_inter - m_t)
        s = lax.dot_general(q, k, (((1,), (1,)), ((), ())), preferred_element_type=F32) * w_intra
        vaug = jnp.concatenate([v, ones_blk], axis=1)
        r1 = jnp.dot(s.astype(BF16), vaug, preferred_element_type=F32)
        cst = c_ref[h]
        r2 = jnp.dot(q, cst.astype(BF16), preferred_element_type=F32)
        num = r1[:, :d] + w_inter * r2[:, :d]
        den = r1[:, d:d + 1] + w_inter * r2[:, d:d + 1]
        hv = num / jnp.maximum(jnp.abs(den), jnp.exp(-m_t))
        b_end = b_col[L - 1:L, :]
        log_w = b_end - b_col + li_col
        m_new = jnp.maximum(b_end + m_prev, jnp.max(log_w, axis=0, keepdims=True))
        w_s = jnp.exp(log_w - m_new)
        decay = jnp.exp(b_end + m_prev - m_new)
        kw = (k.astype(F32) * w_s).astype(BF16)
        upd = lax.dot_general(kw, vaug, (((0,), (0,)), ((), ())), preferred_element_type=F32)
        c_ref[h] = decay * cst + upd
        m_ref[h] = jnp.broadcast_to(m_new, (SUBLANES, LANES))
        hn = hv * lax.rsqrt(jnp.mean(hv * hv, axis=-1, keepdims=True) + RMS_EPS)
        gate = o_ref[:, h * d:(h + 1) * d].astype(F32)
        h_out[:, h * d:(h + 1) * d] = (hn * gate * og_ref[:, h * d:(h + 1) * d]).astype(BF16)


def _mlstm(mq, mk, mv, mo, gcol, grow, og, *, L):
    S = mq.shape[0]
    d = MLSTM_HEAD_DIM
    row = pl.BlockSpec((L, D_MLSTM), lambda i: (i, 0))
    return pl.pallas_call(
        functools.partial(_mlstm_kernel, L=L),
        grid=(S // L,),
        in_specs=[row, row, row, row,
                  pl.BlockSpec((L, 2 * MLSTM_HEADS), lambda i: (i, 0)),
                  pl.BlockSpec((2 * MLSTM_HEADS, L), lambda i: (0, i)),
                  pl.BlockSpec((1, D_MLSTM), lambda i: (0, 0))],
        out_specs=row,
        out_shape=jax.ShapeDtypeStruct((S, D_MLSTM), BF16),
        scratch_shapes=[pltpu.VMEM((MLSTM_HEADS, d, 2 * d), F32),
                        pltpu.VMEM((MLSTM_HEADS, SUBLANES, LANES), F32)],
        compiler_params=_cparams(("arbitrary",)),
        name="mlstm",
    )(mq, mk, mv, mo, gcol, grow, og)


def _layer_norm(z, g, b):
    mu = jnp.mean(z, axis=-1, keepdims=True)
    zc = z - mu
    var = jnp.mean(zc * zc, axis=-1, keepdims=True)
    return zc * lax.rsqrt(var + LN_EPS) * g + b


_R_GROUP0 = N_EXPERTS
_META_W, _META_E = 0, TOP_K


def _first_lane(mask, lane):
    return jnp.min(jnp.where(mask, lane, LANES), axis=-1, keepdims=True)


def _route_tokens(lg):
    lane = lax.broadcasted_iota(jnp.int32, lg.shape, 1)
    is_g = (lane >= _R_GROUP0) & (lane < _R_GROUP0 + N_GROUPS)
    gl = jnp.where(is_g, lg, -jnp.inf)
    ge = jnp.exp(gl - jnp.max(gl, axis=-1, keepdims=True))
    gp = ge / jnp.sum(ge, axis=-1, keepdims=True)
    g_p = jnp.max(gp, axis=-1, keepdims=True)
    g_top = _first_lane((gp == g_p) & is_g, lane) - _R_GROUP0
    in_grp = (lane < N_EXPERTS) & (lax.shift_right_logical(lane, 3) == g_top)
    el = jnp.where(in_grp, lg, -jnp.inf)
    ee = jnp.exp(el - jnp.max(el, axis=-1, keepdims=True))
    p = ee / jnp.sum(ee, axis=-1, keepdims=True)
    p1 = jnp.max(p, axis=-1, keepdims=True)
    i1 = _first_lane((p == p1) & in_grp, lane)
    pm = jnp.where(in_grp & (lane != i1), p, -jnp.inf)
    p2 = jnp.max(pm, axis=-1, keepdims=True)
    i2 = _first_lane(pm == p2, lane)
    den = p1 + p2
    w1 = g_p * (p1 / den)
    w2 = g_p * (p2 / den)
    meta = jnp.where(lane == _META_W, w1, jnp.where(lane == _META_W + 1, w2, 0.0))
    meta = jnp.where(lane == _META_E, i1.astype(F32), jnp.where(lane == _META_E + 1, i2.astype(F32), meta))
    chosen = ((lane == i1) | (lane == i2)).astype(BF16)
    return meta, chosen


def _outproj_kernel(x_ref, a_ref, h_ref, wo_ref, g_ref, b_ref, wr_ref, x1_out, x1t_out, meta_out, oh_out, *,
                    alpha, tm):
    mix = jnp.concatenate([a_ref[...], h_ref[...]], axis=1)
    y = jnp.dot(mix, wo_ref[...], preferred_element_type=F32)
    x1 = _layer_norm(alpha * x_ref[...] + y, g_ref[...], b_ref[...])
    x1_out[...] = x1
    for c in range(SUBLANES):
        x1t_out[pl.ds(c, tm, stride=SUBLANES), :] = x1[:, c * LANES:(c + 1) * LANES]
    lg = jnp.dot(x1, wr_ref[...], preferred_element_type=F32, precision=lax.Precision.HIGHEST)
    meta_out[...], oh_out[...] = _route_tokens(lg)


def _outproj(x, a, hm, wo, g, b, wr, *, tm, alpha):
    S = x.shape[0]
    row = lambda w: pl.BlockSpec((tm, w), lambda i: (i, 0))
    full = lambda arr: pl.BlockSpec(arr.shape, lambda i: (0,) * arr.ndim)
    return pl.pallas_call(
        functools.partial(_outproj_kernel, alpha=alpha, tm=tm),
        grid=(S // tm,),
        in_specs=[row(D_MODEL), row(MLA_HEADS * MLA_V_DIM), row(D_MLSTM), full(wo), full(g), full(b),
                  full(wr)],
        out_specs=(row(D_MODEL), pl.BlockSpec((tm * SUBLANES, LANES), lambda i: (i, 0)), row(LANES),
                   row(LANES)),
        out_shape=(jax.ShapeDtypeStruct((S, D_MODEL), F32),
                   jax.ShapeDtypeStruct((S * SUBLANES, LANES), F32),
                   jax.ShapeDtypeStruct((S, LANES), F32),
                   jax.ShapeDtypeStruct((S, LANES), BF16)),
        compiler_params=_cparams(("arbitrary",)),
        name="outproj",
    )(x, a, hm, wo, g, b, wr)


def _rank_kernel(oh_ref, meta_ref, rank_out, cnt_out, base_ref, *, tr):
    i = pl.program_id(0)

    @pl.when(i == 0)
    def _():
        base_ref[...] = jnp.zeros_like(base_ref)

    oh = oh_ref[...]
    r = lax.broadcasted_iota(jnp.int32, (tr, tr), 0)
    c = lax.broadcasted_iota(jnp.int32, (tr, tr), 1)
    before = (c < r).astype(BF16)
    seen = base_ref[0:1, :] + jnp.dot(before, oh, preferred_element_type=F32)
    lane = lax.broadcasted_iota(jnp.int32, (tr, LANES), 1)
    meta = meta_ref[...]
    out = jnp.zeros((tr, LANES), F32)
    for k in range(TOP_K):
        e_k = meta[:, _META_E + k:_META_E + k + 1].astype(jnp.int32)
        rk = jnp.sum(jnp.where(lane == e_k, seen, 0.0), axis=-1, keepdims=True)
        out = jnp.where(lane == k, rk, out)
    rank_out[...] = out
    total = base_ref[0:1, :] + jnp.sum(oh.astype(F32), axis=0, keepdims=True)
    base_ref[...] = jnp.broadcast_to(total, base_ref.shape)
    cnt_out[...] = jnp.broadcast_to(total, cnt_out.shape)


def _rank(oh, meta, *, tr):
    S = oh.shape[0]
    row = pl.BlockSpec((tr, LANES), lambda i: (i, 0))
    return pl.pallas_call(
        functools.partial(_rank_kernel, tr=tr),
        grid=(S // tr,),
        in_specs=[row, row],
        out_specs=(row, pl.BlockSpec((SUBLANES, LANES), lambda i: (0, 0))),
        out_shape=(jax.ShapeDtypeStruct((S, LANES), F32), jax.ShapeDtypeStruct((SUBLANES, LANES), F32)),
        scratch_shapes=[pltpu.VMEM((SUBLANES, LANES), F32)],
        compiler_params=_cparams(("arbitrary",)),
        name="rank",
    )(oh, meta)


TOKEN_TILE = SUBLANES
DMA_GROUP = 8


def _rows_from_tiles(ref, n):
    return jnp.concatenate([ref[pl.ds(c, n, stride=TOKEN_TILE), :] for c in range(TOKEN_TILE)], axis=1)


def _rows_to_tiles(ref, v):
    for c in range(TOKEN_TILE):
        ref[pl.ds(c, v.shape[0], stride=TOKEN_TILE), :] = v[:, c * LANES:(c + 1) * LANES]


def _pow2_chunks(n, top, fn):
    off = 0
    bit = top
    while bit:
        @pl.when((n & bit) != 0)
        def _(bit=bit, off=off):
            fn(off, bit)
        off = off + (n & bit)
        bit //= 2


def _dispatch_kernel(cnt_ref, pst_ref, d_ref, x_ref, xs_hbm, xbuf, zbuf, sem, zsem, *, tt):
    i = pl.program_id(0)
    n = pl.num_programs(0)
    G = TOKEN_TILE
    slot = lax.rem(i, 2)

    def row_copy(src_row, dst_row, rows, sl):
        return pltpu.make_async_copy(xbuf.at[sl, pl.ds(src_row * G, rows * G)],
                                     xs_hbm.at[pl.ds(dst_row * G, rows * G)], sem.at[sl])

    def zero_copy(dst_row, rows):
        return pltpu.make_async_copy(zbuf.at[pl.ds(0, rows * G)], xs_hbm.at[pl.ds(dst_row * G, rows * G)], zsem)

    @pl.when(i == 0)
    def _():
        zbuf[...] = jnp.zeros_like(zbuf)

        def fill(e, wait):
            pad = (-cnt_ref[e]) & (EXPERT_BLOCK - 1)
            first = pst_ref[e] + cnt_ref[e]
            if wait:
                _pow2_chunks(pad, EXPERT_BLOCK // 2, lambda off, size: zero_copy(0, size).wait())
            else:
                _pow2_chunks(pad, EXPERT_BLOCK // 2, lambda off, size: zero_copy(first + off, size).start())
            return 0
        lax.fori_loop(0, N_EXPERTS, lambda e, c: fill(e, False), 0)
        lax.fori_loop(0, N_EXPERTS, lambda e, c: fill(e, True), 0)
        half = EXPERT_BLOCK // 2
        used = (pst_ref[N_EXPERTS - 1] + cnt_ref[N_EXPERTS - 1] + EXPERT_BLOCK - 1) // EXPERT_BLOCK * 2
        n_half = xs_hbm.shape[0] // (half * G)

        def tail(c, wait):
            if wait:
                zero_copy(0, half).wait()
            else:
                zero_copy(c * half, half).start()
            return 0
        lax.fori_loop(used, n_half, lambda c, _: tail(c, False), 0)
        lax.fori_loop(used, n_half, lambda c, _: tail(c, True), 0)

    def wait_step(sl):
        for _ in range(TOP_K):
            row_copy(0, 0, tt, sl).wait()

    for sl in range(2):
        @pl.when(slot == sl)
        def _(sl=sl):
            @pl.when(i >= 2)
            def _():
                wait_step(sl)
            xbuf[sl] = x_ref[...]
            for t0 in range(0, tt, DMA_GROUP):
                idx = [d_ref[0, 0, j] for j in range(TOP_K * t0, TOP_K * (t0 + DMA_GROUP))]
                for j, dst in enumerate(idx):
                    row_copy(t0 + j // TOP_K, dst, 1, sl).start()

    @pl.when(i == n - 1)
    def _():
        @pl.when(n >= 2)
        def _():
            wait_step(1 - slot)
        wait_step(slot)


def _dispatch(x1t, dest, counts, pad_starts, n_slots, *, tt):
    G = TOKEN_TILE
    T = x1t.shape[0] // G
    nt = T // tt
    grid_spec = pltpu.PrefetchScalarGridSpec(
        num_scalar_prefetch=2,
        grid=(nt,),
        in_specs=[pl.BlockSpec((1, 1, TOP_K * tt), lambda i, c, p: (i, 0, 0), memory_space=pltpu.SMEM),
                  pl.BlockSpec((tt * G, LANES), lambda i, c, p: (i, 0))],
        out_specs=pl.BlockSpec(memory_space=pl.ANY),
        scratch_shapes=[pltpu.VMEM((2, tt * G, LANES), F32),
                        pltpu.VMEM((EXPERT_BLOCK // 2 * G, LANES), F32),
                        pltpu.SemaphoreType.DMA((2,)), pltpu.SemaphoreType.DMA])
    return pl.pallas_call(
        functools.partial(_dispatch_kernel, tt=tt),
        grid_spec=grid_spec,
        out_shape=jax.ShapeDtypeStruct((n_slots * G, LANES), F32),
        compiler_params=_cparams(("arbitrary",)),
        name="dispatch",
    )(counts, pad_starts, dest.reshape(nt, 1, TOP_K * tt), x1t)


def _moe_kernel(bexp_ref, nvb_ref, xs_ref, wg_ref, wu_ref, wd_ref, ys_ref, wgb, wub, wdb):
    b = pl.program_id(0)

    @pl.when((b == 0) | (bexp_ref[b] != bexp_ref[jnp.maximum(b - 1, 0)]))
    def _():
        wgb[...] = wg_ref[0, 0].astype(BF16)
        wub[...] = wu_ref[0, 0].astype(BF16)
        wdb[...] = wd_ref[0, 0].astype(BF16)

    @pl.when(b < nvb_ref[0])
    def _():
        xs = _rows_from_tiles(xs_ref, EXPERT_BLOCK).astype(BF16)
        g = jnp.dot(xs, wgb[...], preferred_element_type=F32)
        u = jnp.dot(xs, wub[...], preferred_element_type=F32)
        hid = (g * (1.0 / (1.0 + jnp.exp(-g)))) * u
        _rows_to_tiles(ys_ref, jnp.dot(hid.astype(BF16), wdb[...], preferred_element_type=F32))

    @pl.when(b >= nvb_ref[0])
    def _():
        ys_ref[...] = jnp.zeros_like(ys_ref)


def _moe(xs, block_expert, n_valid_blocks, wg, wu, wd, layer):
    G = TOKEN_TILE
    assert G * LANES == D_MODEL
    R = EXPERT_BLOCK
    nb = xs.shape[0] // (R * G)
    blk = lambda b, be, nvb: (b, 0)
    wspec = lambda shp: pl.BlockSpec((1, 1) + shp, lambda b, be, nvb: (layer, be[b], 0, 0))
    grid_spec = pltpu.PrefetchScalarGridSpec(
        num_scalar_prefetch=2,
        grid=(nb,),
        in_specs=[pl.BlockSpec((R * G, LANES), blk),
                  wspec((D_MODEL, D_EXPERT)), wspec((D_MODEL, D_EXPERT)), wspec((D_EXPERT, D_MODEL))],
        out_specs=pl.BlockSpec((R * G, LANES), blk),
        scratch_shapes=[pltpu.VMEM((D_MODEL, D_EXPERT), BF16), pltpu.VMEM((D_MODEL, D_EXPERT), BF16),
                        pltpu.VMEM((D_EXPERT, D_MODEL), BF16)])
    return pl.pallas_call(
        _moe_kernel,
        grid_spec=grid_spec,
        out_shape=jax.ShapeDtypeStruct(xs.shape, F32),
        compiler_params=_cparams(("arbitrary",)),
        name="experts",
    )(block_expert, n_valid_blocks, xs, wg, wu, wd)


def _combine_kernel(d_ref, dn_ref, x_ref, w_ref, g_ref, b_ref, ys_hbm, o_ref, ybuf, sem, *, alpha, tt):
    i = pl.program_id(0)
    n = pl.num_programs(0)
    G = TOKEN_TILE
    slot = lax.rem(i, 2)

    def row_copy(src_row, r, rows, sl):
        return pltpu.make_async_copy(ys_hbm.at[pl.ds(src_row * G, rows * G)],
                                     ybuf.at[sl, pl.ds(r * G, rows * G)], sem.at[sl])

    def fetch(idx_ref, sl):
        for t0 in range(0, tt, DMA_GROUP):
            idx = [idx_ref[0, 0, j] for j in range(TOP_K * t0, TOP_K * (t0 + DMA_GROUP))]
            for j, src in enumerate(idx):
                row_copy(src, (j % TOP_K) * tt + t0 + j // TOP_K, 1, sl).start()

    @pl.when(i == 0)
    def _():
        fetch(d_ref, 0)

    for sl in range(2):
        @pl.when(slot == sl)
        def _(sl=sl):
            row_copy(0, 0, TOP_K * tt, sl).wait()
            fetch(dn_ref, 1 - sl)
            w = w_ref[...]
            y = (_rows_from_tiles(ybuf.at[sl, pl.ds(0, tt * G)], tt) * w[:, 0:1]
                 + _rows_from_tiles(ybuf.at[sl, pl.ds(tt * G, tt * G)], tt) * w[:, 1:2])
            o_ref[...] = _layer_norm(alpha * x_ref[...] + y, g_ref[...], b_ref[...])

    @pl.when(i == n - 1)
    def _():
        row_copy(0, 0, TOP_K * tt, 1 - slot).wait()


def _combine(x1, ys, dest, w, g, b, *, tt, alpha):
    S = x1.shape[0]
    nt = S // tt
    G = TOKEN_TILE
    d3 = dest.reshape(nt, 1, TOP_K * tt)
    smem_blk = lambda f: pl.BlockSpec((1, 1, TOP_K * tt), f, memory_space=pltpu.SMEM)
    return pl.pallas_call(
        functools.partial(_combine_kernel, alpha=alpha, tt=tt),
        grid=(nt,),
        in_specs=[smem_blk(lambda i: (i, 0, 0)),
                  smem_blk(lambda i: (jnp.minimum(i + 1, nt - 1), 0, 0)),
                  pl.BlockSpec((tt, D_MODEL), lambda i: (i, 0)),
                  pl.BlockSpec((tt, TOP_K), lambda i: (i, 0)),
                  pl.BlockSpec((1, D_MODEL), lambda i: (0, 0)),
                  pl.BlockSpec((1, D_MODEL), lambda i: (0, 0)),
                  pl.BlockSpec(memory_space=pl.ANY)],
        out_specs=pl.BlockSpec((tt, D_MODEL), lambda i: (i, 0)),
        out_shape=jax.ShapeDtypeStruct((S, D_MODEL), F32),
        scratch_shapes=[pltpu.VMEM((2, TOP_K * tt * G, LANES), F32), pltpu.SemaphoreType.DMA((2,))],
        compiler_params=_cparams(("arbitrary",)),
        name="combine",
    )(d3, d3, x1, w, g, b, ys)


def _route(meta, rank, cnt, T):
    weights = meta[:, _META_W:_META_W + TOP_K]
    flat_e = meta[:, _META_E:_META_E + TOP_K].astype(jnp.int32).reshape(-1)
    n_assign = T * TOP_K
    counts = cnt[0, :N_EXPERTS].astype(jnp.int32)
    padded = (counts + EXPERT_BLOCK - 1) // EXPERT_BLOCK * EXPERT_BLOCK
    pad_ends = jnp.cumsum(padded)
    pad_starts = pad_ends - padded
    dest = pad_starts[flat_e] + rank[:, :TOP_K].astype(jnp.int32).reshape(-1)
    n_blocks = n_assign // EXPERT_BLOCK + N_EXPERTS
    block_start = jnp.arange(n_blocks, dtype=jnp.int32) * EXPERT_BLOCK
    block_expert = jnp.minimum(jnp.sum((pad_ends[None, :] <= block_start[:, None]).astype(jnp.int32), axis=1),
                               N_EXPERTS - 1).astype(jnp.int32)
    n_valid_blocks = (pad_ends[-1:] // EXPERT_BLOCK).astype(jnp.int32)
    return (dest, counts.astype(jnp.int32), pad_starts.astype(jnp.int32), block_expert, n_valid_blocks,
            n_blocks * EXPERT_BLOCK, weights)


def _in_weight(w_in):
    kr0 = MLA_Q_RANK + MLA_KV_RANK
    mq0 = kr0 + MLA_ROPE_DIM
    i0 = mq0 + 4 * D_MLSTM
    half = MLA_ROPE_DIM // 2
    n_small = 2 * MLA_ROPE_DIM + 2 * MLSTM_HEADS
    pad = jnp.zeros(w_in.shape[:-1] + (LANES - n_small,), w_in.dtype)
    return jnp.concatenate(
        [w_in[..., :kr0], w_in[..., mq0:i0], w_in[..., kr0:mq0], w_in[..., kr0 + half:mq0],
         w_in[..., kr0:kr0 + half], w_in[..., i0:i0 + 2 * MLSTM_HEADS], pad], axis=-1).astype(BF16)


def _q_weight(w_qb):
    lead = w_qb.shape[:-1]
    w = w_qb.reshape(lead + (MLA_HEADS, MLA_QK_DIM))
    half = MLA_ROPE_DIM // 2
    z = lambda n: jnp.zeros(lead + (MLA_HEADS, n), w_qb.dtype)
    main = jnp.concatenate([w, z(HEAD_PAD - MLA_QK_DIM)], axis=-1)
    swp = jnp.concatenate([z(MLA_NOPE_DIM), w[..., MLA_NOPE_DIM + half:], w[..., MLA_NOPE_DIM:MLA_NOPE_DIM + half],
                           z(HEAD_PAD - MLA_QK_DIM)], axis=-1)
    hw = MLA_HEADS * HEAD_PAD
    return jnp.concatenate([main.reshape(lead + (hw,)), swp.reshape(lead + (hw,))], axis=-1).astype(BF16)


def _kv_weight(w_kvb_l):
    hw = MLA_HEADS * HEAD_PAD
    w = w_kvb_l.reshape(MLA_KV_RANK, MLA_HEADS, MLA_NOPE_DIM + MLA_V_DIM)
    top_k = jnp.pad(w[:, :, :MLA_NOPE_DIM], ((0, 0), (0, 0), (0, HEAD_PAD - MLA_NOPE_DIM)))
    top_v = jnp.pad(w[:, :, MLA_NOPE_DIM:], ((0, 0), (0, 0), (0, HEAD_PAD - MLA_V_DIM)))
    bot_k = np.zeros((LANES, MLA_HEADS, HEAD_PAD), np.float32)
    bot_v = np.zeros((LANES, MLA_HEADS, HEAD_PAD), np.float32)
    for r in range(MLA_ROPE_DIM):
        bot_k[r, :, MLA_NOPE_DIM + r] = 1.0
    bot_v[_S_ONE, :, MLA_V_DIM] = 1.0
    wk = jnp.concatenate([top_k.reshape(MLA_KV_RANK, hw), jnp.asarray(bot_k).reshape(LANES, hw)], 0)
    wv = jnp.concatenate([top_v.reshape(MLA_KV_RANK, hw), jnp.asarray(bot_v).reshape(LANES, hw)], 0)
    return jnp.concatenate([wk, wv], axis=1).astype(BF16)


def _rope_tables(positions):
    half = MLA_ROPE_DIM // 2
    inv_freq = ROPE_THETA ** (-jnp.arange(half, dtype=F32) / half)
    ang = positions.astype(F32)[:, None] * inv_freq
    cos, sin = jnp.cos(ang), jnp.sin(ang)
    ct = jnp.concatenate([cos, cos], -1)
    st = jnp.concatenate([-sin, sin], -1)
    S = positions.shape[0]
    z = lambda w: jnp.zeros((S, w), F32)
    scale = MLA_QK_DIM ** -0.5 * float(np.log2(np.e))
    tabk = jnp.concatenate([ct, st, z(LANES - 2 * MLA_ROPE_DIM)], -1)
    cqt = jnp.concatenate([jnp.full((S, MLA_NOPE_DIM), scale, F32), ct * scale,
                           z(HEAD_PAD - MLA_QK_DIM)], -1)
    sqt = jnp.concatenate([z(MLA_NOPE_DIM), st * scale, z(HEAD_PAD - MLA_QK_DIM)], -1)
    return tabk, cqt, sqt


def _pick(S, pref):
    t = pref
    while S % t:
        t //= 2
    return t


def kernel(x, positions, w_in, conv_w, conv_b, gate_b, q_a_g, kv_a_g, w_qb, w_kvb, out_g, w_out, ln1_g, ln1_b, w_rg, w_re, w_gate, w_up, w_down, ln2_g, ln2_b):
    B, S, D = x.shape
    assert B == 1 and D == D_MODEL
    depth = w_in.shape[0]
    alpha = float((2 * depth) ** 0.25)
    tm = _pick(S, 512)
    bq = _pick(S, 512)
    L = _pick(S, 256)
    tt = _pick(S, 256)
    assert tm % SUBLANES == 0 and bq % LANES == 0 and L % LANES == 0

    tabk, cqt, sqt = _rope_tables(positions[0])
    win_all = _in_weight(w_in)
    wq_all = _q_weight(w_qb)
    gb_tab = jnp.zeros((depth, 1, LANES), F32).at[:, 0, _S_I0:_S_I0 + 2 * MLSTM_HEADS].set(gate_b)
    wr_all = jnp.concatenate([w_re, w_rg, jnp.zeros((depth, D_MODEL, LANES - N_GROUPS - N_EXPERTS), F32)], -1)
    wo_all = w_out.astype(BF16)
    n_att = MLA_HEADS * MLA_V_DIM

    xs = x[0]
    for l in range(depth):
        q, k, v, mq, mk, mv, mo, gates = _proj(
            xs, win_all[l], wq_all[l], _kv_weight(w_kvb[l]), q_a_g[l][None], kv_a_g[l][None],
            conv_w[l], conv_b[l][None], gb_tab[l], tabk, cqt, sqt, tm=tm)
        a = _attention(q, k, v, out_g[l][None, :n_att], bq=bq)
        gcol = gates[:, _S_I0:_S_I0 + 2 * MLSTM_HEADS]
        hm = _mlstm(mq, mk, mv, mo, gcol, gcol.T, out_g[l][None, n_att:], L=L)
        x1, x1t, meta, chosen = _outproj(xs, a, hm, wo_all[l], ln1_g[l][None], ln1_b[l][None], wr_all[l],
                                         tm=tm, alpha=alpha)
        rank, cnt = _rank(chosen, meta, tr=tm)
        dest, counts, pad_starts, bexp, nvb, n_slots, rw = _route(meta, rank, cnt, S)
        xsort = _dispatch(x1t, dest, counts, pad_starts, n_slots, tt=tt)
        ysort = _moe(xsort, bexp, nvb, w_gate, w_up, w_down, l)
        xs = _combine(x1, ysort, dest, rw, ln2_g[l][None], ln2_b[l][None], tt=tt, alpha=alpha)
    return xs[None]
```

```python
import functools

import numpy as np
import jax
import jax.numpy as jnp
from jax import lax
from jax.experimental import pallas as pl
from jax.experimental.pallas import tpu as pltpu

F32 = jnp.float32
BF16 = jnp.bfloat16

D_MODEL = 1024
MLA_HEADS = 8
MLA_V_DIM = 64
MLA_NOPE_DIM = 64
MLA_ROPE_DIM = 32
MLA_QK_DIM = MLA_NOPE_DIM + MLA_ROPE_DIM
MLA_Q_RANK = 256
MLA_KV_RANK = 128
ROPE_THETA = 10000.0
MLSTM_HEADS = 4
MLSTM_HEAD_DIM = 128
D_MLSTM = MLSTM_HEADS * MLSTM_HEAD_DIM
CONV_WIDTH = 4
N_GROUPS = 8
EXPERTS_PER_GROUP = 8
N_EXPERTS = N_GROUPS * EXPERTS_PER_GROUP
D_EXPERT = 256
TOP_K = 2
EXPERT_BLOCK = 128
LN_EPS = 1e-5
RMS_EPS = 1e-6

LANES = 128
SUBLANES = 8
HEAD_PAD = 128
VMEM_LIMIT = 56 * 1024 * 1024

_C_Q0 = 0
_C_KV0 = MLA_Q_RANK
_C_MQK0 = _C_KV0 + MLA_KV_RANK
_C_MV0 = _C_MQK0 + 2 * D_MLSTM
_C_MO0 = _C_MV0 + D_MLSTM
_C_SMALL0 = _C_MO0 + D_MLSTM
D_IN_PAD = _C_SMALL0 + LANES
_S_I0 = 2 * MLA_ROPE_DIM
_S_F0 = _S_I0 + MLSTM_HEADS
_S_ONE = MLA_ROPE_DIM


def _cparams(sem, vmem=VMEM_LIMIT):
    return pltpu.CompilerParams(dimension_semantics=sem, vmem_limit_bytes=vmem)


def _proj_kernel(x_ref, win_ref, wq_ref, wkv_ref, qg_ref, kvg_ref, cw_ref, cb_ref, gb_ref,
                 tabk_ref, cq_ref, sq_ref,
                 q_out, k_out, v_out, mq_out, mk_out, mv_out, mo_out, g_out, carry_ref, *, tm):
    i = pl.program_id(0)

    @pl.when(i == 0)
    def _():
        carry_ref[...] = jnp.zeros_like(carry_ref)

    xb = x_ref[...].astype(BF16)
    p = jnp.dot(xb, win_ref[...], preferred_element_type=F32)

    def rms(v, g):
        return v * lax.rsqrt(jnp.mean(v * v, axis=-1, keepdims=True) + RMS_EPS) * g

    cqn = rms(p[:, _C_Q0:_C_KV0], qg_ref[...]).astype(BF16)
    ckvn = rms(p[:, _C_KV0:_C_MQK0], kvg_ref[...]).astype(BF16)

    qq = jnp.dot(cqn, wq_ref[...], preferred_element_type=F32)
    cq = cq_ref[...]
    sq = sq_ref[...]
    hw = MLA_HEADS * HEAD_PAD
    for h in range(MLA_HEADS):
        a = qq[:, h * HEAD_PAD:(h + 1) * HEAD_PAD]
        b = qq[:, hw + h * HEAD_PAD: hw + (h + 1) * HEAD_PAD]
        q_out[:, h * HEAD_PAD:(h + 1) * HEAD_PAD] = (a * cq + b * sq).astype(BF16)

    small = p[:, _C_SMALL0:D_IN_PAD]
    lane = lax.broadcasted_iota(jnp.int32, small.shape, 1)
    prod = small * tabk_ref[...]
    kr = prod + pltpu.roll(prod, LANES - MLA_ROPE_DIM, 1)
    kvs = jnp.where(lane < MLA_ROPE_DIM, kr, jnp.where(lane == _S_ONE, 1.0, 0.0))
    kvin = jnp.concatenate([ckvn, kvs.astype(BF16)], axis=1)
    kv = jnp.dot(kvin, wkv_ref[...], preferred_element_type=F32)
    k_out[...] = kv[:, :hw].astype(BF16)
    v_out[...] = kv[:, hw:].T.astype(BF16)

    g = small + gb_ref[...]
    lsig = jnp.minimum(g, 0.0) - jnp.log1p(jnp.exp(-jnp.abs(g)))
    is_f = (lane >= _S_F0) & (lane < _S_F0 + MLSTM_HEADS)
    g_out[...] = jnp.where(is_f, lsig, g)

    mqk = p[:, _C_MQK0:_C_MV0]
    ext = jnp.concatenate([carry_ref[...], mqk], axis=0)
    carry_ref[...] = mqk[tm - SUBLANES:tm, :]
    acc = cb_ref[...] + mqk * cw_ref[CONV_WIDTH - 1:CONV_WIDTH, :]
    for j in range(CONV_WIDTH - 1):
        off = SUBLANES - (CONV_WIDTH - 1) + j
        acc = acc + ext[off:off + tm, :] * cw_ref[j:j + 1, :]
    y = acc * (1.0 / (1.0 + jnp.exp(-acc)))
    mq_out[...] = (y[:, :D_MLSTM] * (MLSTM_HEAD_DIM ** -0.5)).astype(BF16)
    mk_out[...] = y[:, D_MLSTM:].astype(BF16)
    mv_out[...] = p[:, _C_MV0:_C_MO0].astype(BF16)
    mo = p[:, _C_MO0:_C_SMALL0]
    mo_out[...] = (1.0 / (1.0 + jnp.exp(-mo))).astype(BF16)


def _proj(x, win, wq, wkv, qg, kvg, cw, cb, gb, tabk, cqt, sqt, *, tm):
    S = x.shape[0]
    hw = MLA_HEADS * HEAD_PAD
    row = lambda w: pl.BlockSpec((tm, w), lambda i: (i, 0))
    full = lambda a: pl.BlockSpec(a.shape, lambda i: (0,) * a.ndim)
    out_shapes = (
        jax.ShapeDtypeStruct((S, hw), BF16), jax.ShapeDtypeStruct((S, hw), BF16),
        jax.ShapeDtypeStruct((hw, S), BF16),
        jax.ShapeDtypeStruct((S, D_MLSTM), BF16), jax.ShapeDtypeStruct((S, D_MLSTM), BF16),
        jax.ShapeDtypeStruct((S, D_MLSTM), BF16), jax.ShapeDtypeStruct((S, D_MLSTM), BF16),
        jax.ShapeDtypeStruct((S, LANES), F32))
    return pl.pallas_call(
        functools.partial(_proj_kernel, tm=tm),
        grid=(S // tm,),
        in_specs=[row(D_MODEL), full(win), full(wq), full(wkv), full(qg), full(kvg), full(cw),
                  full(cb), full(gb), row(LANES), row(LANES), row(LANES)],
        out_specs=(row(hw), row(hw), pl.BlockSpec((hw, tm), lambda i: (0, i)), row(D_MLSTM), row(D_MLSTM), row(D_MLSTM),
                   row(D_MLSTM), row(LANES)),
        out_shape=out_shapes,
        scratch_shapes=[pltpu.VMEM((SUBLANES, 2 * D_MLSTM), F32)],
        compiler_params=_cparams(("arbitrary",)),
        name="proj",
    )(x, win, wq, wkv, qg, kvg, cw, cb, gb, tabk, cqt, sqt)


VT_ROWS = 80


def _attn_kernel(q_ref, k_ref, vt_ref, og_ref, o_ref, s_ref, *, bq, heads_per_step):
    qi = pl.program_id(1)
    nh = heads_per_step
    cols = [(h * HEAD_PAD, (h + 1) * HEAD_PAD) for h in range(nh)]
    qs = [q_ref[:, c0:c1] for c0, c1 in cols]

    def scores(j, h):
        kt = k_ref[pl.ds(pl.multiple_of(j * bq, bq), bq), cols[h][0]:cols[h][1]]
        return lax.dot_general(kt, qs[h], (((1,), (1,)), ((), ())), preferred_element_type=F32)

    def accumulate(j, h, s, m, acc):
        vt = vt_ref[h * HEAD_PAD:h * HEAD_PAD + VT_ROWS, pl.ds(pl.multiple_of(j * bq, bq), bq)]
        m_new = jnp.maximum(m, jnp.max(s, axis=0, keepdims=True))
        alpha = jnp.exp2(m - m_new)
        p = jnp.exp2(s - m_new)
        return m_new, alpha * acc + jnp.dot(vt, p.astype(BF16), preferred_element_type=F32)

    def step(j, carry, cur, diagonal=False, prefetch=True):
        new = []
        for h in range(nh):
            m, acc = carry[2 * h:2 * h + 2]
            if prefetch:
                s_ref[(1 - cur) * nh + h] = scores(j + 1, h)
            s = s_ref[cur * nh + h]
            if diagonal:
                key = lax.broadcasted_iota(jnp.int32, (bq, bq), 0)
                qry = lax.broadcasted_iota(jnp.int32, (bq, bq), 1)
                s = jnp.where(key <= qry, s, -jnp.inf)
            new += list(accumulate(j, h, s, m, acc))
        return tuple(new)

    def pair(i, carry):
        return step(2 * i + 1, step(2 * i, carry, 0), 1)

    init = []
    for h in range(nh):
        s_ref[h] = scores(0, h)
        init += [jnp.full((1, bq), -jnp.inf, F32), jnp.zeros((VT_ROWS, bq), F32)]
    n_pairs = lax.shift_right_logical(qi, 1)
    n_quads = lax.shift_right_logical(qi, 2)
    carry = lax.fori_loop(0, n_quads, lambda i, cr: pair(2 * i + 1, pair(2 * i, cr)), tuple(init))
    carry = lax.fori_loop(2 * n_quads, n_pairs, pair, carry)
    carry = lax.cond(
        qi % 2 == 1,
        lambda cr: step(qi, step(qi - 1, cr, 0), 1, diagonal=True, prefetch=False),
        lambda cr: step(qi, cr, 0, diagonal=True, prefetch=False),
        carry)
    outs = []
    for h in range(nh):
        acc = carry[2 * h + 1]
        o = acc[:MLA_V_DIM, :] / acc[MLA_V_DIM:MLA_V_DIM + 1, :]
        o = o * lax.rsqrt(jnp.mean(o * o, axis=0, keepdims=True) + RMS_EPS)
        outs.append(o.T)
    o_ref[...] = (jnp.concatenate(outs, axis=1) * og_ref[...]).astype(BF16)


def _attention(q, k, vt, og, *, bq, heads_per_step=2):
    S = q.shape[0]
    n_hp = MLA_HEADS // heads_per_step
    wq = heads_per_step * HEAD_PAD
    wo = heads_per_step * MLA_V_DIM
    return pl.pallas_call(
        functools.partial(_attn_kernel, bq=bq, heads_per_step=heads_per_step),
        grid=(n_hp, S // bq),
        in_specs=[pl.BlockSpec((bq, wq), lambda h, i: (i, h)),
                  pl.BlockSpec((S, wq), lambda h, i: (0, h)),
                  pl.BlockSpec((wq, S), lambda h, i: (h, 0)),
                  pl.BlockSpec((1, wo), lambda h, i: (0, h))],
        out_specs=pl.BlockSpec((bq, wo), lambda h, i: (i, h)),
        out_shape=jax.ShapeDtypeStruct((S, MLA_HEADS * MLA_V_DIM), BF16),
        scratch_shapes=[pltpu.VMEM((2 * heads_per_step, bq, bq), F32)],
        compiler_params=_cparams(("arbitrary", "arbitrary")),
        name="attention",
    )(q, k, vt, og)


def _mlstm_kernel(q_ref, k_ref, v_ref, o_ref, gc_ref, gr_ref, og_ref, h_out, c_ref, m_ref, *, L):
    ci = pl.program_id(0)

    @pl.when(ci == 0)
    def _():
        c_ref[...] = jnp.zeros_like(c_ref)
        m_ref[...] = jnp.zeros_like(m_ref)

    r = lax.broadcasted_iota(jnp.int32, (L, L), 0)
    c = lax.broadcasted_iota(jnp.int32, (L, L), 1)
    tri = c <= r
    lane = lax.broadcasted_iota(jnp.int32, (L, LANES), 1)
    ones_blk = jnp.where(lane == 0, 1.0, 0.0).astype(BF16)
    gc = gc_ref[...]
    gr = gr_ref[...]
    d = MLSTM_HEAD_DIM
    for h in range(MLSTM_HEADS):
        q = q_ref[:, h * d:(h + 1) * d]
        k = k_ref[:, h * d:(h + 1) * d]
        v = v_ref[:, h * d:(h + 1) * d]
        li_col = gc[:, h:h + 1]
        lf_col = gc[:, MLSTM_HEADS + h:MLSTM_HEADS + h + 1]
        li_row = gr[h:h + 1, :]
        lf_row = gr[MLSTM_HEADS + h:MLSTM_HEADS + h + 1, :]
        b_col = jnp.sum(jnp.where(tri, lf_row, 0.0), axis=1, keepdims=True)
        b_row = jnp.sum(jnp.where(r <= c, lf_col, 0.0), axis=0, keepdims=True)
        m_prev = m_ref[h][0:1, 0:1]
        log_d = jnp.where(tri, b_col - b_row + li_row, -jnp.inf)
        log_inter = b_col + m_prev
        m_t = jnp.maximum(log_inter, jnp.max(log_d, axis=1, keepdims=True))
        w_intra = jnp.exp(log_d - m_t)
        w_inter = jnp.exp(log_inter - m_t)
        s = lax.dot_general(q, k, (((1,), (1,)), ((), ())), preferred_element_type=F32) * w_intra
        vaug = jnp.concatenate([v, ones_blk], axis=1)
        r1 = jnp.dot(s.astype(BF16), vaug, preferred_element_type=F32)
        cst = c_ref[h]
        r2 = jnp.dot(q, cst.astype(BF16), preferred_element_type=F32)
        num = r1[:, :d] + w_inter * r2[:, :d]
        den = r1[:, d:d + 1] + w_inter * r2[:, d:d + 1]
        hv = num / jnp.maximum(jnp.abs(den), jnp.exp(-m_t))
        b_end = b_col[L - 1:L, :]
        log_w = b_end - b_col + li_col
        m_new = jnp.maximum(b_end + m_prev, jnp.max(log_w, axis=0, keepdims=True))
        w_s = jnp.exp(log_w - m_new)
        decay = jnp.exp(b_end + m_prev - m_new)
        kw = (k.astype(F32) * w_s).astype(BF16)
        upd = lax.dot_general(kw, vaug, (((0,), (0,)), ((), ())), preferred_element_type=F32)
        c_ref[h] = decay * cst + upd
        m_ref[h] = jnp.broadcast_to(m_new, (SUBLANES, LANES))
        hn = hv * lax.rsqrt(jnp.mean(hv * hv, axis=-1, keepdims=True) + RMS_EPS)
        gate = o_ref[:, h * d:(h + 1) * d].astype(F32)
        h_out[:, h * d:(h + 1) * d] = (hn * gate * og_ref[:, h * d:(h + 1) * d]).astype(BF16)


def _mlstm(mq, mk, mv, mo, gcol, grow, og, *, L):
    S = mq.shape[0]
    d = MLSTM_HEAD_DIM
    row = pl.BlockSpec((L, D_MLSTM), lambda i: (i, 0))
    return pl.pallas_call(
        functools.partial(_mlstm_kernel, L=L),
        grid=(S // L,),
        in_specs=[row, row, row, row,
                  pl.BlockSpec((L, 2 * MLSTM_HEADS), lambda i: (i, 0)),
                  pl.BlockSpec((2 * MLSTM_HEADS, L), lambda i: (0, i)),
                  pl.BlockSpec((1, D_MLSTM), lambda i: (0, 0))],
        out_specs=row,
        out_shape=jax.ShapeDtypeStruct((S, D_MLSTM), BF16),
        scratch_shapes=[pltpu.VMEM((MLSTM_HEADS, d, 2 * d), F32),
                        pltpu.VMEM((MLSTM_HEADS, SUBLANES, LANES), F32)],
        compiler_params=_cparams(("arbitrary",)),
        name="mlstm",
    )(mq, mk, mv, mo, gcol, grow, og)


def _layer_norm(z, g, b):
    mu = jnp.mean(z, axis=-1, keepdims=True)
    zc = z - mu
    var = jnp.mean(zc * zc, axis=-1, keepdims=True)
    return zc * lax.rsqrt(var + LN_EPS) * g + b


_R_GROUP0 = N_EXPERTS
_META_W, _META_E = 0, TOP_K


def _first_lane(mask, lane):
    return jnp.min(jnp.where(mask, lane, LANES), axis=-1, keepdims=True)


def _route_tokens(lg):
    lane = lax.broadcasted_iota(jnp.int32, lg.shape, 1)
    is_g = (lane >= _R_GROUP0) & (lane < _R_GROUP0 + N_GROUPS)
    gl = jnp.where(is_g, lg, -jnp.inf)
    ge = jnp.exp(gl - jnp.max(gl, axis=-1, keepdims=True))
    gp = ge / jnp.sum(ge, axis=-1, keepdims=True)
    g_p = jnp.max(gp, axis=-1, keepdims=True)
    g_top = _first_lane((gp == g_p) & is_g, lane) - _R_GROUP0
    in_grp = (lane < N_EXPERTS) & (lax.shift_right_logical(lane, 3) == g_top)
    el = jnp.where(in_grp, lg, -jnp.inf)
    ee = jnp.exp(el - jnp.max(el, axis=-1, keepdims=True))
    p = ee / jnp.sum(ee, axis=-1, keepdims=True)
    p1 = jnp.max(p, axis=-1, keepdims=True)
    i1 = _first_lane((p == p1) & in_grp, lane)
    pm = jnp.where(in_grp & (lane != i1), p, -jnp.inf)
    p2 = jnp.max(pm, axis=-1, keepdims=True)
    i2 = _first_lane(pm == p2, lane)
    den = p1 + p2
    w1 = g_p * (p1 / den)
    w2 = g_p * (p2 / den)
    meta = jnp.where(lane == _META_W, w1, jnp.where(lane == _META_W + 1, w2, 0.0))
    meta = jnp.where(lane == _META_E, i1.astype(F32), jnp.where(lane == _META_E + 1, i2.astype(F32), meta))
    chosen = ((lane == i1) | (lane == i2)).astype(BF16)
    return meta, chosen


def _outproj_kernel(x_ref, a_ref, h_ref, wo_ref, g_ref, b_ref, wr_ref, x1t_out, meta_out, oh_out, *,
                    alpha, tm):
    mix = jnp.concatenate([a_ref[...], h_ref[...]], axis=1)
    y = jnp.dot(mix, wo_ref[...], preferred_element_type=F32)
    x1 = _layer_norm(alpha * x_ref[...] + y, g_ref[...], b_ref[...])
    for c in range(SUBLANES):
        x1t_out[pl.ds(c, tm, stride=SUBLANES), :] = x1[:, c * LANES:(c + 1) * LANES]
    lg = jnp.dot(x1, wr_ref[...], preferred_element_type=F32, precision=lax.Precision.HIGHEST)
    meta_out[...], oh_out[...] = _route_tokens(lg)


def _outproj(x, a, hm, wo, g, b, wr, *, tm, alpha):
    S = x.shape[0]
    row = lambda w: pl.BlockSpec((tm, w), lambda i: (i, 0))
    full = lambda arr: pl.BlockSpec(arr.shape, lambda i: (0,) * arr.ndim)
    return pl.pallas_call(
        functools.partial(_outproj_kernel, alpha=alpha, tm=tm),
        grid=(S // tm,),
        in_specs=[row(D_MODEL), row(MLA_HEADS * MLA_V_DIM), row(D_MLSTM), full(wo), full(g), full(b),
                  full(wr)],
        out_specs=(pl.BlockSpec((tm * SUBLANES, LANES), lambda i: (i, 0)), row(LANES), row(LANES)),
        out_shape=(jax.ShapeDtypeStruct((S * SUBLANES, LANES), F32),
                   jax.ShapeDtypeStruct((S, LANES), F32),
                   jax.ShapeDtypeStruct((S, LANES), BF16)),
        compiler_params=_cparams(("arbitrary",)),
        name="outproj",
    )(x, a, hm, wo, g, b, wr)


def _rank_kernel(oh_ref, meta_ref, rank_out, cnt_out, base_ref, *, tr):
    i = pl.program_id(0)

    @pl.when(i == 0)
    def _():
        base_ref[...] = jnp.zeros_like(base_ref)

    oh = oh_ref[...]
    r = lax.broadcasted_iota(jnp.int32, (tr, tr), 0)
    c = lax.broadcasted_iota(jnp.int32, (tr, tr), 1)
    before = (c < r).astype(BF16)
    seen = base_ref[0:1, :] + jnp.dot(before, oh, preferred_element_type=F32)
    lane = lax.broadcasted_iota(jnp.int32, (tr, LANES), 1)
    meta = meta_ref[...]
    out = jnp.zeros((tr, LANES), F32)
    for k in range(TOP_K):
        e_k = meta[:, _META_E + k:_META_E + k + 1].astype(jnp.int32)
        rk = jnp.sum(jnp.where(lane == e_k, seen, 0.0), axis=-1, keepdims=True)
        out = jnp.where(lane == k, rk, out)
    rank_out[...] = out
    total = base_ref[0:1, :] + jnp.sum(oh.astype(F32), axis=0, keepdims=True)
    base_ref[...] = jnp.broadcast_to(total, base_ref.shape)
    cnt_out[...] = jnp.broadcast_to(total, cnt_out.shape)


def _rank(oh, meta, *, tr):
    S = oh.shape[0]
    row = pl.BlockSpec((tr, LANES), lambda i: (i, 0))
    return pl.pallas_call(
        functools.partial(_rank_kernel, tr=tr),
        grid=(S // tr,),
        in_specs=[row, row],
        out_specs=(row, pl.BlockSpec((SUBLANES, LANES), lambda i: (0, 0))),
        out_shape=(jax.ShapeDtypeStruct((S, LANES), F32), jax.ShapeDtypeStruct((SUBLANES, LANES), F32)),
        scratch_shapes=[pltpu.VMEM((SUBLANES, LANES), F32)],
        compiler_params=_cparams(("arbitrary",)),
        name="rank",
    )(oh, meta)


TOKEN_TILE = SUBLANES
DMA_GROUP = 8


def _rows_from_tiles(ref, n):
    return jnp.concatenate([ref[pl.ds(c, n, stride=TOKEN_TILE), :] for c in range(TOKEN_TILE)], axis=1)


def _rows_to_tiles(ref, v):
    for c in range(TOKEN_TILE):
        ref[pl.ds(c, v.shape[0], stride=TOKEN_TILE), :] = v[:, c * LANES:(c + 1) * LANES]


def _pow2_chunks(n, top, fn):
    off = 0
    bit = top
    while bit:
        @pl.when((n & bit) != 0)
        def _(bit=bit, off=off):
            fn(off, bit)
        off = off + (n & bit)
        bit //= 2


def _dispatch_kernel(cnt_ref, pst_ref, d_ref, x_ref, xs_hbm, xbuf, zbuf, sem, zsem, *, tt):
    i = pl.program_id(0)
    n = pl.num_programs(0)
    G = TOKEN_TILE
    slot = lax.rem(i, 2)

    def row_copy(src_row, dst_row, rows, sl):
        return pltpu.make_async_copy(xbuf.at[sl, pl.ds(src_row * G, rows * G)],
                                     xs_hbm.at[pl.ds(dst_row * G, rows * G)], sem.at[sl])

    def zero_copy(dst_row, rows):
        return pltpu.make_async_copy(zbuf.at[pl.ds(0, rows * G)], xs_hbm.at[pl.ds(dst_row * G, rows * G)], zsem)

    @pl.when(i == 0)
    def _():
        zbuf[...] = jnp.zeros_like(zbuf)

        def fill(e, wait):
            pad = (-cnt_ref[e]) & (EXPERT_BLOCK - 1)
            first = pst_ref[e] + cnt_ref[e]
            if wait:
                _pow2_chunks(pad, EXPERT_BLOCK // 2, lambda off, size: zero_copy(0, size).wait())
            else:
                _pow2_chunks(pad, EXPERT_BLOCK // 2, lambda off, size: zero_copy(first + off, size).start())
            return 0
        lax.fori_loop(0, N_EXPERTS, lambda e, c: fill(e, False), 0)
        lax.fori_loop(0, N_EXPERTS, lambda e, c: fill(e, True), 0)
        half = EXPERT_BLOCK // 2
        used = (pst_ref[N_EXPERTS - 1] + cnt_ref[N_EXPERTS - 1] + EXPERT_BLOCK - 1) // EXPERT_BLOCK * 2
        n_half = xs_hbm.shape[0] // (half * G)

        def tail(c, wait):
            if wait:
                zero_copy(0, half).wait()
            else:
                zero_copy(c * half, half).start()
            return 0
        lax.fori_loop(used, n_half, lambda c, _: tail(c, False), 0)
        lax.fori_loop(used, n_half, lambda c, _: tail(c, True), 0)

    def wait_step(sl):
        for _ in range(TOP_K):
            row_copy(0, 0, tt, sl).wait()

    for sl in range(2):
        @pl.when(slot == sl)
        def _(sl=sl):
            @pl.when(i >= 2)
            def _():
                wait_step(sl)
            xbuf[sl] = x_ref[...]
            for t0 in range(0, tt, DMA_GROUP):
                idx = [d_ref[0, 0, j] for j in range(TOP_K * t0, TOP_K * (t0 + DMA_GROUP))]
                for j, dst in enumerate(idx):
                    row_copy(t0 + j // TOP_K, dst, 1, sl).start(priority=j % 2)

    @pl.when(i == n - 1)
    def _():
        @pl.when(n >= 2)
        def _():
            wait_step(1 - slot)
        wait_step(slot)


def _dispatch(x1t, dest, counts, pad_starts, n_slots, *, tt):
    G = TOKEN_TILE
    T = x1t.shape[0] // G
    nt = T // tt
    grid_spec = pltpu.PrefetchScalarGridSpec(
        num_scalar_prefetch=2,
        grid=(nt,),
        in_specs=[pl.BlockSpec((1, 1, TOP_K * tt), lambda i, c, p: (i, 0, 0), memory_space=pltpu.SMEM),
                  pl.BlockSpec((tt * G, LANES), lambda i, c, p: (i, 0))],
        out_specs=pl.BlockSpec(memory_space=pl.ANY),
        scratch_shapes=[pltpu.VMEM((2, tt * G, LANES), F32),
                        pltpu.VMEM((EXPERT_BLOCK // 2 * G, LANES), F32),
                        pltpu.SemaphoreType.DMA((2,)), pltpu.SemaphoreType.DMA])
    return pl.pallas_call(
        functools.partial(_dispatch_kernel, tt=tt),
        grid_spec=grid_spec,
        out_shape=jax.ShapeDtypeStruct((n_slots * G, LANES), F32),
        compiler_params=_cparams(("arbitrary",)),
        name="dispatch",
    )(counts, pad_starts, dest.reshape(nt, 1, TOP_K * tt), x1t)


def _moe_kernel(bexp_ref, nvb_ref, xs_ref, wg_ref, wu_ref, wd_ref, ys_ref, wgb, wub, wdb):
    b = pl.program_id(0)

    @pl.when((b == 0) | (bexp_ref[b] != bexp_ref[jnp.maximum(b - 1, 0)]))
    def _():
        wgb[...] = wg_ref[0, 0].astype(BF16)
        wub[...] = wu_ref[0, 0].astype(BF16)
        wdb[...] = wd_ref[0, 0].astype(BF16)

    @pl.when(b < nvb_ref[0])
    def _():
        xs = _rows_from_tiles(xs_ref, EXPERT_BLOCK).astype(BF16)
        g = jnp.dot(xs, wgb[...], preferred_element_type=F32)
        u = jnp.dot(xs, wub[...], preferred_element_type=F32)
        hid = (g * (1.0 / (1.0 + jnp.exp(-g)))) * u
        _rows_to_tiles(ys_ref, jnp.dot(hid.astype(BF16), wdb[...], preferred_element_type=F32))

    @pl.when(b >= nvb_ref[0])
    def _():
        ys_ref[...] = jnp.zeros_like(ys_ref)


def _moe(xs, block_expert, n_valid_blocks, wg, wu, wd, layer):
    G = TOKEN_TILE
    assert G * LANES == D_MODEL
    R = EXPERT_BLOCK
    nb = xs.shape[0] // (R * G)
    blk = lambda b, be, nvb: (b, 0)
    wspec = lambda shp: pl.BlockSpec((1, 1) + shp, lambda b, be, nvb: (layer, be[b], 0, 0))
    grid_spec = pltpu.PrefetchScalarGridSpec(
        num_scalar_prefetch=2,
        grid=(nb,),
        in_specs=[pl.BlockSpec((R * G, LANES), blk),
                  wspec((D_MODEL, D_EXPERT)), wspec((D_MODEL, D_EXPERT)), wspec((D_EXPERT, D_MODEL))],
        out_specs=pl.BlockSpec((R * G, LANES), blk),
        scratch_shapes=[pltpu.VMEM((D_MODEL, D_EXPERT), BF16), pltpu.VMEM((D_MODEL, D_EXPERT), BF16),
                        pltpu.VMEM((D_EXPERT, D_MODEL), BF16)])
    return pl.pallas_call(
        _moe_kernel,
        grid_spec=grid_spec,
        out_shape=jax.ShapeDtypeStruct(xs.shape, F32),
        compiler_params=_cparams(("arbitrary",)),
        name="experts",
    )(block_expert, n_valid_blocks, xs, wg, wu, wd)


def _combine_kernel(d_ref, dn_ref, x_ref, w_ref, g_ref, b_ref, ys_hbm, o_ref, ybuf, sem, *, alpha, tt):
    i = pl.program_id(0)
    n = pl.num_programs(0)
    G = TOKEN_TILE
    slot = lax.rem(i, 2)

    def row_copy(src_row, r, rows, sl):
        return pltpu.make_async_copy(ys_hbm.at[pl.ds(src_row * G, rows * G)],
                                     ybuf.at[sl, pl.ds(r * G, rows * G)], sem.at[sl])

    def fetch(idx_ref, sl):
        for t0 in range(0, tt, DMA_GROUP):
            idx = [idx_ref[0, 0, j] for j in range(TOP_K * t0, TOP_K * (t0 + DMA_GROUP))]
            for j, src in enumerate(idx):
                row_copy(src, (j % TOP_K) * tt + t0 + j // TOP_K, 1, sl).start(priority=j % 2)

    @pl.when(i == 0)
    def _():
        fetch(d_ref, 0)

    for sl in range(2):
        @pl.when(slot == sl)
        def _(sl=sl):
            row_copy(0, 0, TOP_K * tt, sl).wait()
            fetch(dn_ref, 1 - sl)
            w = w_ref[...]
            y = (_rows_from_tiles(ybuf.at[sl, pl.ds(0, tt * G)], tt) * w[:, 0:1]
                 + _rows_from_tiles(ybuf.at[sl, pl.ds(tt * G, tt * G)], tt) * w[:, 1:2])
            o_ref[...] = _layer_norm(alpha * _rows_from_tiles(x_ref, tt) + y, g_ref[...], b_ref[...])

    @pl.when(i == n - 1)
    def _():
        row_copy(0, 0, TOP_K * tt, 1 - slot).wait()


def _combine(x1t, ys, dest, w, g, b, *, tt, alpha):
    G = TOKEN_TILE
    S = x1t.shape[0] // G
    nt = S // tt
    d3 = dest.reshape(nt, 1, TOP_K * tt)
    smem_blk = lambda f: pl.BlockSpec((1, 1, TOP_K * tt), f, memory_space=pltpu.SMEM)
    return pl.pallas_call(
        functools.partial(_combine_kernel, alpha=alpha, tt=tt),
        grid=(nt,),
        in_specs=[smem_blk(lambda i: (i, 0, 0)),
                  smem_blk(lambda i: (jnp.minimum(i + 1, nt - 1), 0, 0)),
                  pl.BlockSpec((tt * G, LANES), lambda i: (i, 0)),
                  pl.BlockSpec((tt, TOP_K), lambda i: (i, 0)),
                  pl.BlockSpec((1, D_MODEL), lambda i: (0, 0)),
                  pl.BlockSpec((1, D_MODEL), lambda i: (0, 0)),
                  pl.BlockSpec(memory_space=pl.ANY)],
        out_specs=pl.BlockSpec((tt, D_MODEL), lambda i: (i, 0)),
        out_shape=jax.ShapeDtypeStruct((S, D_MODEL), F32),
        scratch_shapes=[pltpu.VMEM((2, TOP_K * tt * G, LANES), F32), pltpu.SemaphoreType.DMA((2,))],
        compiler_params=_cparams(("arbitrary",)),
        name="combine",
    )(d3, d3, x1t, w, g, b, ys)


def _route(meta, rank, cnt, T):
    weights = meta[:, _META_W:_META_W + TOP_K]
    flat_e = meta[:, _META_E:_META_E + TOP_K].astype(jnp.int32).reshape(-1)
    n_assign = T * TOP_K
    counts = cnt[0, :N_EXPERTS].astype(jnp.int32)
    padded = (counts + EXPERT_BLOCK - 1) // EXPERT_BLOCK * EXPERT_BLOCK
    pad_ends = jnp.cumsum(padded)
    pad_starts = pad_ends - padded
    dest = pad_starts[flat_e] + rank[:, :TOP_K].astype(jnp.int32).reshape(-1)
    n_blocks = n_assign // EXPERT_BLOCK + N_EXPERTS
    block_start = jnp.arange(n_blocks, dtype=jnp.int32) * EXPERT_BLOCK
    block_expert = jnp.minimum(jnp.sum((pad_ends[None, :] <= block_start[:, None]).astype(jnp.int32), axis=1),
                               N_EXPERTS - 1).astype(jnp.int32)
    n_valid_blocks = (pad_ends[-1:] // EXPERT_BLOCK).astype(jnp.int32)
    return (dest, counts.astype(jnp.int32), pad_starts.astype(jnp.int32), block_expert, n_valid_blocks,
            n_blocks * EXPERT_BLOCK, weights)


def _in_weight(w_in):
    kr0 = MLA_Q_RANK + MLA_KV_RANK
    mq0 = kr0 + MLA_ROPE_DIM
    i0 = mq0 + 4 * D_MLSTM
    half = MLA_ROPE_DIM // 2
    n_small = 2 * MLA_ROPE_DIM + 2 * MLSTM_HEADS
    pad = jnp.zeros(w_in.shape[:-1] + (LANES - n_small,), w_in.dtype)
    return jnp.concatenate(
        [w_in[..., :kr0], w_in[..., mq0:i0], w_in[..., kr0:mq0], w_in[..., kr0 + half:mq0],
         w_in[..., kr0:kr0 + half], w_in[..., i0:i0 + 2 * MLSTM_HEADS], pad], axis=-1).astype(BF16)


def _q_weight(w_qb):
    lead = w_qb.shape[:-1]
    w = w_qb.reshape(lead + (MLA_HEADS, MLA_QK_DIM))
    half = MLA_ROPE_DIM // 2
    z = lambda n: jnp.zeros(lead + (MLA_HEADS, n), w_qb.dtype)
    main = jnp.concatenate([w, z(HEAD_PAD - MLA_QK_DIM)], axis=-1)
    swp = jnp.concatenate([z(MLA_NOPE_DIM), w[..., MLA_NOPE_DIM + half:], w[..., MLA_NOPE_DIM:MLA_NOPE_DIM + half],
                           z(HEAD_PAD - MLA_QK_DIM)], axis=-1)
    hw = MLA_HEADS * HEAD_PAD
    return jnp.concatenate([main.reshape(lead + (hw,)), swp.reshape(lead + (hw,))], axis=-1).astype(BF16)


def _kv_weight(w_kvb_l):
    hw = MLA_HEADS * HEAD_PAD
    w = w_kvb_l.reshape(MLA_KV_RANK, MLA_HEADS, MLA_NOPE_DIM + MLA_V_DIM)
    top_k = jnp.pad(w[:, :, :MLA_NOPE_DIM], ((0, 0), (0, 0), (0, HEAD_PAD - MLA_NOPE_DIM)))
    top_v = jnp.pad(w[:, :, MLA_NOPE_DIM:], ((0, 0), (0, 0), (0, HEAD_PAD - MLA_V_DIM)))
    bot_k = np.zeros((LANES, MLA_HEADS, HEAD_PAD), np.float32)
    bot_v = np.zeros((LANES, MLA_HEADS, HEAD_PAD), np.float32)
    for r in range(MLA_ROPE_DIM):
        bot_k[r, :, MLA_NOPE_DIM + r] = 1.0
    bot_v[_S_ONE, :, MLA_V_DIM] = 1.0
    wk = jnp.concatenate([top_k.reshape(MLA_KV_RANK, hw), jnp.asarray(bot_k).reshape(LANES, hw)], 0)
    wv = jnp.concatenate([top_v.reshape(MLA_KV_RANK, hw), jnp.asarray(bot_v).reshape(LANES, hw)], 0)
    return jnp.concatenate([wk, wv], axis=1).astype(BF16)


def _rope_tables(positions):
    half = MLA_ROPE_DIM // 2
    inv_freq = ROPE_THETA ** (-jnp.arange(half, dtype=F32) / half)
    ang = positions.astype(F32)[:, None] * inv_freq
    cos, sin = jnp.cos(ang), jnp.sin(ang)
    ct = jnp.concatenate([cos, cos], -1)
    st = jnp.concatenate([-sin, sin], -1)
    S = positions.shape[0]
    z = lambda w: jnp.zeros((S, w), F32)
    scale = MLA_QK_DIM ** -0.5 * float(np.log2(np.e))
    tabk = jnp.concatenate([ct, st, z(LANES - 2 * MLA_ROPE_DIM)], -1)
    cqt = jnp.concatenate([jnp.full((S, MLA_NOPE_DIM), scale, F32), ct * scale,
                           z(HEAD_PAD - MLA_QK_DIM)], -1)
    sqt = jnp.concatenate([z(MLA_NOPE_DIM), st * scale, z(HEAD_PAD - MLA_QK_DIM)], -1)
    return tabk, cqt, sqt


def _pick(S, pref):
    t = pref
    while S % t:
        t //= 2
    return t


def kernel(x, positions, w_in, conv_w, conv_b, gate_b, q_a_g, kv_a_g, w_qb, w_kvb, out_g, w_out, ln1_g, ln1_b, w_rg, w_re, w_gate, w_up, w_down, ln2_g, ln2_b):
    B, S, D = x.shape
    assert B == 1 and D == D_MODEL
    depth = w_in.shape[0]
    alpha = float((2 * depth) ** 0.25)
    tm = _pick(S, 512)
    bq = _pick(S, 512)
    L = _pick(S, 256)
    tt = _pick(S, 256)
    assert tm % SUBLANES == 0 and bq % LANES == 0 and L % LANES == 0

    tabk, cqt, sqt = _rope_tables(positions[0])
    win_all = _in_weight(w_in)
    wq_all = _q_weight(w_qb)
    gb_tab = jnp.zeros((depth, 1, LANES), F32).at[:, 0, _S_I0:_S_I0 + 2 * MLSTM_HEADS].set(gate_b)
    wr_all = jnp.concatenate([w_re, w_rg, jnp.zeros((depth, D_MODEL, LANES - N_GROUPS - N_EXPERTS), F32)], -1)
    wo_all = w_out.astype(BF16)
    n_att = MLA_HEADS * MLA_V_DIM

    xs = x[0]
    for l in range(depth):
        q, k, v, mq, mk, mv, mo, gates = _proj(
            xs, win_all[l], wq_all[l], _kv_weight(w_kvb[l]), q_a_g[l][None], kv_a_g[l][None],
            conv_w[l], conv_b[l][None], gb_tab[l], tabk, cqt, sqt, tm=tm)
        a = _attention(q, k, v, out_g[l][None, :n_att], bq=bq)
        gcol = gates[:, _S_I0:_S_I0 + 2 * MLSTM_HEADS]
        hm = _mlstm(mq, mk, mv, mo, gcol, gcol.T, out_g[l][None, n_att:], L=L)
        x1t, meta, chosen = _outproj(xs, a, hm, wo_all[l], ln1_g[l][None], ln1_b[l][None], wr_all[l],
                                     tm=tm, alpha=alpha)
        rank, cnt = _rank(chosen, meta, tr=tm)
        dest, counts, pad_starts, bexp, nvb, n_slots, rw = _route(meta, rank, cnt, S)
        xsort = _dispatch(x1t, dest, counts, pad_starts, n_slots, tt=tt)
        ysort = _moe(xsort, bexp, nvb, w_gate, w_up, w_down, l)
        xs = _combine(x1t, ysort, dest, rw, ln2_g[l][None], ln2_b[l][None], tt=tt, alpha=alpha)
    return xs[None]
```

```python
import functools

import numpy as np
import jax
import jax.numpy as jnp
from jax import lax
from jax.experimental import pallas as pl
from jax.experimental.pallas import tpu as pltpu

F32 = jnp.float32
BF16 = jnp.bfloat16

D_MODEL = 1024
MLA_HEADS = 8
MLA_V_DIM = 64
MLA_NOPE_DIM = 64
MLA_ROPE_DIM = 32
MLA_QK_DIM = MLA_NOPE_DIM + MLA_ROPE_DIM
MLA_Q_RANK = 256
MLA_KV_RANK = 128
ROPE_THETA = 10000.0
MLSTM_HEADS = 4
MLSTM_HEAD_DIM = 128
D_MLSTM = MLSTM_HEADS * MLSTM_HEAD_DIM
CONV_WIDTH = 4
N_GROUPS = 8
EXPERTS_PER_GROUP = 8
N_EXPERTS = N_GROUPS * EXPERTS_PER_GROUP
D_EXPERT = 256
TOP_K = 2
EXPERT_BLOCK = 128
LN_EPS = 1e-5
RMS_EPS = 1e-6

LANES = 128
SUBLANES = 8
HEAD_PAD = 128
VMEM_LIMIT = 56 * 1024 * 1024

_C_Q0 = 0
_C_KV0 = MLA_Q_RANK
_C_MQK0 = _C_KV0 + MLA_KV_RANK
_C_MV0 = _C_MQK0 + 2 * D_MLSTM
_C_MO0 = _C_MV0 + D_MLSTM
_C_SMALL0 = _C_MO0 + D_MLSTM
D_IN_PAD = _C_SMALL0 + LANES
_S_I0 = 2 * MLA_ROPE_DIM
_S_F0 = _S_I0 + MLSTM_HEADS
_S_ONE = MLA_ROPE_DIM


def _cparams(sem, vmem=VMEM_LIMIT):
    return pltpu.CompilerParams(dimension_semantics=sem, vmem_limit_bytes=vmem)


def _proj_kernel(x_ref, win_ref, wq_ref, wkv_ref, qg_ref, kvg_ref, cw_ref, cb_ref, gb_ref,
                 tabk_ref, cq_ref, sq_ref,
                 q_out, k_out, v_out, mq_out, mk_out, mv_out, mo_out, g_out, carry_ref, *, tm):
    i = pl.program_id(0)

    @pl.when(i == 0)
    def _():
        carry_ref[...] = jnp.zeros_like(carry_ref)

    xb = x_ref[...].astype(BF16)
    p = jnp.dot(xb, win_ref[...], preferred_element_type=F32)

    def rms(v, g):
        return v * lax.rsqrt(jnp.mean(v * v, axis=-1, keepdims=True) + RMS_EPS) * g

    cqn = rms(p[:, _C_Q0:_C_KV0], qg_ref[...]).astype(BF16)
    ckvn = rms(p[:, _C_KV0:_C_MQK0], kvg_ref[...]).astype(BF16)

    qq = jnp.dot(cqn, wq_ref[...], preferred_element_type=F32)
    cq = cq_ref[...]
    sq = sq_ref[...]
    hw = MLA_HEADS * HEAD_PAD
    for h in range(MLA_HEADS):
        a = qq[:, h * HEAD_PAD:(h + 1) * HEAD_PAD]
        b = qq[:, hw + h * HEAD_PAD: hw + (h + 1) * HEAD_PAD]
        q_out[:, h * HEAD_PAD:(h + 1) * HEAD_PAD] = (a * cq + b * sq).astype(BF16)

    small = p[:, _C_SMALL0:D_IN_PAD]
    lane = lax.broadcasted_iota(jnp.int32, small.shape, 1)
    prod = small * tabk_ref[...]
    kr = prod + pltpu.roll(prod, LANES - MLA_ROPE_DIM, 1)
    kvs = jnp.where(lane < MLA_ROPE_DIM, kr, jnp.where(lane == _S_ONE, 1.0, 0.0))
    kvin = jnp.concatenate([ckvn, kvs.astype(BF16)], axis=1)
    kv = jnp.dot(kvin, wkv_ref[...], preferred_element_type=F32)
    k_out[...] = kv[:, :hw].astype(BF16)
    v_out[...] = kv[:, hw:].T.astype(BF16)

    g = small + gb_ref[...]
    lsig = jnp.minimum(g, 0.0) - jnp.log1p(jnp.exp(-jnp.abs(g)))
    is_f = (lane >= _S_F0) & (lane < _S_F0 + MLSTM_HEADS)
    g_out[...] = jnp.where(is_f, lsig, g)

    mqk = p[:, _C_MQK0:_C_MV0]
    ext = jnp.concatenate([carry_ref[...], mqk], axis=0)
    carry_ref[...] = mqk[tm - SUBLANES:tm, :]
    acc = cb_ref[...] + mqk * cw_ref[CONV_WIDTH - 1:CONV_WIDTH, :]
    for j in range(CONV_WIDTH - 1):
        off = SUBLANES - (CONV_WIDTH - 1) + j
        acc = acc + ext[off:off + tm, :] * cw_ref[j:j + 1, :]
    y = acc * (1.0 / (1.0 + jnp.exp(-acc)))
    mq_out[...] = (y[:, :D_MLSTM] * (MLSTM_HEAD_DIM ** -0.5)).astype(BF16)
    mk_out[...] = y[:, D_MLSTM:].astype(BF16)
    mv_out[...] = p[:, _C_MV0:_C_MO0].astype(BF16)
    mo = p[:, _C_MO0:_C_SMALL0]
    mo_out[...] = (1.0 / (1.0 + jnp.exp(-mo))).astype(BF16)


def _proj(x, win, wq, wkv, qg, kvg, cw, cb, gb, tabk, cqt, sqt, *, tm):
    S = x.shape[0]
    hw = MLA_HEADS * HEAD_PAD
    row = lambda w: pl.BlockSpec((tm, w), lambda i: (i, 0))
    full = lambda a: pl.BlockSpec(a.shape, lambda i: (0,) * a.ndim)
    out_shapes = (
        jax.ShapeDtypeStruct((S, hw), BF16), jax.ShapeDtypeStruct((S, hw), BF16),
        jax.ShapeDtypeStruct((hw, S), BF16),
        jax.ShapeDtypeStruct((S, D_MLSTM), BF16), jax.ShapeDtypeStruct((S, D_MLSTM), BF16),
        jax.ShapeDtypeStruct((S, D_MLSTM), BF16), jax.ShapeDtypeStruct((S, D_MLSTM), BF16),
        jax.ShapeDtypeStruct((S, LANES), F32))
    return pl.pallas_call(
        functools.partial(_proj_kernel, tm=tm),
        grid=(S // tm,),
        in_specs=[row(D_MODEL), full(win), full(wq), full(wkv), full(qg), full(kvg), full(cw),
                  full(cb), full(gb), row(LANES), row(LANES), row(LANES)],
        out_specs=(row(hw), row(hw), pl.BlockSpec((hw, tm), lambda i: (0, i)), row(D_MLSTM), row(D_MLSTM), row(D_MLSTM),
                   row(D_MLSTM), row(LANES)),
        out_shape=out_shapes,
        scratch_shapes=[pltpu.VMEM((SUBLANES, 2 * D_MLSTM), F32)],
        compiler_params=_cparams(("arbitrary",)),
        name="proj",
    )(x, win, wq, wkv, qg, kvg, cw, cb, gb, tabk, cqt, sqt)


VT_ROWS = 80


def _attn_kernel(q_ref, k_ref, vt_ref, og_ref, o_ref, s_ref, *, bq, heads_per_step):
    qi = pl.program_id(1)
    nh = heads_per_step
    cols = [(h * HEAD_PAD, (h + 1) * HEAD_PAD) for h in range(nh)]
    qs = [q_ref[:, c0:c1] for c0, c1 in cols]

    def scores(j, h):
        kt = k_ref[pl.ds(pl.multiple_of(j * bq, bq), bq), cols[h][0]:cols[h][1]]
        return lax.dot_general(kt, qs[h], (((1,), (1,)), ((), ())), preferred_element_type=F32)

    def accumulate(j, h, s, m, acc):
        vt = vt_ref[h * HEAD_PAD:h * HEAD_PAD + VT_ROWS, pl.ds(pl.multiple_of(j * bq, bq), bq)]
        m_new = jnp.maximum(m, jnp.max(s, axis=0, keepdims=True))
        alpha = jnp.exp2(m - m_new)
        p = jnp.exp2(s - m_new)
        return m_new, alpha * acc + jnp.dot(vt, p.astype(BF16), preferred_element_type=F32)

    def step(j, carry, cur, diagonal=False, prefetch=True):
        new = []
        for h in range(nh):
            m, acc = carry[2 * h:2 * h + 2]
            if prefetch:
                s_ref[(1 - cur) * nh + h] = scores(j + 1, h)
            s = s_ref[cur * nh + h]
            if diagonal:
                key = lax.broadcasted_iota(jnp.int32, (bq, bq), 0)
                qry = lax.broadcasted_iota(jnp.int32, (bq, bq), 1)
                s = jnp.where(key <= qry, s, -jnp.inf)
            new += list(accumulate(j, h, s, m, acc))
        return tuple(new)

    def pair(i, carry):
        return step(2 * i + 1, step(2 * i, carry, 0), 1)

    init = []
    for h in range(nh):
        s_ref[h] = scores(0, h)
        init += [jnp.full((1, bq), -jnp.inf, F32), jnp.zeros((VT_ROWS, bq), F32)]
    n_pairs = lax.shift_right_logical(qi, 1)
    n_quads = lax.shift_right_logical(qi, 2)
    carry = lax.fori_loop(0, n_quads, lambda i, cr: pair(2 * i + 1, pair(2 * i, cr)), tuple(init))
    carry = lax.fori_loop(2 * n_quads, n_pairs, pair, carry)
    carry = lax.cond(
        qi % 2 == 1,
        lambda cr: step(qi, step(qi - 1, cr, 0), 1, diagonal=True, prefetch=False),
        lambda cr: step(qi, cr, 0, diagonal=True, prefetch=False),
        carry)
    outs = []
    for h in range(nh):
        acc = carry[2 * h + 1]
        o = acc[:MLA_V_DIM, :] / acc[MLA_V_DIM:MLA_V_DIM + 1, :]
        o = o * lax.rsqrt(jnp.mean(o * o, axis=0, keepdims=True) + RMS_EPS)
        outs.append(o.T)
    o_ref[...] = (jnp.concatenate(outs, axis=1) * og_ref[...]).astype(BF16)


def _attention(q, k, vt, og, *, bq, heads_per_step=2):
    S = q.shape[0]
    n_hp = MLA_HEADS // heads_per_step
    wq = heads_per_step * HEAD_PAD
    wo = heads_per_step * MLA_V_DIM
    return pl.pallas_call(
        functools.partial(_attn_kernel, bq=bq, heads_per_step=heads_per_step),
        grid=(n_hp, S // bq),
        in_specs=[pl.BlockSpec((bq, wq), lambda h, i: (i, h)),
                  pl.BlockSpec((S, wq), lambda h, i: (0, h)),
                  pl.BlockSpec((wq, S), lambda h, i: (h, 0)),
                  pl.BlockSpec((1, wo), lambda h, i: (0, h))],
        out_specs=pl.BlockSpec((bq, wo), lambda h, i: (i, h)),
        out_shape=jax.ShapeDtypeStruct((S, MLA_HEADS * MLA_V_DIM), BF16),
        scratch_shapes=[pltpu.VMEM((2 * heads_per_step, bq, bq), F32)],
        compiler_params=_cparams(("arbitrary", "arbitrary")),
        name="attention",
    )(q, k, vt, og)


def _mlstm_kernel(q_ref, k_ref, v_ref, o_ref, gc_ref, gr_ref, og_ref, h_out, c_ref, m_ref, *, L):
    ci = pl.program_id(0)

    @pl.when(ci == 0)
    def _():
        c_ref[...] = jnp.zeros_like(c_ref)
        m_ref[...] = jnp.zeros_like(m_ref)

    r = lax.broadcasted_iota(jnp.int32, (L, L), 0)
    c = lax.broadcasted_iota(jnp.int32, (L, L), 1)
    tri = c <= r
    lane = lax.broadcasted_iota(jnp.int32, (L, LANES), 1)
    ones_blk = jnp.where(lane == 0, 1.0, 0.0).astype(BF16)
    gc = gc_ref[...]
    gr = gr_ref[...]
    d = MLSTM_HEAD_DIM
    for h in range(MLSTM_HEADS):
        q = q_ref[:, h * d:(h + 1) * d]
        k = k_ref[:, h * d:(h + 1) * d]
        v = v_ref[:, h * d:(h + 1) * d]
        li_col = gc[:, h:h + 1]
        lf_col = gc[:, MLSTM_HEADS + h:MLSTM_HEADS + h + 1]
        li_row = gr[h:h + 1, :]
        lf_row = gr[MLSTM_HEADS + h:MLSTM_HEADS + h + 1, :]
        b_col = jnp.sum(jnp.where(tri, lf_row, 0.0), axis=1, keepdims=True)
        b_row = jnp.sum(jnp.where(r <= c, lf_col, 0.0), axis=0, keepdims=True)
        m_prev = m_ref[h][0:1, 0:1]
        log_d = jnp.where(tri, b_col - b_row + li_row, -jnp.inf)
        log_inter = b_col + m_prev
        m_t = jnp.maximum(log_inter, jnp.max(log_d, axis=1, keepdims=True))
        w_intra = jnp.exp(log_d - m_t)
        w_inter = jnp.exp(log_inter - m_t)
        s = lax.dot_general(q, k, (((1,), (1,)), ((), ())), preferred_element_type=F32) * w_intra
        vaug = jnp.concatenate([v, ones_blk], axis=1)
        r1 = jnp.dot(s.astype(BF16), vaug, preferred_element_type=F32)
        cst = c_ref[h]
        r2 = jnp.dot(q, cst.astype(BF16), preferred_element_type=F32)
        num = r1[:, :d] + w_inter * r2[:, :d]
        den = r1[:, d:d + 1] + w_inter * r2[:, d:d + 1]
        hv = num / jnp.maximum(jnp.abs(den), jnp.exp(-m_t))
        b_end = b_col[L - 1:L, :]
        log_w = b_end - b_col + li_col
        m_new = jnp.maximum(b_end + m_prev, jnp.max(log_w, axis=0, keepdims=True))
        w_s = jnp.exp(log_w - m_new)
        decay = jnp.exp(b_end + m_prev - m_new)
        kw = (k.astype(F32) * w_s).astype(BF16)
        upd = lax.dot_general(kw, vaug, (((0,), (0,)), ((), ())), preferred_element_type=F32)
        c_ref[h] = decay * cst + upd
        m_ref[h] = jnp.broadcast_to(m_new, (SUBLANES, LANES))
        hn = hv * lax.rsqrt(jnp.mean(hv * hv, axis=-1, keepdims=True) + RMS_EPS)
        gate = o_ref[:, h * d:(h + 1) * d].astype(F32)
        h_out[:, h * d:(h + 1) * d] = (hn * gate * og_ref[:, h * d:(h + 1) * d]).astype(BF16)


def _mlstm(mq, mk, mv, mo, gcol, grow, og, *, L):
    S = mq.shape[0]
    d = MLSTM_HEAD_DIM
    row = pl.BlockSpec((L, D_MLSTM), lambda i: (i, 0))
    return pl.pallas_call(
        functools.partial(_mlstm_kernel, L=L),
        grid=(S // L,),
        in_specs=[row, row, row, row,
                  pl.BlockSpec((L, 2 * MLSTM_HEADS), lambda i: (i, 0)),
                  pl.BlockSpec((2 * MLSTM_HEADS, L), lambda i: (0, i)),
                  pl.BlockSpec((1, D_MLSTM), lambda i: (0, 0))],
        out_specs=row,
        out_shape=jax.ShapeDtypeStruct((S, D_MLSTM), BF16),
        scratch_shapes=[pltpu.VMEM((MLSTM_HEADS, d, 2 * d), F32),
                        pltpu.VMEM((MLSTM_HEADS, SUBLANES, LANES), F32)],
        compiler_params=_cparams(("arbitrary",)),
        name="mlstm",
    )(mq, mk, mv, mo, gcol, grow, og)


def _layer_norm(z, g, b):
    mu = jnp.mean(z, axis=-1, keepdims=True)
    zc = z - mu
    var = jnp.mean(zc * zc, axis=-1, keepdims=True)
    return zc * lax.rsqrt(var + LN_EPS) * g + b


_R_GROUP0 = N_EXPERTS
_META_W, _META_E = 0, TOP_K


def _first_lane(mask, lane):
    return jnp.min(jnp.where(mask, lane, LANES), axis=-1, keepdims=True)


def _route_tokens(lg):
    lane = lax.broadcasted_iota(jnp.int32, lg.shape, 1)
    is_g = (lane >= _R_GROUP0) & (lane < _R_GROUP0 + N_GROUPS)
    gl = jnp.where(is_g, lg, -jnp.inf)
    ge = jnp.exp(gl - jnp.max(gl, axis=-1, keepdims=True))
    gp = ge / jnp.sum(ge, axis=-1, keepdims=True)
    g_p = jnp.max(gp, axis=-1, keepdims=True)
    g_top = _first_lane((gp == g_p) & is_g, lane) - _R_GROUP0
    in_grp = (lane < N_EXPERTS) & (lax.shift_right_logical(lane, 3) == g_top)
    el = jnp.where(in_grp, lg, -jnp.inf)
    ee = jnp.exp(el - jnp.max(el, axis=-1, keepdims=True))
    p = ee / jnp.sum(ee, axis=-1, keepdims=True)
    p1 = jnp.max(p, axis=-1, keepdims=True)
    i1 = _first_lane((p == p1) & in_grp, lane)
    pm = jnp.where(in_grp & (lane != i1), p, -jnp.inf)
    p2 = jnp.max(pm, axis=-1, keepdims=True)
    i2 = _first_lane(pm == p2, lane)
    den = p1 + p2
    w1 = g_p * (p1 / den)
    w2 = g_p * (p2 / den)
    meta = jnp.where(lane == _META_W, w1, jnp.where(lane == _META_W + 1, w2, 0.0))
    meta = jnp.where(lane == _META_E, i1.astype(F32), jnp.where(lane == _META_E + 1, i2.astype(F32), meta))
    chosen = ((lane == i1) | (lane == i2)).astype(BF16)
    return meta, chosen


def _outproj_kernel(x_ref, a_ref, h_ref, wo_ref, g_ref, b_ref, wr_ref, x1t_out, meta_out, oh_out, *,
                    alpha, tm):
    mix = jnp.concatenate([a_ref[...], h_ref[...]], axis=1)
    y = jnp.dot(mix, wo_ref[...], preferred_element_type=F32)
    x1 = _layer_norm(alpha * x_ref[...] + y, g_ref[...], b_ref[...])
    for c in range(SUBLANES):
        x1t_out[pl.ds(c, tm, stride=SUBLANES), :] = x1[:, c * LANES:(c + 1) * LANES]
    lg = jnp.dot(x1, wr_ref[...], preferred_element_type=F32, precision=lax.Precision.HIGHEST)
    meta_out[...], oh_out[...] = _route_tokens(lg)


def _outproj(x, a, hm, wo, g, b, wr, *, tm, alpha):
    S = x.shape[0]
    row = lambda w: pl.BlockSpec((tm, w), lambda i: (i, 0))
    full = lambda arr: pl.BlockSpec(arr.shape, lambda i: (0,) * arr.ndim)
    return pl.pallas_call(
        functools.partial(_outproj_kernel, alpha=alpha, tm=tm),
        grid=(S // tm,),
        in_specs=[row(D_MODEL), row(MLA_HEADS * MLA_V_DIM), row(D_MLSTM), full(wo), full(g), full(b),
                  full(wr)],
        out_specs=(pl.BlockSpec((tm * SUBLANES, LANES), lambda i: (i, 0)), row(LANES), row(LANES)),
        out_shape=(jax.ShapeDtypeStruct((S * SUBLANES, LANES), F32),
                   jax.ShapeDtypeStruct((S, LANES), F32),
                   jax.ShapeDtypeStruct((S, LANES), BF16)),
        compiler_params=_cparams(("arbitrary",)),
        name="outproj",
    )(x, a, hm, wo, g, b, wr)


def _rank_kernel(oh_ref, meta_ref, rank_out, cnt_out, base_ref, *, tr):
    i = pl.program_id(0)

    @pl.when(i == 0)
    def _():
        base_ref[...] = jnp.zeros_like(base_ref)

    oh = oh_ref[...]
    r = lax.broadcasted_iota(jnp.int32, (tr, tr), 0)
    c = lax.broadcasted_iota(jnp.int32, (tr, tr), 1)
    before = (c < r).astype(BF16)
    seen = base_ref[0:1, :] + jnp.dot(before, oh, preferred_element_type=F32)
    lane = lax.broadcasted_iota(jnp.int32, (tr, LANES), 1)
    meta = meta_ref[...]
    out = jnp.zeros((tr, LANES), F32)
    for k in range(TOP_K):
        e_k = meta[:, _META_E + k:_META_E + k + 1].astype(jnp.int32)
        rk = jnp.sum(jnp.where(lane == e_k, seen, 0.0), axis=-1, keepdims=True)
        out = jnp.where(lane == k, rk, out)
    rank_out[...] = out
    total = base_ref[0:1, :] + jnp.sum(oh.astype(F32), axis=0, keepdims=True)
    base_ref[...] = jnp.broadcast_to(total, base_ref.shape)
    cnt_out[...] = jnp.broadcast_to(total, cnt_out.shape)


def _rank(oh, meta, *, tr):
    S = oh.shape[0]
    row = pl.BlockSpec((tr, LANES), lambda i: (i, 0))
    return pl.pallas_call(
        functools.partial(_rank_kernel, tr=tr),
        grid=(S // tr,),
        in_specs=[row, row],
        out_specs=(row, pl.BlockSpec((SUBLANES, LANES), lambda i: (0, 0))),
        out_shape=(jax.ShapeDtypeStruct((S, LANES), F32), jax.ShapeDtypeStruct((SUBLANES, LANES), F32)),
        scratch_shapes=[pltpu.VMEM((SUBLANES, LANES), F32)],
        compiler_params=_cparams(("arbitrary",)),
        name="rank",
    )(oh, meta)


TOKEN_TILE = SUBLANES
DMA_GROUP = 8


def _rows_from_tiles(ref, n):
    return jnp.concatenate([ref[pl.ds(c, n, stride=TOKEN_TILE), :] for c in range(TOKEN_TILE)], axis=1)


def _rows_to_tiles(ref, v):
    for c in range(TOKEN_TILE):
        ref[pl.ds(c, v.shape[0], stride=TOKEN_TILE), :] = v[:, c * LANES:(c + 1) * LANES]


def _pow2_chunks(n, top, fn):
    off = 0
    bit = top
    while bit:
        @pl.when((n & bit) != 0)
        def _(bit=bit, off=off):
            fn(off, bit)
        off = off + (n & bit)
        bit //= 2


def _dispatch_kernel(cnt_ref, pst_ref, d_ref, x_ref, xs_hbm, xbuf, zbuf, sem, zsem, *, tt):
    i = pl.program_id(0)
    n = pl.num_programs(0)
    G = TOKEN_TILE
    slot = lax.rem(i, 2)

    def row_copy(src_row, dst_row, rows, sl):
        return pltpu.make_async_copy(xbuf.at[sl, pl.ds(src_row * G, rows * G)],
                                     xs_hbm.at[pl.ds(dst_row * G, rows * G)], sem.at[sl])

    def zero_copy(dst_row, rows):
        return pltpu.make_async_copy(zbuf.at[pl.ds(0, rows * G)], xs_hbm.at[pl.ds(dst_row * G, rows * G)], zsem)

    @pl.when(i == 0)
    def _():
        zbuf[...] = jnp.zeros_like(zbuf)

        def fill(e, wait):
            pad = (-cnt_ref[e]) & (EXPERT_BLOCK - 1)
            first = pst_ref[e] + cnt_ref[e]
            if wait:
                _pow2_chunks(pad, EXPERT_BLOCK // 2, lambda off, size: zero_copy(0, size).wait())
            else:
                _pow2_chunks(pad, EXPERT_BLOCK // 2, lambda off, size: zero_copy(first + off, size).start())
            return 0
        lax.fori_loop(0, N_EXPERTS, lambda e, c: fill(e, False), 0)
        lax.fori_loop(0, N_EXPERTS, lambda e, c: fill(e, True), 0)
        half = EXPERT_BLOCK // 2
        used = (pst_ref[N_EXPERTS - 1] + cnt_ref[N_EXPERTS - 1] + EXPERT_BLOCK - 1) // EXPERT_BLOCK * 2
        n_half = xs_hbm.shape[0] // (half * G)

        def tail(c, wait):
            if wait:
                zero_copy(0, half).wait()
            else:
                zero_copy(c * half, half).start()
            return 0
        lax.fori_loop(used, n_half, lambda c, _: tail(c, False), 0)
        lax.fori_loop(used, n_half, lambda c, _: tail(c, True), 0)

    def wait_step(sl):
        for _ in range(TOP_K):
            row_copy(0, 0, tt, sl).wait()

    for sl in range(2):
        @pl.when(slot == sl)
        def _(sl=sl):
            @pl.when(i >= 2)
            def _():
                wait_step(sl)
            xbuf[sl] = x_ref[...]
            for t0 in range(0, tt, DMA_GROUP):
                idx = [d_ref[0, 0, j] for j in range(TOP_K * t0, TOP_K * (t0 + DMA_GROUP))]
                for j, dst in enumerate(idx):
                    row_copy(t0 + j // TOP_K, dst, 1, sl).start(priority=j % 2)

    @pl.when(i == n - 1)
    def _():
        @pl.when(n >= 2)
        def _():
            wait_step(1 - slot)
        wait_step(slot)


def _dispatch(x1t, dest, counts, pad_starts, n_slots, *, tt):
    G = TOKEN_TILE
    T = x1t.shape[0] // G
    nt = T // tt
    grid_spec = pltpu.PrefetchScalarGridSpec(
        num_scalar_prefetch=2,
        grid=(nt,),
        in_specs=[pl.BlockSpec((1, 1, TOP_K * tt), lambda i, c, p: (i, 0, 0), memory_space=pltpu.SMEM),
                  pl.BlockSpec((tt * G, LANES), lambda i, c, p: (i, 0))],
        out_specs=pl.BlockSpec(memory_space=pl.ANY),
        scratch_shapes=[pltpu.VMEM((2, tt * G, LANES), F32),
                        pltpu.VMEM((EXPERT_BLOCK // 2 * G, LANES), F32),
                        pltpu.SemaphoreType.DMA((2,)), pltpu.SemaphoreType.DMA])
    return pl.pallas_call(
        functools.partial(_dispatch_kernel, tt=tt),
        grid_spec=grid_spec,
        out_shape=jax.ShapeDtypeStruct((n_slots * G, LANES), F32),
        compiler_params=_cparams(("arbitrary",)),
        name="dispatch",
    )(counts, pad_starts, dest.reshape(nt, 1, TOP_K * tt), x1t)


X_RING = 3


def _moe_kernel(bexp_ref, nvb_ref, xs_hbm, wg_ref, wu_ref, wd_ref, ys_ref, xbuf, wgb, wub, wdb, xsem):
    b = pl.program_id(0)
    nvb = nvb_ref[0]
    rows = EXPERT_BLOCK * TOKEN_TILE

    def block_copy(j, sl):
        return pltpu.make_async_copy(xs_hbm.at[pl.ds(j * rows, rows)], xbuf.at[sl], xsem.at[sl])

    @pl.when(b == 0)
    def _():
        for j in range(X_RING - 1):
            @pl.when(j < nvb)
            def _(j=j):
                block_copy(j, j).start()

    ahead = b + (X_RING - 1)

    @pl.when(ahead < nvb)
    def _():
        block_copy(ahead, lax.rem(ahead, X_RING)).start()

    @pl.when((b == 0) | (bexp_ref[b] != bexp_ref[jnp.maximum(b - 1, 0)]))
    def _():
        wgb[...] = wg_ref[0, 0].astype(BF16)
        wub[...] = wu_ref[0, 0].astype(BF16)
        wdb[...] = wd_ref[0, 0].astype(BF16)

    @pl.when(b < nvb)
    def _():
        sl = lax.rem(b, X_RING)
        block_copy(b, sl).wait()
        xs = _rows_from_tiles(xbuf.at[sl], EXPERT_BLOCK).astype(BF16)
        g = jnp.dot(xs, wgb[...], preferred_element_type=F32)
        u = jnp.dot(xs, wub[...], preferred_element_type=F32)
        hid = (g * (1.0 / (1.0 + jnp.exp(-g)))) * u
        _rows_to_tiles(ys_ref, jnp.dot(hid.astype(BF16), wdb[...], preferred_element_type=F32))

    @pl.when(b >= nvb_ref[0])
    def _():
        ys_ref[...] = jnp.zeros_like(ys_ref)


def _moe(xs, block_expert, n_valid_blocks, wg, wu, wd, layer):
    G = TOKEN_TILE
    assert G * LANES == D_MODEL
    R = EXPERT_BLOCK
    nb = xs.shape[0] // (R * G)
    blk = lambda b, be, nvb: (b, 0)
    wspec = lambda shp: pl.BlockSpec((1, 1) + shp, lambda b, be, nvb: (layer, be[b], 0, 0))
    grid_spec = pltpu.PrefetchScalarGridSpec(
        num_scalar_prefetch=2,
        grid=(nb,),
        in_specs=[pl.BlockSpec(memory_space=pl.ANY),
                  wspec((D_MODEL, D_EXPERT)), wspec((D_MODEL, D_EXPERT)), wspec((D_EXPERT, D_MODEL))],
        out_specs=pl.BlockSpec((R * G, LANES), blk),
        scratch_shapes=[pltpu.VMEM((X_RING, R * G, LANES), F32),
                        pltpu.VMEM((D_MODEL, D_EXPERT), BF16), pltpu.VMEM((D_MODEL, D_EXPERT), BF16),
                        pltpu.VMEM((D_EXPERT, D_MODEL), BF16), pltpu.SemaphoreType.DMA((X_RING,))])
    return pl.pallas_call(
        _moe_kernel,
        grid_spec=grid_spec,
        out_shape=jax.ShapeDtypeStruct(xs.shape, F32),
        compiler_params=_cparams(("arbitrary",)),
        name="experts",
    )(block_expert, n_valid_blocks, xs, wg, wu, wd)


def _combine_kernel(d_ref, dn_ref, x_ref, w_ref, g_ref, b_ref, ys_hbm, o_ref, ybuf, sem, *, alpha, tt):
    i = pl.program_id(0)
    n = pl.num_programs(0)
    G = TOKEN_TILE
    slot = lax.rem(i, 2)

    def row_copy(src_row, r, rows, sl):
        return pltpu.make_async_copy(ys_hbm.at[pl.ds(src_row * G, rows * G)],
                                     ybuf.at[sl, pl.ds(r * G, rows * G)], sem.at[sl])

    def fetch(idx_ref, sl):
        for t0 in range(0, tt, DMA_GROUP):
            idx = [idx_ref[0, 0, j] for j in range(TOP_K * t0, TOP_K * (t0 + DMA_GROUP))]
            for j, src in enumerate(idx):
                row_copy(src, (j % TOP_K) * tt + t0 + j // TOP_K, 1, sl).start(priority=j % 2)

    @pl.when(i == 0)
    def _():
        fetch(d_ref, 0)

    for sl in range(2):
        @pl.when(slot == sl)
        def _(sl=sl):
            row_copy(0, 0, TOP_K * tt, sl).wait()
            fetch(dn_ref, 1 - sl)
            w = w_ref[...]
            y = (_rows_from_tiles(ybuf.at[sl, pl.ds(0, tt * G)], tt) * w[:, 0:1]
                 + _rows_from_tiles(ybuf.at[sl, pl.ds(tt * G, tt * G)], tt) * w[:, 1:2])
            o_ref[...] = _layer_norm(alpha * _rows_from_tiles(x_ref, tt) + y, g_ref[...], b_ref[...])

    @pl.when(i == n - 1)
    def _():
        row_copy(0, 0, TOP_K * tt, 1 - slot).wait()


def _combine(x1t, ys, dest, w, g, b, *, tt, alpha):
    G = TOKEN_TILE
    S = x1t.shape[0] // G
    nt = S // tt
    d3 = dest.reshape(nt, 1, TOP_K * tt)
    smem_blk = lambda f: pl.BlockSpec((1, 1, TOP_K * tt), f, memory_space=pltpu.SMEM)
    return pl.pallas_call(
        functools.partial(_combine_kernel, alpha=alpha, tt=tt),
        grid=(nt,),
        in_specs=[smem_blk(lambda i: (i, 0, 0)),
                  smem_blk(lambda i: (jnp.minimum(i + 1, nt - 1), 0, 0)),
                  pl.BlockSpec((tt * G, LANES), lambda i: (i, 0)),
                  pl.BlockSpec((tt, TOP_K), lambda i: (i, 0)),
                  pl.BlockSpec((1, D_MODEL), lambda i: (0, 0)),
                  pl.BlockSpec((1, D_MODEL), lambda i: (0, 0)),
                  pl.BlockSpec(memory_space=pl.ANY)],
        out_specs=pl.BlockSpec((tt, D_MODEL), lambda i: (i, 0)),
        out_shape=jax.ShapeDtypeStruct((S, D_MODEL), F32),
        scratch_shapes=[pltpu.VMEM((2, TOP_K * tt * G, LANES), F32), pltpu.SemaphoreType.DMA((2,))],
        compiler_params=_cparams(("arbitrary",)),
        name="combine",
    )(d3, d3, x1t, w, g, b, ys)


def _route(meta, rank, cnt, T):
    weights = meta[:, _META_W:_META_W + TOP_K]
    flat_e = meta[:, _META_E:_META_E + TOP_K].astype(jnp.int32).reshape(-1)
    n_assign = T * TOP_K
    counts = cnt[0, :N_EXPERTS].astype(jnp.int32)
    padded = (counts + EXPERT_BLOCK - 1) // EXPERT_BLOCK * EXPERT_BLOCK
    pad_ends = jnp.cumsum(padded)
    pad_starts = pad_ends - padded
    dest = pad_starts[flat_e] + rank[:, :TOP_K].astype(jnp.int32).reshape(-1)
    n_blocks = n_assign // EXPERT_BLOCK + N_EXPERTS
    block_start = jnp.arange(n_blocks, dtype=jnp.int32) * EXPERT_BLOCK
    block_expert = jnp.minimum(jnp.sum((pad_ends[None, :] <= block_start[:, None]).astype(jnp.int32), axis=1),
                               N_EXPERTS - 1).astype(jnp.int32)
    n_valid_blocks = (pad_ends[-1:] // EXPERT_BLOCK).astype(jnp.int32)
    return (dest, counts.astype(jnp.int32), pad_starts.astype(jnp.int32), block_expert, n_valid_blocks,
            n_blocks * EXPERT_BLOCK, weights)


def _in_weight(w_in):
    kr0 = MLA_Q_RANK + MLA_KV_RANK
    mq0 = kr0 + MLA_ROPE_DIM
    i0 = mq0 + 4 * D_MLSTM
    half = MLA_ROPE_DIM // 2
    n_small = 2 * MLA_ROPE_DIM + 2 * MLSTM_HEADS
    pad = jnp.zeros(w_in.shape[:-1] + (LANES - n_small,), w_in.dtype)
    return jnp.concatenate(
        [w_in[..., :kr0], w_in[..., mq0:i0], w_in[..., kr0:mq0], w_in[..., kr0 + half:mq0],
         w_in[..., kr0:kr0 + half], w_in[..., i0:i0 + 2 * MLSTM_HEADS], pad], axis=-1).astype(BF16)


def _q_weight(w_qb):
    lead = w_qb.shape[:-1]
    w = w_qb.reshape(lead + (MLA_HEADS, MLA_QK_DIM))
    half = MLA_ROPE_DIM // 2
    z = lambda n: jnp.zeros(lead + (MLA_HEADS, n), w_qb.dtype)
    main = jnp.concatenate([w, z(HEAD_PAD - MLA_QK_DIM)], axis=-1)
    swp = jnp.concatenate([z(MLA_NOPE_DIM), w[..., MLA_NOPE_DIM + half:], w[..., MLA_NOPE_DIM:MLA_NOPE_DIM + half],
                           z(HEAD_PAD - MLA_QK_DIM)], axis=-1)
    hw = MLA_HEADS * HEAD_PAD
    return jnp.concatenate([main.reshape(lead + (hw,)), swp.reshape(lead + (hw,))], axis=-1).astype(BF16)


def _kv_weight(w_kvb_l):
    hw = MLA_HEADS * HEAD_PAD
    w = w_kvb_l.reshape(MLA_KV_RANK, MLA_HEADS, MLA_NOPE_DIM + MLA_V_DIM)
    top_k = jnp.pad(w[:, :, :MLA_NOPE_DIM], ((0, 0), (0, 0), (0, HEAD_PAD - MLA_NOPE_DIM)))
    top_v = jnp.pad(w[:, :, MLA_NOPE_DIM:], ((0, 0), (0, 0), (0, HEAD_PAD - MLA_V_DIM)))
    bot_k = np.zeros((LANES, MLA_HEADS, HEAD_PAD), np.float32)
    bot_v = np.zeros((LANES, MLA_HEADS, HEAD_PAD), np.float32)
    for r in range(MLA_ROPE_DIM):
        bot_k[r, :, MLA_NOPE_DIM + r] = 1.0
    bot_v[_S_ONE, :, MLA_V_DIM] = 1.0
    wk = jnp.concatenate([top_k.reshape(MLA_KV_RANK, hw), jnp.asarray(bot_k).reshape(LANES, hw)], 0)
    wv = jnp.concatenate([top_v.reshape(MLA_KV_RANK, hw), jnp.asarray(bot_v).reshape(LANES, hw)], 0)
    return jnp.concatenate([wk, wv], axis=1).astype(BF16)


def _rope_tables(positions):
    half = MLA_ROPE_DIM // 2
    inv_freq = ROPE_THETA ** (-jnp.arange(half, dtype=F32) / half)
    ang = positions.astype(F32)[:, None] * inv_freq
    cos, sin = jnp.cos(ang), jnp.sin(ang)
    ct = jnp.concatenate([cos, cos], -1)
    st = jnp.concatenate([-sin, sin], -1)
    S = positions.shape[0]
    z = lambda w: jnp.zeros((S, w), F32)
    scale = MLA_QK_DIM ** -0.5 * float(np.log2(np.e))
    tabk = jnp.concatenate([ct, st, z(LANES - 2 * MLA_ROPE_DIM)], -1)
    cqt = jnp.concatenate([jnp.full((S, MLA_NOPE_DIM), scale, F32), ct * scale,
                           z(HEAD_PAD - MLA_QK_DIM)], -1)
    sqt = jnp.concatenate([z(MLA_NOPE_DIM), st * scale, z(HEAD_PAD - MLA_QK_DIM)], -1)
    return tabk, cqt, sqt


def _pick(S, pref):
    t = pref
    while S % t:
        t //= 2
    return t


def kernel(x, positions, w_in, conv_w, conv_b, gate_b, q_a_g, kv_a_g, w_qb, w_kvb, out_g, w_out, ln1_g, ln1_b, w_rg, w_re, w_gate, w_up, w_down, ln2_g, ln2_b):
    B, S, D = x.shape
    assert B == 1 and D == D_MODEL
    depth = w_in.shape[0]
    alpha = float((2 * depth) ** 0.25)
    tm = _pick(S, 512)
    bq = _pick(S, 512)
    L = _pick(S, 256)
    tt = _pick(S, 256)
    assert tm % SUBLANES == 0 and bq % LANES == 0 and L % LANES == 0

    tabk, cqt, sqt = _rope_tables(positions[0])
    win_all = _in_weight(w_in)
    wq_all = _q_weight(w_qb)
    gb_tab = jnp.zeros((depth, 1, LANES), F32).at[:, 0, _S_I0:_S_I0 + 2 * MLSTM_HEADS].set(gate_b)
    wr_all = jnp.concatenate([w_re, w_rg, jnp.zeros((depth, D_MODEL, LANES - N_GROUPS - N_EXPERTS), F32)], -1)
    wo_all = w_out.astype(BF16)
    n_att = MLA_HEADS * MLA_V_DIM

    xs = x[0]
    for l in range(depth):
        q, k, v, mq, mk, mv, mo, gates = _proj(
            xs, win_all[l], wq_all[l], _kv_weight(w_kvb[l]), q_a_g[l][None], kv_a_g[l][None],
            conv_w[l], conv_b[l][None], gb_tab[l], tabk, cqt, sqt, tm=tm)
        a = _attention(q, k, v, out_g[l][None, :n_att], bq=bq)
        gcol = gates[:, _S_I0:_S_I0 + 2 * MLSTM_HEADS]
        hm = _mlstm(mq, mk, mv, mo, gcol, gcol.T, out_g[l][None, n_att:], L=L)
        x1t, meta, chosen = _outproj(xs, a, hm, wo_all[l], ln1_g[l][None], ln1_b[l][None], wr_all[l],
                                     tm=tm, alpha=alpha)
        rank, cnt = _rank(chosen, meta, tr=tm)
        dest, counts, pad_starts, bexp, nvb, n_slots, rw = _route(meta, rank, cnt, S)
        xsort = _dispatch(x1t, dest, counts, pad_starts, n_slots, tt=tt)
        ysort = _moe(xsort, bexp, nvb, w_gate, w_up, w_down, l)
        xs = _combine(x1t, ysort, dest, rw, ln2_g[l][None], ln2_b[l][None], tt=tt, alpha=alpha)
    return xs[None]
```

```python
import functools

import numpy as np
import jax
import jax.numpy as jnp
from jax import lax
from jax.experimental import pallas as pl
from jax.experimental.pallas import tpu as pltpu

F32 = jnp.float32
BF16 = jnp.bfloat16

D_MODEL = 1024
MLA_HEADS = 8
MLA_V_DIM = 64
MLA_NOPE_DIM = 64
MLA_ROPE_DIM = 32
MLA_QK_DIM = MLA_NOPE_DIM + MLA_ROPE_DIM
MLA_Q_RANK = 256
MLA_KV_RANK = 128
ROPE_THETA = 10000.0
MLSTM_HEADS = 4
MLSTM_HEAD_DIM = 128
D_MLSTM = MLSTM_HEADS * MLSTM_HEAD_DIM
CONV_WIDTH = 4
N_GROUPS = 8
EXPERTS_PER_GROUP = 8
N_EXPERTS = N_GROUPS * EXPERTS_PER_GROUP
D_EXPERT = 256
TOP_K = 2
EXPERT_BLOCK = 128
LN_EPS = 1e-5
RMS_EPS = 1e-6

LANES = 128
SUBLANES = 8
HEAD_PAD = 128
VMEM_LIMIT = 56 * 1024 * 1024

_C_Q0 = 0
_C_KV0 = MLA_Q_RANK
_C_MQK0 = _C_KV0 + MLA_KV_RANK
_C_MV0 = _C_MQK0 + 2 * D_MLSTM
_C_MO0 = _C_MV0 + D_MLSTM
_C_SMALL0 = _C_MO0 + D_MLSTM
D_IN_PAD = _C_SMALL0 + LANES
_S_I0 = 2 * MLA_ROPE_DIM
_S_F0 = _S_I0 + MLSTM_HEADS
_S_ONE = MLA_ROPE_DIM


def _cparams(sem, vmem=VMEM_LIMIT):
    return pltpu.CompilerParams(dimension_semantics=sem, vmem_limit_bytes=vmem)


def _proj_kernel(x_ref, win_ref, wq_ref, wkv_ref, qg_ref, kvg_ref, cw_ref, cb_ref, gb_ref,
                 tabk_ref, cq_ref, sq_ref,
                 q_out, k_out, v_out, mq_out, mk_out, mv_out, mo_out, g_out, carry_ref, *, tm):
    i = pl.program_id(0)

    @pl.when(i == 0)
    def _():
        carry_ref[...] = jnp.zeros_like(carry_ref)

    xb = x_ref[...].astype(BF16)
    p = jnp.dot(xb, win_ref[...], preferred_element_type=F32)

    def rms(v, g):
        return v * lax.rsqrt(jnp.mean(v * v, axis=-1, keepdims=True) + RMS_EPS) * g

    cqn = rms(p[:, _C_Q0:_C_KV0], qg_ref[...]).astype(BF16)
    ckvn = rms(p[:, _C_KV0:_C_MQK0], kvg_ref[...]).astype(BF16)

    qq = jnp.dot(cqn, wq_ref[...], preferred_element_type=F32)
    cq = cq_ref[...]
    sq = sq_ref[...]
    hw = MLA_HEADS * HEAD_PAD
    for h in range(MLA_HEADS):
        a = qq[:, h * HEAD_PAD:(h + 1) * HEAD_PAD]
        b = qq[:, hw + h * HEAD_PAD: hw + (h + 1) * HEAD_PAD]
        q_out[:, h * HEAD_PAD:(h + 1) * HEAD_PAD] = (a * cq + b * sq).astype(BF16)

    small = p[:, _C_SMALL0:D_IN_PAD]
    lane = lax.broadcasted_iota(jnp.int32, small.shape, 1)
    prod = small * tabk_ref[...]
    kr = prod + pltpu.roll(prod, LANES - MLA_ROPE_DIM, 1)
    kvs = jnp.where(lane < MLA_ROPE_DIM, kr, jnp.where(lane == _S_ONE, 1.0, 0.0))
    kvin = jnp.concatenate([ckvn, kvs.astype(BF16)], axis=1)
    kv = jnp.dot(kvin, wkv_ref[...], preferred_element_type=F32)
    k_out[...] = kv[:, :hw].astype(BF16)
    v_out[...] = kv[:, hw:].T.astype(BF16)

    g = small + gb_ref[...]
    lsig = jnp.minimum(g, 0.0) - jnp.log1p(jnp.exp(-jnp.abs(g)))
    is_f = (lane >= _S_F0) & (lane < _S_F0 + MLSTM_HEADS)
    g_out[...] = jnp.where(is_f, lsig, g)

    mqk = p[:, _C_MQK0:_C_MV0]
    ext = jnp.concatenate([carry_ref[...], mqk], axis=0)
    carry_ref[...] = mqk[tm - SUBLANES:tm, :]
    acc = cb_ref[...] + mqk * cw_ref[CONV_WIDTH - 1:CONV_WIDTH, :]
    for j in range(CONV_WIDTH - 1):
        off = SUBLANES - (CONV_WIDTH - 1) + j
        acc = acc + ext[off:off + tm, :] * cw_ref[j:j + 1, :]
    y = acc * (1.0 / (1.0 + jnp.exp(-acc)))
    mq_out[...] = (y[:, :D_MLSTM] * (MLSTM_HEAD_DIM ** -0.5)).astype(BF16)
    mk_out[...] = y[:, D_MLSTM:].astype(BF16)
    mv_out[...] = p[:, _C_MV0:_C_MO0].astype(BF16)
    mo = p[:, _C_MO0:_C_SMALL0]
    mo_out[...] = (1.0 / (1.0 + jnp.exp(-mo))).astype(BF16)


def _proj(x, win, wq, wkv, qg, kvg, cw, cb, gb, tabk, cqt, sqt, *, tm):
    S = x.shape[0]
    hw = MLA_HEADS * HEAD_PAD
    row = lambda w: pl.BlockSpec((tm, w), lambda i: (i, 0))
    full = lambda a: pl.BlockSpec(a.shape, lambda i: (0,) * a.ndim)
    out_shapes = (
        jax.ShapeDtypeStruct((S, hw), BF16), jax.ShapeDtypeStruct((S, hw), BF16),
        jax.ShapeDtypeStruct((hw, S), BF16),
        jax.ShapeDtypeStruct((S, D_MLSTM), BF16), jax.ShapeDtypeStruct((S, D_MLSTM), BF16),
        jax.ShapeDtypeStruct((S, D_MLSTM), BF16), jax.ShapeDtypeStruct((S, D_MLSTM), BF16),
        jax.ShapeDtypeStruct((S, LANES), F32))
    return pl.pallas_call(
        functools.partial(_proj_kernel, tm=tm),
        grid=(S // tm,),
        in_specs=[row(D_MODEL), full(win), full(wq), full(wkv), full(qg), full(kvg), full(cw),
                  full(cb), full(gb), row(LANES), row(LANES), row(LANES)],
        out_specs=(row(hw), row(hw), pl.BlockSpec((hw, tm), lambda i: (0, i)), row(D_MLSTM), row(D_MLSTM), row(D_MLSTM),
                   row(D_MLSTM), row(LANES)),
        out_shape=out_shapes,
        scratch_shapes=[pltpu.VMEM((SUBLANES, 2 * D_MLSTM), F32)],
        compiler_params=_cparams(("arbitrary",)),
        name="proj",
    )(x, win, wq, wkv, qg, kvg, cw, cb, gb, tabk, cqt, sqt)


VT_ROWS = 80


def _attn_kernel(q_ref, k_ref, vt_ref, og_ref, o_ref, s_ref, *, bq, heads_per_step):
    qi = pl.program_id(1)
    nh = heads_per_step
    cols = [(h * HEAD_PAD, (h + 1) * HEAD_PAD) for h in range(nh)]
    qs = [q_ref[:, c0:c1] for c0, c1 in cols]

    def scores(j, h):
        kt = k_ref[pl.ds(pl.multiple_of(j * bq, bq), bq), cols[h][0]:cols[h][1]]
        return lax.dot_general(kt, qs[h], (((1,), (1,)), ((), ())), preferred_element_type=F32)

    def accumulate(j, h, s, m, acc):
        vt = vt_ref[h * HEAD_PAD:h * HEAD_PAD + VT_ROWS, pl.ds(pl.multiple_of(j * bq, bq), bq)]
        m_new = jnp.maximum(m, jnp.max(s, axis=0, keepdims=True))
        alpha = jnp.exp2(m - m_new)
        p = jnp.exp2(s - m_new)
        return m_new, alpha * acc + jnp.dot(vt, p.astype(BF16), preferred_element_type=F32)

    def step(j, carry, cur, diagonal=False, prefetch=True):
        new = []
        for h in range(nh):
            m, acc = carry[2 * h:2 * h + 2]
            if prefetch:
                s_ref[(1 - cur) * nh + h] = scores(j + 1, h)
            s = s_ref[cur * nh + h]
            if diagonal:
                key = lax.broadcasted_iota(jnp.int32, (bq, bq), 0)
                qry = lax.broadcasted_iota(jnp.int32, (bq, bq), 1)
                s = jnp.where(key <= qry, s, -jnp.inf)
            new += list(accumulate(j, h, s, m, acc))
        return tuple(new)

    def pair(i, carry):
        return step(2 * i + 1, step(2 * i, carry, 0), 1)

    init = []
    for h in range(nh):
        s_ref[h] = scores(0, h)
        init += [jnp.full((1, bq), -jnp.inf, F32), jnp.zeros((VT_ROWS, bq), F32)]
    n_pairs = lax.shift_right_logical(qi, 1)
    n_quads = lax.shift_right_logical(qi, 2)
    carry = lax.fori_loop(0, n_quads, lambda i, cr: pair(2 * i + 1, pair(2 * i, cr)), tuple(init))
    carry = lax.fori_loop(2 * n_quads, n_pairs, pair, carry)
    carry = lax.cond(
        qi % 2 == 1,
        lambda cr: step(qi, step(qi - 1, cr, 0), 1, diagonal=True, prefetch=False),
        lambda cr: step(qi, cr, 0, diagonal=True, prefetch=False),
        carry)
    outs = []
    for h in range(nh):
        acc = carry[2 * h + 1]
        o = acc[:MLA_V_DIM, :] / acc[MLA_V_DIM:MLA_V_DIM + 1, :]
        o = o * lax.rsqrt(jnp.mean(o * o, axis=0, keepdims=True) + RMS_EPS)
        outs.append(o.T)
    o_ref[...] = (jnp.concatenate(outs, axis=1) * og_ref[...]).astype(BF16)


def _attention(q, k, vt, og, *, bq, heads_per_step=2):
    S = q.shape[0]
    n_hp = MLA_HEADS // heads_per_step
    wq = heads_per_step * HEAD_PAD
    wo = heads_per_step * MLA_V_DIM
    return pl.pallas_call(
        functools.partial(_attn_kernel, bq=bq, heads_per_step=heads_per_step),
        grid=(n_hp, S // bq),
        in_specs=[pl.BlockSpec((bq, wq), lambda h, i: (i, h)),
                  pl.BlockSpec((S, wq), lambda h, i: (0, h)),
                  pl.BlockSpec((wq, S), lambda h, i: (h, 0)),
                  pl.BlockSpec((1, wo), lambda h, i: (0, h))],
        out_specs=pl.BlockSpec((bq, wo), lambda h, i: (i, h)),
        out_shape=jax.ShapeDtypeStruct((S, MLA_HEADS * MLA_V_DIM), BF16),
        scratch_shapes=[pltpu.VMEM((2 * heads_per_step, bq, bq), F32)],
        compiler_params=_cparams(("arbitrary", "arbitrary")),
        name="attention",
    )(q, k, vt, og)


def _mlstm_kernel(q_ref, k_ref, v_ref, o_ref, gc_ref, gr_ref, og_ref, h_out, c_ref, m_ref, *, L):
    ci = pl.program_id(0)

    @pl.when(ci == 0)
    def _():
        c_ref[...] = jnp.zeros_like(c_ref)
        m_ref[...] = jnp.zeros_like(m_ref)

    r = lax.broadcasted_iota(jnp.int32, (L, L), 0)
    c = lax.broadcasted_iota(jnp.int32, (L, L), 1)
    tri = c <= r
    lane = lax.broadcasted_iota(jnp.int32, (L, LANES), 1)
    ones_blk = jnp.where(lane == 0, 1.0, 0.0).astype(BF16)
    gc = gc_ref[...]
    gr = gr_ref[...]
    d = MLSTM_HEAD_DIM
    for h in range(MLSTM_HEADS):
        q = q_ref[:, h * d:(h + 1) * d]
        k = k_ref[:, h * d:(h + 1) * d]
        v = v_ref[:, h * d:(h + 1) * d]
        li_col = gc[:, h:h + 1]
        lf_col = gc[:, MLSTM_HEADS + h:MLSTM_HEADS + h + 1]
        li_row = gr[h:h + 1, :]
        lf_row = gr[MLSTM_HEADS + h:MLSTM_HEADS + h + 1, :]
        b_col = jnp.sum(jnp.where(tri, lf_row, 0.0), axis=1, keepdims=True)
        b_row = jnp.sum(jnp.where(r <= c, lf_col, 0.0), axis=0, keepdims=True)
        m_prev = m_ref[h][0:1, 0:1]
        log_d = jnp.where(tri, b_col - b_row + li_row, -jnp.inf)
        log_inter = b_col + m_prev
        m_t = jnp.maximum(log_inter, jnp.max(log_d, axis=1, keepdims=True))
        w_intra = jnp.exp(log_d - m_t)
        w_inter = jnp.exp(log_inter - m_t)
        s = lax.dot_general(q, k, (((1,), (1,)), ((), ())), preferred_element_type=F32) * w_intra
        vaug = jnp.concatenate([v, ones_blk], axis=1)
        r1 = jnp.dot(s.astype(BF16), vaug, preferred_element_type=F32)
        cst = c_ref[h]
        r2 = jnp.dot(q, cst.astype(BF16), preferred_element_type=F32)
        num = r1[:, :d] + w_inter * r2[:, :d]
        den = r1[:, d:d + 1] + w_inter * r2[:, d:d + 1]
        hv = num / jnp.maximum(jnp.abs(den), jnp.exp(-m_t))
        b_end = b_col[L - 1:L, :]
        log_w = b_end - b_col + li_col
        m_new = jnp.maximum(b_end + m_prev, jnp.max(log_w, axis=0, keepdims=True))
        w_s = jnp.exp(log_w - m_new)
        decay = jnp.exp(b_end + m_prev - m_new)
        kw = (k.astype(F32) * w_s).astype(BF16)
        upd = lax.dot_general(kw, vaug, (((0,), (0,)), ((), ())), preferred_element_type=F32)
        c_ref[h] = decay * cst + upd
        m_ref[h] = jnp.broadcast_to(m_new, (SUBLANES, LANES))
        hn = hv * lax.rsqrt(jnp.mean(hv * hv, axis=-1, keepdims=True) + RMS_EPS)
        gate = o_ref[:, h * d:(h + 1) * d].astype(F32)
        h_out[:, h * d:(h + 1) * d] = (hn * gate * og_ref[:, h * d:(h + 1) * d]).astype(BF16)


def _mlstm(mq, mk, mv, mo, gcol, grow, og, *, L):
    S = mq.shape[0]
    d = MLSTM_HEAD_DIM
    row = pl.BlockSpec((L, D_MLSTM), lambda i: (i, 0))
    return pl.pallas_call(
        functools.partial(_mlstm_kernel, L=L),
        grid=(S // L,),
        in_specs=[row, row, row, row,
                  pl.BlockSpec((L, 2 * MLSTM_HEADS), lambda i: (i, 0)),
                  pl.BlockSpec((2 * MLSTM_HEADS, L), lambda i: (0, i)),
                  pl.BlockSpec((1, D_MLSTM), lambda i: (0, 0))],
        out_specs=row,
        out_shape=jax.ShapeDtypeStruct((S, D_MLSTM), BF16),
        scratch_shapes=[pltpu.VMEM((MLSTM_HEADS, d, 2 * d), F32),
                        pltpu.VMEM((MLSTM_HEADS, SUBLANES, LANES), F32)],
        compiler_params=_cparams(("arbitrary",)),
        name="mlstm",
    )(mq, mk, mv, mo, gcol, grow, og)


def _layer_norm(z, g, b):
    mu = jnp.mean(z, axis=-1, keepdims=True)
    zc = z - mu
    var = jnp.mean(zc * zc, axis=-1, keepdims=True)
    return zc * lax.rsqrt(var + LN_EPS) * g + b


_R_GROUP0 = N_EXPERTS
_META_W, _META_E = 0, TOP_K


def _first_lane(mask, lane):
    return jnp.min(jnp.where(mask, lane, LANES), axis=-1, keepdims=True)


def _route_tokens(lg):
    lane = lax.broadcasted_iota(jnp.int32, lg.shape, 1)
    is_g = (lane >= _R_GROUP0) & (lane < _R_GROUP0 + N_GROUPS)
    gl = jnp.where(is_g, lg, -jnp.inf)
    ge = jnp.exp(gl - jnp.max(gl, axis=-1, keepdims=True))
    gp = ge / jnp.sum(ge, axis=-1, keepdims=True)
    g_p = jnp.max(gp, axis=-1, keepdims=True)
    g_top = _first_lane((gp == g_p) & is_g, lane) - _R_GROUP0
    in_grp = (lane < N_EXPERTS) & (lax.shift_right_logical(lane, 3) == g_top)
    el = jnp.where(in_grp, lg, -jnp.inf)
    ee = jnp.exp(el - jnp.max(el, axis=-1, keepdims=True))
    p = ee / jnp.sum(ee, axis=-1, keepdims=True)
    p1 = jnp.max(p, axis=-1, keepdims=True)
    i1 = _first_lane((p == p1) & in_grp, lane)
    pm = jnp.where(in_grp & (lane != i1), p, -jnp.inf)
    p2 = jnp.max(pm, axis=-1, keepdims=True)
    i2 = _first_lane(pm == p2, lane)
    den = p1 + p2
    w1 = g_p * (p1 / den)
    w2 = g_p * (p2 / den)
    meta = jnp.where(lane == _META_W, w1, jnp.where(lane == _META_W + 1, w2, 0.0))
    meta = jnp.where(lane == _META_E, i1.astype(F32), jnp.where(lane == _META_E + 1, i2.astype(F32), meta))
    chosen = ((lane == i1) | (lane == i2)).astype(BF16)
    return meta, chosen


def _outproj_kernel(x_ref, a_ref, h_ref, wo_ref, g_ref, b_ref, wr_ref, x1t_out, meta_out, oh_out, *,
                    alpha, tm):
    mix = jnp.concatenate([a_ref[...], h_ref[...]], axis=1)
    y = jnp.dot(mix, wo_ref[...], preferred_element_type=F32)
    x1 = _layer_norm(alpha * x_ref[...] + y, g_ref[...], b_ref[...])
    for c in range(SUBLANES):
        x1t_out[pl.ds(c, tm, stride=SUBLANES), :] = x1[:, c * LANES:(c + 1) * LANES]
    x_hi = x1.astype(BF16)
    x_lo = (x1 - x_hi.astype(F32)).astype(BF16)
    t = jnp.dot(x_hi, wr_ref[...], preferred_element_type=F32)
    lg = (t[:, :LANES] + t[:, LANES:]) + jnp.dot(x_lo, wr_ref[:, :LANES], preferred_element_type=F32)
    meta_out[...], oh_out[...] = _route_tokens(lg)


def _outproj(x, a, hm, wo, g, b, wr, *, tm, alpha):
    S = x.shape[0]
    row = lambda w: pl.BlockSpec((tm, w), lambda i: (i, 0))
    full = lambda arr: pl.BlockSpec(arr.shape, lambda i: (0,) * arr.ndim)
    return pl.pallas_call(
        functools.partial(_outproj_kernel, alpha=alpha, tm=tm),
        grid=(S // tm,),
        in_specs=[row(D_MODEL), row(MLA_HEADS * MLA_V_DIM), row(D_MLSTM), full(wo), full(g), full(b),
                  full(wr)],
        out_specs=(pl.BlockSpec((tm * SUBLANES, LANES), lambda i: (i, 0)), row(LANES), row(LANES)),
        out_shape=(jax.ShapeDtypeStruct((S * SUBLANES, LANES), F32),
                   jax.ShapeDtypeStruct((S, LANES), F32),
                   jax.ShapeDtypeStruct((S, LANES), BF16)),
        compiler_params=_cparams(("arbitrary",)),
        name="outproj",
    )(x, a, hm, wo, g, b, wr)


def _rank_kernel(oh_ref, meta_ref, rank_out, cnt_out, base_ref, *, tr):
    i = pl.program_id(0)

    @pl.when(i == 0)
    def _():
        base_ref[...] = jnp.zeros_like(base_ref)

    oh = oh_ref[...]
    r = lax.broadcasted_iota(jnp.int32, (tr, tr), 0)
    c = lax.broadcasted_iota(jnp.int32, (tr, tr), 1)
    before = (c < r).astype(BF16)
    seen = base_ref[0:1, :] + jnp.dot(before, oh, preferred_element_type=F32)
    lane = lax.broadcasted_iota(jnp.int32, (tr, LANES), 1)
    meta = meta_ref[...]
    out = jnp.zeros((tr, LANES), F32)
    for k in range(TOP_K):
        e_k = meta[:, _META_E + k:_META_E + k + 1].astype(jnp.int32)
        rk = jnp.sum(jnp.where(lane == e_k, seen, 0.0), axis=-1, keepdims=True)
        out = jnp.where(lane == k, rk, out)
    rank_out[...] = out
    total = base_ref[0:1, :] + jnp.sum(oh.astype(F32), axis=0, keepdims=True)
    base_ref[...] = jnp.broadcast_to(total, base_ref.shape)
    cnt_out[...] = jnp.broadcast_to(total, cnt_out.shape)


def _rank(oh, meta, *, tr):
    S = oh.shape[0]
    row = pl.BlockSpec((tr, LANES), lambda i: (i, 0))
    return pl.pallas_call(
        functools.partial(_rank_kernel, tr=tr),
        grid=(S // tr,),
        in_specs=[row, row],
        out_specs=(row, pl.BlockSpec((SUBLANES, LANES), lambda i: (0, 0))),
        out_shape=(jax.ShapeDtypeStruct((S, LANES), F32), jax.ShapeDtypeStruct((SUBLANES, LANES), F32)),
        scratch_shapes=[pltpu.VMEM((SUBLANES, LANES), F32)],
        compiler_params=_cparams(("arbitrary",)),
        name="rank",
    )(oh, meta)


TOKEN_TILE = SUBLANES
DMA_GROUP = 8


def _rows_from_tiles(ref, n):
    return jnp.concatenate([ref[pl.ds(c, n, stride=TOKEN_TILE), :] for c in range(TOKEN_TILE)], axis=1)


def _rows_to_tiles(ref, v):
    for c in range(TOKEN_TILE):
        ref[pl.ds(c, v.shape[0], stride=TOKEN_TILE), :] = v[:, c * LANES:(c + 1) * LANES]


def _pow2_chunks(n, top, fn):
    off = 0
    bit = top
    while bit:
        @pl.when((n & bit) != 0)
        def _(bit=bit, off=off):
            fn(off, bit)
        off = off + (n & bit)
        bit //= 2


def _dispatch_kernel(cnt_ref, pst_ref, d_ref, x_ref, xs_hbm, xbuf, zbuf, sem, zsem, *, tt):
    i = pl.program_id(0)
    n = pl.num_programs(0)
    G = TOKEN_TILE
    slot = lax.rem(i, 2)

    def row_copy(src_row, dst_row, rows, sl):
        return pltpu.make_async_copy(xbuf.at[sl, pl.ds(src_row * G, rows * G)],
                                     xs_hbm.at[pl.ds(dst_row * G, rows * G)], sem.at[sl])

    def zero_copy(dst_row, rows):
        return pltpu.make_async_copy(zbuf.at[pl.ds(0, rows * G)], xs_hbm.at[pl.ds(dst_row * G, rows * G)], zsem)

    @pl.when(i == 0)
    def _():
        zbuf[...] = jnp.zeros_like(zbuf)

        def fill(e, wait):
            pad = (-cnt_ref[e]) & (EXPERT_BLOCK - 1)
            first = pst_ref[e] + cnt_ref[e]
            if wait:
                _pow2_chunks(pad, EXPERT_BLOCK // 2, lambda off, size: zero_copy(0, size).wait())
            else:
                _pow2_chunks(pad, EXPERT_BLOCK // 2, lambda off, size: zero_copy(first + off, size).start())
            return 0
        lax.fori_loop(0, N_EXPERTS, lambda e, c: fill(e, False), 0)
        lax.fori_loop(0, N_EXPERTS, lambda e, c: fill(e, True), 0)
        half = EXPERT_BLOCK // 2
        used = (pst_ref[N_EXPERTS - 1] + cnt_ref[N_EXPERTS - 1] + EXPERT_BLOCK - 1) // EXPERT_BLOCK * 2
        n_half = xs_hbm.shape[0] // (half * G)

        def tail(c, wait):
            if wait:
                zero_copy(0, half).wait()
            else:
                zero_copy(c * half, half).start()
            return 0
        lax.fori_loop(used, n_half, lambda c, _: tail(c, False), 0)
        lax.fori_loop(used, n_half, lambda c, _: tail(c, True), 0)

    def wait_step(sl):
        for _ in range(TOP_K):
            row_copy(0, 0, tt, sl).wait()

    for sl in range(2):
        @pl.when(slot == sl)
        def _(sl=sl):
            @pl.when(i >= 2)
            def _():
                wait_step(sl)
            xbuf[sl] = x_ref[...]
            for t0 in range(0, tt, DMA_GROUP):
                idx = [d_ref[0, 0, j] for j in range(TOP_K * t0, TOP_K * (t0 + DMA_GROUP))]
                for j, dst in enumerate(idx):
                    row_copy(t0 + j // TOP_K, dst, 1, sl).start(priority=j % 2)

    @pl.when(i == n - 1)
    def _():
        @pl.when(n >= 2)
        def _():
            wait_step(1 - slot)
        wait_step(slot)


def _dispatch(x1t, dest, counts, pad_starts, n_slots, *, tt):
    G = TOKEN_TILE
    T = x1t.shape[0] // G
    nt = T // tt
    grid_spec = pltpu.PrefetchScalarGridSpec(
        num_scalar_prefetch=2,
        grid=(nt,),
        in_specs=[pl.BlockSpec((1, 1, TOP_K * tt), lambda i, c, p: (i, 0, 0), memory_space=pltpu.SMEM),
                  pl.BlockSpec((tt * G, LANES), lambda i, c, p: (i, 0))],
        out_specs=pl.BlockSpec(memory_space=pl.ANY),
        scratch_shapes=[pltpu.VMEM((2, tt * G, LANES), F32),
                        pltpu.VMEM((EXPERT_BLOCK // 2 * G, LANES), F32),
                        pltpu.SemaphoreType.DMA((2,)), pltpu.SemaphoreType.DMA])
    return pl.pallas_call(
        functools.partial(_dispatch_kernel, tt=tt),
        grid_spec=grid_spec,
        out_shape=jax.ShapeDtypeStruct((n_slots * G, LANES), F32),
        compiler_params=_cparams(("arbitrary",)),
        name="dispatch",
    )(counts, pad_starts, dest.reshape(nt, 1, TOP_K * tt), x1t)


X_RING = 3


def _moe_kernel(bexp_ref, nvb_ref, xs_hbm, wg_ref, wu_ref, wd_ref, ys_ref, xbuf, wgb, wub, wdb, xsem):
    b = pl.program_id(0)
    nvb = nvb_ref[0]
    rows = EXPERT_BLOCK * TOKEN_TILE

    def block_copy(j, sl):
        return pltpu.make_async_copy(xs_hbm.at[pl.ds(j * rows, rows)], xbuf.at[sl], xsem.at[sl])

    @pl.when(b == 0)
    def _():
        for j in range(X_RING - 1):
            @pl.when(j < nvb)
            def _(j=j):
                block_copy(j, j).start()

    ahead = b + (X_RING - 1)

    @pl.when(ahead < nvb)
    def _():
        block_copy(ahead, lax.rem(ahead, X_RING)).start()

    @pl.when((b == 0) | (bexp_ref[b] != bexp_ref[jnp.maximum(b - 1, 0)]))
    def _():
        wgb[...] = wg_ref[0, 0].astype(BF16)
        wub[...] = wu_ref[0, 0].astype(BF16)
        wdb[...] = wd_ref[0, 0].astype(BF16)

    @pl.when(b < nvb)
    def _():
        sl = lax.rem(b, X_RING)
        block_copy(b, sl).wait()
        xs = _rows_from_tiles(xbuf.at[sl], EXPERT_BLOCK).astype(BF16)
        g = jnp.dot(xs, wgb[...], preferred_element_type=F32)
        u = jnp.dot(xs, wub[...], preferred_element_type=F32)
        hid = (g * (1.0 / (1.0 + jnp.exp(-g)))) * u
        _rows_to_tiles(ys_ref, jnp.dot(hid.astype(BF16), wdb[...], preferred_element_type=F32))

    @pl.when(b >= nvb_ref[0])
    def _():
        ys_ref[...] = jnp.zeros_like(ys_ref)


def _moe(xs, block_expert, n_valid_blocks, wg, wu, wd, layer):
    G = TOKEN_TILE
    assert G * LANES == D_MODEL
    R = EXPERT_BLOCK
    nb = xs.shape[0] // (R * G)
    blk = lambda b, be, nvb: (b, 0)
    wspec = lambda shp: pl.BlockSpec((1, 1) + shp, lambda b, be, nvb: (layer, be[b], 0, 0))
    grid_spec = pltpu.PrefetchScalarGridSpec(
        num_scalar_prefetch=2,
        grid=(nb,),
        in_specs=[pl.BlockSpec(memory_space=pl.ANY),
                  wspec((D_MODEL, D_EXPERT)), wspec((D_MODEL, D_EXPERT)), wspec((D_EXPERT, D_MODEL))],
        out_specs=pl.BlockSpec((R * G, LANES), blk),
        scratch_shapes=[pltpu.VMEM((X_RING, R * G, LANES), F32),
                        pltpu.VMEM((D_MODEL, D_EXPERT), BF16), pltpu.VMEM((D_MODEL, D_EXPERT), BF16),
                        pltpu.VMEM((D_EXPERT, D_MODEL), BF16), pltpu.SemaphoreType.DMA((X_RING,))])
    return pl.pallas_call(
        _moe_kernel,
        grid_spec=grid_spec,
        out_shape=jax.ShapeDtypeStruct(xs.shape, F32),
        compiler_params=_cparams(("arbitrary",)),
        name="experts",
    )(block_expert, n_valid_blocks, xs, wg, wu, wd)


def _combine_kernel(d_ref, dn_ref, x_ref, w_ref, g_ref, b_ref, ys_hbm, o_ref, ybuf, sem, *, alpha, tt):
    i = pl.program_id(0)
    n = pl.num_programs(0)
    G = TOKEN_TILE
    slot = lax.rem(i, 2)

    def row_copy(src_row, r, rows, sl):
        return pltpu.make_async_copy(ys_hbm.at[pl.ds(src_row * G, rows * G)],
                                     ybuf.at[sl, pl.ds(r * G, rows * G)], sem.at[sl])

    def fetch(idx_ref, sl):
        for t0 in range(0, tt, DMA_GROUP):
            idx = [idx_ref[0, 0, j] for j in range(TOP_K * t0, TOP_K * (t0 + DMA_GROUP))]
            for j, src in enumerate(idx):
                row_copy(src, (j % TOP_K) * tt + t0 + j // TOP_K, 1, sl).start(priority=j % 2)

    @pl.when(i == 0)
    def _():
        fetch(d_ref, 0)

    for sl in range(2):
        @pl.when(slot == sl)
        def _(sl=sl):
            row_copy(0, 0, TOP_K * tt, sl).wait()
            fetch(dn_ref, 1 - sl)
            w = w_ref[...]
            y = (_rows_from_tiles(ybuf.at[sl, pl.ds(0, tt * G)], tt) * w[:, 0:1]
                 + _rows_from_tiles(ybuf.at[sl, pl.ds(tt * G, tt * G)], tt) * w[:, 1:2])
            o_ref[...] = _layer_norm(alpha * _rows_from_tiles(x_ref, tt) + y, g_ref[...], b_ref[...])

    @pl.when(i == n - 1)
    def _():
        row_copy(0, 0, TOP_K * tt, 1 - slot).wait()


def _combine(x1t, ys, dest, w, g, b, *, tt, alpha):
    G = TOKEN_TILE
    S = x1t.shape[0] // G
    nt = S // tt
    d3 = dest.reshape(nt, 1, TOP_K * tt)
    smem_blk = lambda f: pl.BlockSpec((1, 1, TOP_K * tt), f, memory_space=pltpu.SMEM)
    return pl.pallas_call(
        functools.partial(_combine_kernel, alpha=alpha, tt=tt),
        grid=(nt,),
        in_specs=[smem_blk(lambda i: (i, 0, 0)),
                  smem_blk(lambda i: (jnp.minimum(i + 1, nt - 1), 0, 0)),
                  pl.BlockSpec((tt * G, LANES), lambda i: (i, 0)),
                  pl.BlockSpec((tt, TOP_K), lambda i: (i, 0)),
                  pl.BlockSpec((1, D_MODEL), lambda i: (0, 0)),
                  pl.BlockSpec((1, D_MODEL), lambda i: (0, 0)),
                  pl.BlockSpec(memory_space=pl.ANY)],
        out_specs=pl.BlockSpec((tt, D_MODEL), lambda i: (i, 0)),
        out_shape=jax.ShapeDtypeStruct((S, D_MODEL), F32),
        scratch_shapes=[pltpu.VMEM((2, TOP_K * tt * G, LANES), F32), pltpu.SemaphoreType.DMA((2,))],
        compiler_params=_cparams(("arbitrary",)),
        name="combine",
    )(d3, d3, x1t, w, g, b, ys)


def _route(meta, rank, cnt, T):
    weights = meta[:, _META_W:_META_W + TOP_K]
    flat_e = meta[:, _META_E:_META_E + TOP_K].astype(jnp.int32).reshape(-1)
    n_assign = T * TOP_K
    counts = cnt[0, :N_EXPERTS].astype(jnp.int32)
    padded = (counts + EXPERT_BLOCK - 1) // EXPERT_BLOCK * EXPERT_BLOCK
    pad_ends = jnp.cumsum(padded)
    pad_starts = pad_ends - padded
    dest = pad_starts[flat_e] + rank[:, :TOP_K].astype(jnp.int32).reshape(-1)
    n_blocks = n_assign // EXPERT_BLOCK + N_EXPERTS
    block_start = jnp.arange(n_blocks, dtype=jnp.int32) * EXPERT_BLOCK
    block_expert = jnp.minimum(jnp.sum((pad_ends[None, :] <= block_start[:, None]).astype(jnp.int32), axis=1),
                               N_EXPERTS - 1).astype(jnp.int32)
    n_valid_blocks = (pad_ends[-1:] // EXPERT_BLOCK).astype(jnp.int32)
    return (dest, counts.astype(jnp.int32), pad_starts.astype(jnp.int32), block_expert, n_valid_blocks,
            n_blocks * EXPERT_BLOCK, weights)


def _in_weight(w_in):
    kr0 = MLA_Q_RANK + MLA_KV_RANK
    mq0 = kr0 + MLA_ROPE_DIM
    i0 = mq0 + 4 * D_MLSTM
    half = MLA_ROPE_DIM // 2
    n_small = 2 * MLA_ROPE_DIM + 2 * MLSTM_HEADS
    pad = jnp.zeros(w_in.shape[:-1] + (LANES - n_small,), w_in.dtype)
    return jnp.concatenate(
        [w_in[..., :kr0], w_in[..., mq0:i0], w_in[..., kr0:mq0], w_in[..., kr0 + half:mq0],
         w_in[..., kr0:kr0 + half], w_in[..., i0:i0 + 2 * MLSTM_HEADS], pad], axis=-1).astype(BF16)


def _q_weight(w_qb):
    lead = w_qb.shape[:-1]
    w = w_qb.reshape(lead + (MLA_HEADS, MLA_QK_DIM))
    half = MLA_ROPE_DIM // 2
    z = lambda n: jnp.zeros(lead + (MLA_HEADS, n), w_qb.dtype)
    main = jnp.concatenate([w, z(HEAD_PAD - MLA_QK_DIM)], axis=-1)
    swp = jnp.concatenate([z(MLA_NOPE_DIM), w[..., MLA_NOPE_DIM + half:], w[..., MLA_NOPE_DIM:MLA_NOPE_DIM + half],
                           z(HEAD_PAD - MLA_QK_DIM)], axis=-1)
    hw = MLA_HEADS * HEAD_PAD
    return jnp.concatenate([main.reshape(lead + (hw,)), swp.reshape(lead + (hw,))], axis=-1).astype(BF16)


def _kv_weight(w_kvb_l):
    hw = MLA_HEADS * HEAD_PAD
    w = w_kvb_l.reshape(MLA_KV_RANK, MLA_HEADS, MLA_NOPE_DIM + MLA_V_DIM)
    top_k = jnp.pad(w[:, :, :MLA_NOPE_DIM], ((0, 0), (0, 0), (0, HEAD_PAD - MLA_NOPE_DIM)))
    top_v = jnp.pad(w[:, :, MLA_NOPE_DIM:], ((0, 0), (0, 0), (0, HEAD_PAD - MLA_V_DIM)))
    bot_k = np.zeros((LANES, MLA_HEADS, HEAD_PAD), np.float32)
    bot_v = np.zeros((LANES, MLA_HEADS, HEAD_PAD), np.float32)
    for r in range(MLA_ROPE_DIM):
        bot_k[r, :, MLA_NOPE_DIM + r] = 1.0
    bot_v[_S_ONE, :, MLA_V_DIM] = 1.0
    wk = jnp.concatenate([top_k.reshape(MLA_KV_RANK, hw), jnp.asarray(bot_k).reshape(LANES, hw)], 0)
    wv = jnp.concatenate([top_v.reshape(MLA_KV_RANK, hw), jnp.asarray(bot_v).reshape(LANES, hw)], 0)
    return jnp.concatenate([wk, wv], axis=1).astype(BF16)


def _rope_tables(positions):
    half = MLA_ROPE_DIM // 2
    inv_freq = ROPE_THETA ** (-jnp.arange(half, dtype=F32) / half)
    ang = positions.astype(F32)[:, None] * inv_freq
    cos, sin = jnp.cos(ang), jnp.sin(ang)
    ct = jnp.concatenate([cos, cos], -1)
    st = jnp.concatenate([-sin, sin], -1)
    S = positions.shape[0]
    z = lambda w: jnp.zeros((S, w), F32)
    scale = MLA_QK_DIM ** -0.5 * float(np.log2(np.e))
    tabk = jnp.concatenate([ct, st, z(LANES - 2 * MLA_ROPE_DIM)], -1)
    cqt = jnp.concatenate([jnp.full((S, MLA_NOPE_DIM), scale, F32), ct * scale,
                           z(HEAD_PAD - MLA_QK_DIM)], -1)
    sqt = jnp.concatenate([z(MLA_NOPE_DIM), st * scale, z(HEAD_PAD - MLA_QK_DIM)], -1)
    return tabk, cqt, sqt


def _pick(S, pref):
    t = pref
    while S % t:
        t //= 2
    return t


def kernel(x, positions, w_in, conv_w, conv_b, gate_b, q_a_g, kv_a_g, w_qb, w_kvb, out_g, w_out, ln1_g, ln1_b, w_rg, w_re, w_gate, w_up, w_down, ln2_g, ln2_b):
    B, S, D = x.shape
    assert B == 1 and D == D_MODEL
    depth = w_in.shape[0]
    alpha = float((2 * depth) ** 0.25)
    tm = _pick(S, 512)
    bq = _pick(S, 512)
    L = _pick(S, 256)
    tt = _pick(S, 256)
    assert tm % SUBLANES == 0 and bq % LANES == 0 and L % LANES == 0

    tabk, cqt, sqt = _rope_tables(positions[0])
    win_all = _in_weight(w_in)
    wq_all = _q_weight(w_qb)
    gb_tab = jnp.zeros((depth, 1, LANES), F32).at[:, 0, _S_I0:_S_I0 + 2 * MLSTM_HEADS].set(gate_b)
    wr_f32 = jnp.concatenate([w_re, w_rg, jnp.zeros((depth, D_MODEL, LANES - N_GROUPS - N_EXPERTS), F32)], -1)
    wr_hi = wr_f32.astype(BF16)
    wr_all = jnp.concatenate([wr_hi, (wr_f32 - wr_hi.astype(F32)).astype(BF16)], -1)
    wo_all = w_out.astype(BF16)
    n_att = MLA_HEADS * MLA_V_DIM

    xs = x[0]
    for l in range(depth):
        q, k, v, mq, mk, mv, mo, gates = _proj(
            xs, win_all[l], wq_all[l], _kv_weight(w_kvb[l]), q_a_g[l][None], kv_a_g[l][None],
            conv_w[l], conv_b[l][None], gb_tab[l], tabk, cqt, sqt, tm=tm)
        a = _attention(q, k, v, out_g[l][None, :n_att], bq=bq)
        gcol = gates[:, _S_I0:_S_I0 + 2 * MLSTM_HEADS]
        hm = _mlstm(mq, mk, mv, mo, gcol, gcol.T, out_g[l][None, n_att:], L=L)
        x1t, meta, chosen = _outproj(xs, a, hm, wo_all[l], ln1_g[l][None], ln1_b[l][None], wr_all[l],
                                     tm=tm, alpha=alpha)
        rank, cnt = _rank(chosen, meta, tr=tm)
        dest, counts, pad_starts, bexp, nvb, n_slots, rw = _route(meta, rank, cnt, S)
        xsort = _dispatch(x1t, dest, counts, pad_starts, n_slots, tt=tt)
        ysort = _moe(xsort, bexp, nvb, w_gate, w_up, w_down, l)
        xs = _combine(x1t, ysort, dest, rw, ln2_g[l][None], ln2_b[l][None], tt=tt, alpha=alpha)
    return xs[None]
```

```python
import functools

import numpy as np
import jax
import jax.numpy as jnp
from jax import lax
from jax.experimental import pallas as pl
from jax.experimental.pallas import tpu as pltpu

F32 = jnp.float32
BF16 = jnp.bfloat16

D_MODEL = 1024
MLA_HEADS = 8
MLA_V_DIM = 64
MLA_NOPE_DIM = 64
MLA_ROPE_DIM = 32
MLA_QK_DIM = MLA_NOPE_DIM + MLA_ROPE_DIM
MLA_Q_RANK = 256
MLA_KV_RANK = 128
ROPE_THETA = 10000.0
MLSTM_HEADS = 4
MLSTM_HEAD_DIM = 128
D_MLSTM = MLSTM_HEADS * MLSTM_HEAD_DIM
CONV_WIDTH = 4
N_GROUPS = 8
EXPERTS_PER_GROUP = 8
N_EXPERTS = N_GROUPS * EXPERTS_PER_GROUP
D_EXPERT = 256
TOP_K = 2
EXPERT_BLOCK = 128
LN_EPS = 1e-5
RMS_EPS = 1e-6

LANES = 128
SUBLANES = 8
HEAD_PAD = 128
VMEM_LIMIT = 56 * 1024 * 1024

_C_Q0 = 0
_C_KV0 = MLA_Q_RANK
_C_MQK0 = _C_KV0 + MLA_KV_RANK
_C_MV0 = _C_MQK0 + 2 * D_MLSTM
_C_MO0 = _C_MV0 + D_MLSTM
_C_SMALL0 = _C_MO0 + D_MLSTM
D_IN_PAD = _C_SMALL0 + LANES
_S_I0 = 2 * MLA_ROPE_DIM
_S_F0 = _S_I0 + MLSTM_HEADS
_S_ONE = MLA_ROPE_DIM


def _cparams(sem, vmem=VMEM_LIMIT):
    return pltpu.CompilerParams(dimension_semantics=sem, vmem_limit_bytes=vmem)


def _proj_kernel(x_ref, win_ref, wq_ref, wkv_ref, qg_ref, kvg_ref, cw_ref, cb_ref, gb_ref,
                 tabk_ref, cq_ref, sq_ref,
                 q_out, k_out, v_out, mq_out, mk_out, mv_out, mo_out, g_out, carry_ref, *, tm):
    i = pl.program_id(0)

    @pl.when(i == 0)
    def _():
        carry_ref[...] = jnp.zeros_like(carry_ref)

    xb = x_ref[...].astype(BF16)
    p = jnp.dot(xb, win_ref[...], preferred_element_type=F32)

    def rms(v, g):
        return v * lax.rsqrt(jnp.mean(v * v, axis=-1, keepdims=True) + RMS_EPS) * g

    cqn = rms(p[:, _C_Q0:_C_KV0], qg_ref[...]).astype(BF16)
    ckvn = rms(p[:, _C_KV0:_C_MQK0], kvg_ref[...]).astype(BF16)

    qq = jnp.dot(cqn, wq_ref[...], preferred_element_type=F32)
    cq = cq_ref[...]
    sq = sq_ref[...]
    hw = MLA_HEADS * HEAD_PAD
    for h in range(MLA_HEADS):
        a = qq[:, h * HEAD_PAD:(h + 1) * HEAD_PAD]
        b = qq[:, hw + h * HEAD_PAD: hw + (h + 1) * HEAD_PAD]
        q_out[:, h * HEAD_PAD:(h + 1) * HEAD_PAD] = (a * cq + b * sq).astype(BF16)

    small = p[:, _C_SMALL0:D_IN_PAD]
    lane = lax.broadcasted_iota(jnp.int32, small.shape, 1)
    prod = small * tabk_ref[...]
    kr = prod + pltpu.roll(prod, LANES - MLA_ROPE_DIM, 1)
    kvs = jnp.where(lane < MLA_ROPE_DIM, kr, jnp.where(lane == _S_ONE, 1.0, 0.0))
    kvin = jnp.concatenate([ckvn, kvs.astype(BF16)], axis=1)
    kv = jnp.dot(kvin, wkv_ref[...], preferred_element_type=F32)
    k_out[...] = kv[:, :hw].astype(BF16)
    v_out[...] = kv[:, hw:].T.astype(BF16)

    g = small + gb_ref[...]
    lsig = jnp.minimum(g, 0.0) - jnp.log1p(jnp.exp(-jnp.abs(g)))
    is_f = (lane >= _S_F0) & (lane < _S_F0 + MLSTM_HEADS)
    g_out[...] = jnp.where(is_f, lsig, g)

    mqk = p[:, _C_MQK0:_C_MV0]
    ext = jnp.concatenate([carry_ref[...], mqk], axis=0)
    carry_ref[...] = mqk[tm - SUBLANES:tm, :]
    acc = cb_ref[...] + mqk * cw_ref[CONV_WIDTH - 1:CONV_WIDTH, :]
    for j in range(CONV_WIDTH - 1):
        off = SUBLANES - (CONV_WIDTH - 1) + j
        acc = acc + ext[off:off + tm, :] * cw_ref[j:j + 1, :]
    y = acc * (1.0 / (1.0 + jnp.exp(-acc)))
    mq_out[...] = (y[:, :D_MLSTM] * (MLSTM_HEAD_DIM ** -0.5)).astype(BF16)
    mk_out[...] = y[:, D_MLSTM:].astype(BF16)
    mv_out[...] = p[:, _C_MV0:_C_MO0].astype(BF16)
    mo = p[:, _C_MO0:_C_SMALL0]
    mo_out[...] = (1.0 / (1.0 + jnp.exp(-mo))).astype(BF16)


def _proj(x, win, wq, wkv, qg, kvg, cw, cb, gb, tabk, cqt, sqt, *, tm):
    S = x.shape[0]
    hw = MLA_HEADS * HEAD_PAD
    row = lambda w: pl.BlockSpec((tm, w), lambda i: (i, 0))
    full = lambda a: pl.BlockSpec(a.shape, lambda i: (0,) * a.ndim)
    out_shapes = (
        jax.ShapeDtypeStruct((S, hw), BF16), jax.ShapeDtypeStruct((S, hw), BF16),
        jax.ShapeDtypeStruct((hw, S), BF16),
        jax.ShapeDtypeStruct((S, D_MLSTM), BF16), jax.ShapeDtypeStruct((S, D_MLSTM), BF16),
        jax.ShapeDtypeStruct((S, D_MLSTM), BF16), jax.ShapeDtypeStruct((S, D_MLSTM), BF16),
        jax.ShapeDtypeStruct((S, LANES), F32))
    return pl.pallas_call(
        functools.partial(_proj_kernel, tm=tm),
        grid=(S // tm,),
        in_specs=[row(D_MODEL), full(win), full(wq), full(wkv), full(qg), full(kvg), full(cw),
                  full(cb), full(gb), row(LANES), row(LANES), row(LANES)],
        out_specs=(row(hw), row(hw), pl.BlockSpec((hw, tm), lambda i: (0, i)), row(D_MLSTM), row(D_MLSTM), row(D_MLSTM),
                   row(D_MLSTM), row(LANES)),
        out_shape=out_shapes,
        scratch_shapes=[pltpu.VMEM((SUBLANES, 2 * D_MLSTM), F32)],
        compiler_params=_cparams(("arbitrary",)),
        name="proj",
    )(x, win, wq, wkv, qg, kvg, cw, cb, gb, tabk, cqt, sqt)


VT_ROWS = 80


def _attn_kernel(q_ref, k_ref, vt_ref, og_ref, o_ref, s_ref, *, bq, heads_per_step):
    qi = pl.program_id(1)
    nh = heads_per_step
    cols = [(h * HEAD_PAD, (h + 1) * HEAD_PAD) for h in range(nh)]
    qs = [q_ref[:, c0:c1] for c0, c1 in cols]

    def scores(j, h):
        kt = k_ref[pl.ds(pl.multiple_of(j * bq, bq), bq), cols[h][0]:cols[h][1]]
        return lax.dot_general(kt, qs[h], (((1,), (1,)), ((), ())), preferred_element_type=F32)

    def accumulate(j, h, s, m, acc):
        vt = vt_ref[h * HEAD_PAD:h * HEAD_PAD + VT_ROWS, pl.ds(pl.multiple_of(j * bq, bq), bq)]
        m_new = jnp.maximum(m, jnp.max(s, axis=0, keepdims=True))
        alpha = jnp.exp2(m - m_new)
        p = jnp.exp2(s - m_new)
        return m_new, alpha * acc + jnp.dot(vt, p.astype(BF16), preferred_element_type=F32)

    def step(j, carry, cur, diagonal=False, prefetch=True):
        new = []
        for h in range(nh):
            m, acc = carry[2 * h:2 * h + 2]
            if prefetch:
                s_ref[(1 - cur) * nh + h] = scores(j + 1, h)
            s = s_ref[cur * nh + h]
            if diagonal:
                key = lax.broadcasted_iota(jnp.int32, (bq, bq), 0)
                qry = lax.broadcasted_iota(jnp.int32, (bq, bq), 1)
                s = jnp.where(key <= qry, s, -jnp.inf)
            new += list(accumulate(j, h, s, m, acc))
        return tuple(new)

    def pair(i, carry):
        return step(2 * i + 1, step(2 * i, carry, 0), 1)

    init = []
    for h in range(nh):
        s_ref[h] = scores(0, h)
        init += [jnp.full((1, bq), -jnp.inf, F32), jnp.zeros((VT_ROWS, bq), F32)]
    n_pairs = lax.shift_right_logical(qi, 1)
    n_quads = lax.shift_right_logical(qi, 2)
    n_octs = lax.shift_right_logical(qi, 3)
    quad = lambda i, cr: pair(2 * i + 1, pair(2 * i, cr))
    carry = lax.fori_loop(0, n_octs, lambda i, cr: quad(2 * i + 1, quad(2 * i, cr)), tuple(init))
    carry = lax.fori_loop(2 * n_octs, n_quads, quad, carry)
    carry = lax.fori_loop(2 * n_quads, n_pairs, pair, carry)
    carry = lax.cond(
        qi % 2 == 1,
        lambda cr: step(qi, step(qi - 1, cr, 0), 1, diagonal=True, prefetch=False),
        lambda cr: step(qi, cr, 0, diagonal=True, prefetch=False),
        carry)
    outs = []
    for h in range(nh):
        acc = carry[2 * h + 1]
        o = acc[:MLA_V_DIM, :] / acc[MLA_V_DIM:MLA_V_DIM + 1, :]
        o = o * lax.rsqrt(jnp.mean(o * o, axis=0, keepdims=True) + RMS_EPS)
        outs.append(o.T)
    o_ref[...] = (jnp.concatenate(outs, axis=1) * og_ref[...]).astype(BF16)


def _attention(q, k, vt, og, *, bq, heads_per_step=2):
    S = q.shape[0]
    n_hp = MLA_HEADS // heads_per_step
    wq = heads_per_step * HEAD_PAD
    wo = heads_per_step * MLA_V_DIM
    return pl.pallas_call(
        functools.partial(_attn_kernel, bq=bq, heads_per_step=heads_per_step),
        grid=(n_hp, S // bq),
        in_specs=[pl.BlockSpec((bq, wq), lambda h, i: (i, h)),
                  pl.BlockSpec((S, wq), lambda h, i: (0, h)),
                  pl.BlockSpec((wq, S), lambda h, i: (h, 0)),
                  pl.BlockSpec((1, wo), lambda h, i: (0, h))],
        out_specs=pl.BlockSpec((bq, wo), lambda h, i: (i, h)),
        out_shape=jax.ShapeDtypeStruct((S, MLA_HEADS * MLA_V_DIM), BF16),
        scratch_shapes=[pltpu.VMEM((2 * heads_per_step, bq, bq), F32)],
        compiler_params=_cparams(("arbitrary", "arbitrary")),
        name="attention",
    )(q, k, vt, og)


def _mlstm_kernel(q_ref, k_ref, v_ref, o_ref, gc_ref, gr_ref, og_ref, h_out, c_ref, m_ref, *, L):
    ci = pl.program_id(0)

    @pl.when(ci == 0)
    def _():
        c_ref[...] = jnp.zeros_like(c_ref)
        m_ref[...] = jnp.zeros_like(m_ref)

    r = lax.broadcasted_iota(jnp.int32, (L, L), 0)
    c = lax.broadcasted_iota(jnp.int32, (L, L), 1)
    tri = c <= r
    lane = lax.broadcasted_iota(jnp.int32, (L, LANES), 1)
    ones_blk = jnp.where(lane == 0, 1.0, 0.0).astype(BF16)
    gc = gc_ref[...]
    gr = gr_ref[...]
    d = MLSTM_HEAD_DIM
    for h in range(MLSTM_HEADS):
        q = q_ref[:, h * d:(h + 1) * d]
        k = k_ref[:, h * d:(h + 1) * d]
        v = v_ref[:, h * d:(h + 1) * d]
        li_col = gc[:, h:h + 1]
        lf_col = gc[:, MLSTM_HEADS + h:MLSTM_HEADS + h + 1]
        li_row = gr[h:h + 1, :]
        lf_row = gr[MLSTM_HEADS + h:MLSTM_HEADS + h + 1, :]
        b_col = jnp.sum(jnp.where(tri, lf_row, 0.0), axis=1, keepdims=True)
        b_row = jnp.sum(jnp.where(r <= c, lf_col, 0.0), axis=0, keepdims=True)
        m_prev = m_ref[h][0:1, 0:1]
        log_d = jnp.where(tri, b_col - b_row + li_row, -jnp.inf)
        log_inter = b_col + m_prev
        m_t = jnp.maximum(log_inter, jnp.max(log_d, axis=1, keepdims=True))
        w_intra = jnp.exp(log_d - m_t)
        w_inter = jnp.exp(log_inter - m_t)
        s = lax.dot_general(q, k, (((1,), (1,)), ((), ())), preferred_element_type=F32) * w_intra
        vaug = jnp.concatenate([v, ones_blk], axis=1)
        r1 = jnp.dot(s.astype(BF16), vaug, preferred_element_type=F32)
        cst = c_ref[h]
        r2 = jnp.dot(q, cst.astype(BF16), preferred_element_type=F32)
        num = r1[:, :d] + w_inter * r2[:, :d]
        den = r1[:, d:d + 1] + w_inter * r2[:, d:d + 1]
        hv = num / jnp.maximum(jnp.abs(den), jnp.exp(-m_t))
        b_end = b_col[L - 1:L, :]
        log_w = b_end - b_col + li_col
        m_new = jnp.maximum(b_end + m_prev, jnp.max(log_w, axis=0, keepdims=True))
        w_s = jnp.exp(log_w - m_new)
        decay = jnp.exp(b_end + m_prev - m_new)
        kw = (k.astype(F32) * w_s).astype(BF16)
        upd = lax.dot_general(kw, vaug, (((0,), (0,)), ((), ())), preferred_element_type=F32)
        c_ref[h] = decay * cst + upd
        m_ref[h] = jnp.broadcast_to(m_new, (SUBLANES, LANES))
        hn = hv * lax.rsqrt(jnp.mean(hv * hv, axis=-1, keepdims=True) + RMS_EPS)
        gate = o_ref[:, h * d:(h + 1) * d].astype(F32)
        h_out[:, h * d:(h + 1) * d] = (hn * gate * og_ref[:, h * d:(h + 1) * d]).astype(BF16)


def _mlstm(mq, mk, mv, mo, gcol, grow, og, *, L):
    S = mq.shape[0]
    d = MLSTM_HEAD_DIM
    row = pl.BlockSpec((L, D_MLSTM), lambda i: (i, 0))
    return pl.pallas_call(
        functools.partial(_mlstm_kernel, L=L),
        grid=(S // L,),
        in_specs=[row, row, row, row,
                  pl.BlockSpec((L, 2 * MLSTM_HEADS), lambda i: (i, 0)),
                  pl.BlockSpec((2 * MLSTM_HEADS, L), lambda i: (0, i)),
                  pl.BlockSpec((1, D_MLSTM), lambda i: (0, 0))],
        out_specs=row,
        out_shape=jax.ShapeDtypeStruct((S, D_MLSTM), BF16),
        scratch_shapes=[pltpu.VMEM((MLSTM_HEADS, d, 2 * d), F32),
                        pltpu.VMEM((MLSTM_HEADS, SUBLANES, LANES), F32)],
        compiler_params=_cparams(("arbitrary",)),
        name="mlstm",
    )(mq, mk, mv, mo, gcol, grow, og)


def _layer_norm(z, g, b):
    mu = jnp.mean(z, axis=-1, keepdims=True)
    zc = z - mu
    var = jnp.mean(zc * zc, axis=-1, keepdims=True)
    return zc * lax.rsqrt(var + LN_EPS) * g + b


_R_GROUP0 = N_EXPERTS
_META_W, _META_E = 0, TOP_K


def _first_lane(mask, lane):
    return jnp.min(jnp.where(mask, lane, LANES), axis=-1, keepdims=True)


def _route_tokens(lg):
    lane = lax.broadcasted_iota(jnp.int32, lg.shape, 1)
    is_g = (lane >= _R_GROUP0) & (lane < _R_GROUP0 + N_GROUPS)
    gl = jnp.where(is_g, lg, -jnp.inf)
    ge = jnp.exp(gl - jnp.max(gl, axis=-1, keepdims=True))
    gp = ge / jnp.sum(ge, axis=-1, keepdims=True)
    g_p = jnp.max(gp, axis=-1, keepdims=True)
    g_top = _first_lane((gp == g_p) & is_g, lane) - _R_GROUP0
    in_grp = (lane < N_EXPERTS) & (lax.shift_right_logical(lane, 3) == g_top)
    el = jnp.where(in_grp, lg, -jnp.inf)
    ee = jnp.exp(el - jnp.max(el, axis=-1, keepdims=True))
    p = ee / jnp.sum(ee, axis=-1, keepdims=True)
    p1 = jnp.max(p, axis=-1, keepdims=True)
    i1 = _first_lane((p == p1) & in_grp, lane)
    pm = jnp.where(in_grp & (lane != i1), p, -jnp.inf)
    p2 = jnp.max(pm, axis=-1, keepdims=True)
    i2 = _first_lane(pm == p2, lane)
    den = p1 + p2
    w1 = g_p * (p1 / den)
    w2 = g_p * (p2 / den)
    meta = jnp.where(lane == _META_W, w1, jnp.where(lane == _META_W + 1, w2, 0.0))
    meta = jnp.where(lane == _META_E, i1.astype(F32), jnp.where(lane == _META_E + 1, i2.astype(F32), meta))
    chosen = ((lane == i1) | (lane == i2)).astype(BF16)
    return meta, chosen


def _outproj_kernel(x_ref, a_ref, h_ref, wo_ref, g_ref, b_ref, wr_ref, x1t_out, meta_out, oh_out, *,
                    alpha, tm):
    mix = jnp.concatenate([a_ref[...], h_ref[...]], axis=1)
    y = jnp.dot(mix, wo_ref[...], preferred_element_type=F32)
    x1 = _layer_norm(alpha * x_ref[...] + y, g_ref[...], b_ref[...])
    for c in range(SUBLANES):
        x1t_out[pl.ds(c, tm, stride=SUBLANES), :] = x1[:, c * LANES:(c + 1) * LANES]
    x_hi = x1.astype(BF16)
    x_lo = (x1 - x_hi.astype(F32)).astype(BF16)
    t = jnp.dot(x_hi, wr_ref[...], preferred_element_type=F32)
    lg = (t[:, :LANES] + t[:, LANES:]) + jnp.dot(x_lo, wr_ref[:, :LANES], preferred_element_type=F32)
    meta_out[...], oh_out[...] = _route_tokens(lg)


def _outproj(x, a, hm, wo, g, b, wr, *, tm, alpha):
    S = x.shape[0]
    row = lambda w: pl.BlockSpec((tm, w), lambda i: (i, 0))
    full = lambda arr: pl.BlockSpec(arr.shape, lambda i: (0,) * arr.ndim)
    return pl.pallas_call(
        functools.partial(_outproj_kernel, alpha=alpha, tm=tm),
        grid=(S // tm,),
        in_specs=[row(D_MODEL), row(MLA_HEADS * MLA_V_DIM), row(D_MLSTM), full(wo), full(g), full(b),
                  full(wr)],
        out_specs=(pl.BlockSpec((tm * SUBLANES, LANES), lambda i: (i, 0)), row(LANES), row(LANES)),
        out_shape=(jax.ShapeDtypeStruct((S * SUBLANES, LANES), F32),
                   jax.ShapeDtypeStruct((S, LANES), F32),
                   jax.ShapeDtypeStruct((S, LANES), BF16)),
        compiler_params=_cparams(("arbitrary",)),
        name="outproj",
    )(x, a, hm, wo, g, b, wr)


def _rank_kernel(oh_ref, meta_ref, rank_out, cnt_out, base_ref, *, tr):
    i = pl.program_id(0)

    @pl.when(i == 0)
    def _():
        base_ref[...] = jnp.zeros_like(base_ref)

    oh = oh_ref[...]
    r = lax.broadcasted_iota(jnp.int32, (tr, tr), 0)
    c = lax.broadcasted_iota(jnp.int32, (tr, tr), 1)
    before = (c < r).astype(BF16)
    seen = base_ref[0:1, :] + jnp.dot(before, oh, preferred_element_type=F32)
    lane = lax.broadcasted_iota(jnp.int32, (tr, LANES), 1)
    meta = meta_ref[...]
    out = jnp.zeros((tr, LANES), F32)
    for k in range(TOP_K):
        e_k = meta[:, _META_E + k:_META_E + k + 1].astype(jnp.int32)
        rk = jnp.sum(jnp.where(lane == e_k, seen, 0.0), axis=-1, keepdims=True)
        out = jnp.where(lane == k, rk, out)
    rank_out[...] = out
    total = base_ref[0:1, :] + jnp.sum(oh.astype(F32), axis=0, keepdims=True)
    base_ref[...] = jnp.broadcast_to(total, base_ref.shape)
    cnt_out[...] = jnp.broadcast_to(total, cnt_out.shape)


def _rank(oh, meta, *, tr):
    S = oh.shape[0]
    row = pl.BlockSpec((tr, LANES), lambda i: (i, 0))
    return pl.pallas_call(
        functools.partial(_rank_kernel, tr=tr),
        grid=(S // tr,),
        in_specs=[row, row],
        out_specs=(row, pl.BlockSpec((SUBLANES, LANES), lambda i: (0, 0))),
        out_shape=(jax.ShapeDtypeStruct((S, LANES), F32), jax.ShapeDtypeStruct((SUBLANES, LANES), F32)),
        scratch_shapes=[pltpu.VMEM((SUBLANES, LANES), F32)],
        compiler_params=_cparams(("arbitrary",)),
        name="rank",
    )(oh, meta)


TOKEN_TILE = SUBLANES
DMA_GROUP = 8


def _rows_from_tiles(ref, n):
    return jnp.concatenate([ref[pl.ds(c, n, stride=TOKEN_TILE), :] for c in range(TOKEN_TILE)], axis=1)


def _rows_to_tiles(ref, v):
    for c in range(TOKEN_TILE):
        ref[pl.ds(c, v.shape[0], stride=TOKEN_TILE), :] = v[:, c * LANES:(c + 1) * LANES]


def _pow2_chunks(n, top, fn):
    off = 0
    bit = top
    while bit:
        @pl.when((n & bit) != 0)
        def _(bit=bit, off=off):
            fn(off, bit)
        off = off + (n & bit)
        bit //= 2


def _dispatch_kernel(cnt_ref, pst_ref, d_ref, x_ref, xs_hbm, xbuf, zbuf, sem, zsem, *, tt):
    i = pl.program_id(0)
    n = pl.num_programs(0)
    G = TOKEN_TILE
    slot = lax.rem(i, 2)

    def row_copy(src_row, dst_row, rows, sl):
        return pltpu.make_async_copy(xbuf.at[sl, pl.ds(src_row * G, rows * G)],
                                     xs_hbm.at[pl.ds(dst_row * G, rows * G)], sem.at[sl])

    def zero_copy(dst_row, rows):
        return pltpu.make_async_copy(zbuf.at[pl.ds(0, rows * G)], xs_hbm.at[pl.ds(dst_row * G, rows * G)], zsem)

    @pl.when(i == 0)
    def _():
        zbuf[...] = jnp.zeros_like(zbuf)

        def fill(e, wait):
            pad = (-cnt_ref[e]) & (EXPERT_BLOCK - 1)
            first = pst_ref[e] + cnt_ref[e]
            if wait:
                _pow2_chunks(pad, EXPERT_BLOCK // 2, lambda off, size: zero_copy(0, size).wait())
            else:
                _pow2_chunks(pad, EXPERT_BLOCK // 2, lambda off, size: zero_copy(first + off, size).start())
            return 0
        lax.fori_loop(0, N_EXPERTS, lambda e, c: fill(e, False), 0)
        lax.fori_loop(0, N_EXPERTS, lambda e, c: fill(e, True), 0)
        half = EXPERT_BLOCK // 2
        used = (pst_ref[N_EXPERTS - 1] + cnt_ref[N_EXPERTS - 1] + EXPERT_BLOCK - 1) // EXPERT_BLOCK * 2
        n_half = xs_hbm.shape[0] // (half * G)

        def tail(c, wait):
            if wait:
                zero_copy(0, half).wait()
            else:
                zero_copy(c * half, half).start()
            return 0
        lax.fori_loop(used, n_half, lambda c, _: tail(c, False), 0)
        lax.fori_loop(used, n_half, lambda c, _: tail(c, True), 0)

    def wait_step(sl):
        for _ in range(TOP_K):
            row_copy(0, 0, tt, sl).wait()

    for sl in range(2):
        @pl.when(slot == sl)
        def _(sl=sl):
            @pl.when(i >= 2)
            def _():
                wait_step(sl)
            xbuf[sl] = x_ref[...]
            for t0 in range(0, tt, DMA_GROUP):
                idx = [d_ref[0, 0, j] for j in range(TOP_K * t0, TOP_K * (t0 + DMA_GROUP))]
                for j, dst in enumerate(idx):
                    row_copy(t0 + j // TOP_K, dst, 1, sl).start(priority=j % 2)

    @pl.when(i == n - 1)
    def _():
        @pl.when(n >= 2)
        def _():
            wait_step(1 - slot)
        wait_step(slot)


def _dispatch(x1t, dest, counts, pad_starts, n_slots, *, tt):
    G = TOKEN_TILE
    T = x1t.shape[0] // G
    nt = T // tt
    grid_spec = pltpu.PrefetchScalarGridSpec(
        num_scalar_prefetch=2,
        grid=(nt,),
        in_specs=[pl.BlockSpec((1, 1, TOP_K * tt), lambda i, c, p: (i, 0, 0), memory_space=pltpu.SMEM),
                  pl.BlockSpec((tt * G, LANES), lambda i, c, p: (i, 0))],
        out_specs=pl.BlockSpec(memory_space=pl.ANY),
        scratch_shapes=[pltpu.VMEM((2, tt * G, LANES), F32),
                        pltpu.VMEM((EXPERT_BLOCK // 2 * G, LANES), F32),
                        pltpu.SemaphoreType.DMA((2,)), pltpu.SemaphoreType.DMA])
    return pl.pallas_call(
        functools.partial(_dispatch_kernel, tt=tt),
        grid_spec=grid_spec,
        out_shape=jax.ShapeDtypeStruct((n_slots * G, LANES), F32),
        compiler_params=_cparams(("arbitrary",)),
        name="dispatch",
    )(counts, pad_starts, dest.reshape(nt, 1, TOP_K * tt), x1t)


X_RING = 3


def _moe_kernel(bexp_ref, nvb_ref, xs_hbm, wg_ref, wu_ref, wd_ref, ys_ref, xbuf, wgb, wub, wdb, xsem):
    b = pl.program_id(0)
    nvb = nvb_ref[0]
    rows = EXPERT_BLOCK * TOKEN_TILE

    def block_copy(j, sl):
        return pltpu.make_async_copy(xs_hbm.at[pl.ds(j * rows, rows)], xbuf.at[sl], xsem.at[sl])

    @pl.when(b == 0)
    def _():
        for j in range(X_RING - 1):
            @pl.when(j < nvb)
            def _(j=j):
                block_copy(j, j).start()

    ahead = b + (X_RING - 1)

    @pl.when(ahead < nvb)
    def _():
        block_copy(ahead, lax.rem(ahead, X_RING)).start()

    @pl.when((b == 0) | (bexp_ref[b] != bexp_ref[jnp.maximum(b - 1, 0)]))
    def _():
        wgb[...] = wg_ref[0, 0].astype(BF16)
        wub[...] = wu_ref[0, 0].astype(BF16)
        wdb[...] = wd_ref[0, 0].astype(BF16)

    @pl.when(b < nvb)
    def _():
        sl = lax.rem(b, X_RING)
        block_copy(b, sl).wait()
        xs = _rows_from_tiles(xbuf.at[sl], EXPERT_BLOCK).astype(BF16)
        g = jnp.dot(xs, wgb[...], preferred_element_type=F32)
        u = jnp.dot(xs, wub[...], preferred_element_type=F32)
        hid = (g * (1.0 / (1.0 + jnp.exp(-g)))) * u
        _rows_to_tiles(ys_ref, jnp.dot(hid.astype(BF16), wdb[...], preferred_element_type=F32))

    @pl.when(b >= nvb_ref[0])
    def _():
        ys_ref[...] = jnp.zeros_like(ys_ref)


def _moe(xs, block_expert, n_valid_blocks, wg, wu, wd, layer):
    G = TOKEN_TILE
    assert G * LANES == D_MODEL
    R = EXPERT_BLOCK
    nb = xs.shape[0] // (R * G)
    blk = lambda b, be, nvb: (b, 0)
    wspec = lambda shp: pl.BlockSpec((1, 1) + shp, lambda b, be, nvb: (layer, be[b], 0, 0))
    grid_spec = pltpu.PrefetchScalarGridSpec(
        num_scalar_prefetch=2,
        grid=(nb,),
        in_specs=[pl.BlockSpec(memory_space=pl.ANY),
                  wspec((D_MODEL, D_EXPERT)), wspec((D_MODEL, D_EXPERT)), wspec((D_EXPERT, D_MODEL))],
        out_specs=pl.BlockSpec((R * G, LANES), blk),
        scratch_shapes=[pltpu.VMEM((X_RING, R * G, LANES), F32),
                        pltpu.VMEM((D_MODEL, D_EXPERT), BF16), pltpu.VMEM((D_MODEL, D_EXPERT), BF16),
                        pltpu.VMEM((D_EXPERT, D_MODEL), BF16), pltpu.SemaphoreType.DMA((X_RING,))])
    return pl.pallas_call(
        _moe_kernel,
        grid_spec=grid_spec,
        out_shape=jax.ShapeDtypeStruct(xs.shape, F32),
        compiler_params=_cparams(("arbitrary",)),
        name="experts",
    )(block_expert, n_valid_blocks, xs, wg, wu, wd)


def _combine_kernel(d_ref, dn_ref, x_ref, w_ref, g_ref, b_ref, ys_hbm, o_ref, ybuf, sem, *, alpha, tt):
    i = pl.program_id(0)
    n = pl.num_programs(0)
    G = TOKEN_TILE
    slot = lax.rem(i, 2)

    def row_copy(src_row, r, rows, sl):
        return pltpu.make_async_copy(ys_hbm.at[pl.ds(src_row * G, rows * G)],
                                     ybuf.at[sl, pl.ds(r * G, rows * G)], sem.at[sl])

    def fetch(idx_ref, sl):
        for t0 in range(0, tt, DMA_GROUP):
            idx = [idx_ref[0, 0, j] for j in range(TOP_K * t0, TOP_K * (t0 + DMA_GROUP))]
            for j, src in enumerate(idx):
                row_copy(src, (j % TOP_K) * tt + t0 + j // TOP_K, 1, sl).start(priority=j % 2)

    @pl.when(i == 0)
    def _():
        fetch(d_ref, 0)

    for sl in range(2):
        @pl.when(slot == sl)
        def _(sl=sl):
            row_copy(0, 0, TOP_K * tt, sl).wait()
            fetch(dn_ref, 1 - sl)
            w = w_ref[...]
            y = (_rows_from_tiles(ybuf.at[sl, pl.ds(0, tt * G)], tt) * w[:, 0:1]
                 + _rows_from_tiles(ybuf.at[sl, pl.ds(tt * G, tt * G)], tt) * w[:, 1:2])
            o_ref[...] = _layer_norm(alpha * _rows_from_tiles(x_ref, tt) + y, g_ref[...], b_ref[...])

    @pl.when(i == n - 1)
    def _():
        row_copy(0, 0, TOP_K * tt, 1 - slot).wait()


def _combine(x1t, ys, dest, w, g, b, *, tt, alpha):
    G = TOKEN_TILE
    S = x1t.shape[0] // G
    nt = S // tt
    d3 = dest.reshape(nt, 1, TOP_K * tt)
    smem_blk = lambda f: pl.BlockSpec((1, 1, TOP_K * tt), f, memory_space=pltpu.SMEM)
    return pl.pallas_call(
        functools.partial(_combine_kernel, alpha=alpha, tt=tt),
        grid=(nt,),
        in_specs=[smem_blk(lambda i: (i, 0, 0)),
                  smem_blk(lambda i: (jnp.minimum(i + 1, nt - 1), 0, 0)),
                  pl.BlockSpec((tt * G, LANES), lambda i: (i, 0)),
                  pl.BlockSpec((tt, TOP_K), lambda i: (i, 0)),
                  pl.BlockSpec((1, D_MODEL), lambda i: (0, 0)),
                  pl.BlockSpec((1, D_MODEL), lambda i: (0, 0)),
                  pl.BlockSpec(memory_space=pl.ANY)],
        out_specs=pl.BlockSpec((tt, D_MODEL), lambda i: (i, 0)),
        out_shape=jax.ShapeDtypeStruct((S, D_MODEL), F32),
        scratch_shapes=[pltpu.VMEM((2, TOP_K * tt * G, LANES), F32), pltpu.SemaphoreType.DMA((2,))],
        compiler_params=_cparams(("arbitrary",)),
        name="combine",
    )(d3, d3, x1t, w, g, b, ys)


def _route(meta, rank, cnt, T):
    weights = meta[:, _META_W:_META_W + TOP_K]
    flat_e = meta[:, _META_E:_META_E + TOP_K].astype(jnp.int32).reshape(-1)
    n_assign = T * TOP_K
    counts = cnt[0, :N_EXPERTS].astype(jnp.int32)
    padded = (counts + EXPERT_BLOCK - 1) // EXPERT_BLOCK * EXPERT_BLOCK
    pad_ends = jnp.cumsum(padded)
    pad_starts = pad_ends - padded
    dest = pad_starts[flat_e] + rank[:, :TOP_K].astype(jnp.int32).reshape(-1)
    n_blocks = n_assign // EXPERT_BLOCK + N_EXPERTS
    block_start = jnp.arange(n_blocks, dtype=jnp.int32) * EXPERT_BLOCK
    block_expert = jnp.minimum(jnp.sum((pad_ends[None, :] <= block_start[:, None]).astype(jnp.int32), axis=1),
                               N_EXPERTS - 1).astype(jnp.int32)
    n_valid_blocks = (pad_ends[-1:] // EXPERT_BLOCK).astype(jnp.int32)
    return (dest, counts.astype(jnp.int32), pad_starts.astype(jnp.int32), block_expert, n_valid_blocks,
            n_blocks * EXPERT_BLOCK, weights)


def _in_weight(w_in):
    kr0 = MLA_Q_RANK + MLA_KV_RANK
    mq0 = kr0 + MLA_ROPE_DIM
    i0 = mq0 + 4 * D_MLSTM
    half = MLA_ROPE_DIM // 2
    n_small = 2 * MLA_ROPE_DIM + 2 * MLSTM_HEADS
    pad = jnp.zeros(w_in.shape[:-1] + (LANES - n_small,), w_in.dtype)
    return jnp.concatenate(
        [w_in[..., :kr0], w_in[..., mq0:i0], w_in[..., kr0:mq0], w_in[..., kr0 + half:mq0],
         w_in[..., kr0:kr0 + half], w_in[..., i0:i0 + 2 * MLSTM_HEADS], pad], axis=-1).astype(BF16)


def _q_weight(w_qb):
    lead = w_qb.shape[:-1]
    w = w_qb.reshape(lead + (MLA_HEADS, MLA_QK_DIM))
    half = MLA_ROPE_DIM // 2
    z = lambda n: jnp.zeros(lead + (MLA_HEADS, n), w_qb.dtype)
    main = jnp.concatenate([w, z(HEAD_PAD - MLA_QK_DIM)], axis=-1)
    swp = jnp.concatenate([z(MLA_NOPE_DIM), w[..., MLA_NOPE_DIM + half:], w[..., MLA_NOPE_DIM:MLA_NOPE_DIM + half],
                           z(HEAD_PAD - MLA_QK_DIM)], axis=-1)
    hw = MLA_HEADS * HEAD_PAD
    return jnp.concatenate([main.reshape(lead + (hw,)), swp.reshape(lead + (hw,))], axis=-1).astype(BF16)


def _kv_weight(w_kvb_l):
    hw = MLA_HEADS * HEAD_PAD
    w = w_kvb_l.reshape(MLA_KV_RANK, MLA_HEADS, MLA_NOPE_DIM + MLA_V_DIM)
    top_k = jnp.pad(w[:, :, :MLA_NOPE_DIM], ((0, 0), (0, 0), (0, HEAD_PAD - MLA_NOPE_DIM)))
    top_v = jnp.pad(w[:, :, MLA_NOPE_DIM:], ((0, 0), (0, 0), (0, HEAD_PAD - MLA_V_DIM)))
    bot_k = np.zeros((LANES, MLA_HEADS, HEAD_PAD), np.float32)
    bot_v = np.zeros((LANES, MLA_HEADS, HEAD_PAD), np.float32)
    for r in range(MLA_ROPE_DIM):
        bot_k[r, :, MLA_NOPE_DIM + r] = 1.0
    bot_v[_S_ONE, :, MLA_V_DIM] = 1.0
    wk = jnp.concatenate([top_k.reshape(MLA_KV_RANK, hw), jnp.asarray(bot_k).reshape(LANES, hw)], 0)
    wv = jnp.concatenate([top_v.reshape(MLA_KV_RANK, hw), jnp.asarray(bot_v).reshape(LANES, hw)], 0)
    return jnp.concatenate([wk, wv], axis=1).astype(BF16)


def _rope_tables(positions):
    half = MLA_ROPE_DIM // 2
    inv_freq = ROPE_THETA ** (-jnp.arange(half, dtype=F32) / half)
    ang = positions.astype(F32)[:, None] * inv_freq
    cos, sin = jnp.cos(ang), jnp.sin(ang)
    ct = jnp.concatenate([cos, cos], -1)
    st = jnp.concatenate([-sin, sin], -1)
    S = positions.shape[0]
    z = lambda w: jnp.zeros((S, w), F32)
    scale = MLA_QK_DIM ** -0.5 * float(np.log2(np.e))
    tabk = jnp.concatenate([ct, st, z(LANES - 2 * MLA_ROPE_DIM)], -1)
    cqt = jnp.concatenate([jnp.full((S, MLA_NOPE_DIM), scale, F32), ct * scale,
                           z(HEAD_PAD - MLA_QK_DIM)], -1)
    sqt = jnp.concatenate([z(MLA_NOPE_DIM), st * scale, z(HEAD_PAD - MLA_QK_DIM)], -1)
    return tabk, cqt, sqt


def _pick(S, pref):
    t = pref
    while S % t:
        t //= 2
    return t


def kernel(x, positions, w_in, conv_w, conv_b, gate_b, q_a_g, kv_a_g, w_qb, w_kvb, out_g, w_out, ln1_g, ln1_b, w_rg, w_re, w_gate, w_up, w_down, ln2_g, ln2_b):
    B, S, D = x.shape
    assert B == 1 and D == D_MODEL
    depth = w_in.shape[0]
    alpha = float((2 * depth) ** 0.25)
    tm = _pick(S, 512)
    bq = _pick(S, 512)
    L = _pick(S, 256)
    tt = _pick(S, 256)
    assert tm % SUBLANES == 0 and bq % LANES == 0 and L % LANES == 0

    tabk, cqt, sqt = _rope_tables(positions[0])
    win_all = _in_weight(w_in)
    wq_all = _q_weight(w_qb)
    gb_tab = jnp.zeros((depth, 1, LANES), F32).at[:, 0, _S_I0:_S_I0 + 2 * MLSTM_HEADS].set(gate_b)
    wr_f32 = jnp.concatenate([w_re, w_rg, jnp.zeros((depth, D_MODEL, LANES - N_GROUPS - N_EXPERTS), F32)], -1)
    wr_hi = wr_f32.astype(BF16)
    wr_all = jnp.concatenate([wr_hi, (wr_f32 - wr_hi.astype(F32)).astype(BF16)], -1)
    wo_all = w_out.astype(BF16)
    n_att = MLA_HEADS * MLA_V_DIM

    xs = x[0]
    for l in range(depth):
        q, k, v, mq, mk, mv, mo, gates = _proj(
            xs, win_all[l], wq_all[l], _kv_weight(w_kvb[l]), q_a_g[l][None], kv_a_g[l][None],
            conv_w[l], conv_b[l][None], gb_tab[l], tabk, cqt, sqt, tm=tm)
        a = _attention(q, k, v, out_g[l][None, :n_att], bq=bq)
        gcol = gates[:, _S_I0:_S_I0 + 2 * MLSTM_HEADS]
        hm = _mlstm(mq, mk, mv, mo, gcol, gcol.T, out_g[l][None, n_att:], L=L)
        x1t, meta, chosen = _outproj(xs, a, hm, wo_all[l], ln1_g[l][None], ln1_b[l][None], wr_all[l],
                                     tm=tm, alpha=alpha)
        rank, cnt = _rank(chosen, meta, tr=tm)
        dest, counts, pad_starts, bexp, nvb, n_slots, rw = _route(meta, rank, cnt, S)
        xsort = _dispatch(x1t, dest, counts, pad_starts, n_slots, tt=tt)
        ysort = _moe(xsort, bexp, nvb, w_gate, w_up, w_down, l)
        xs = _combine(x1t, ysort, dest, rw, ln2_g[l][None], ln2_b[l][None], tt=tt, alpha=alpha)
    return xs[None]
```

```python
import functools

import numpy as np
import jax
import jax.numpy as jnp
from jax import lax
from jax.experimental import pallas as pl
from jax.experimental.pallas import tpu as pltpu

F32 = jnp.float32
BF16 = jnp.bfloat16

D_MODEL = 1024
MLA_HEADS = 8
MLA_V_DIM = 64
MLA_NOPE_DIM = 64
MLA_ROPE_DIM = 32
MLA_QK_DIM = MLA_NOPE_DIM + MLA_ROPE_DIM
MLA_Q_RANK = 256
MLA_KV_RANK = 128
ROPE_THETA = 10000.0
MLSTM_HEADS = 4
MLSTM_HEAD_DIM = 128
D_MLSTM = MLSTM_HEADS * MLSTM_HEAD_DIM
CONV_WIDTH = 4
N_GROUPS = 8
EXPERTS_PER_GROUP = 8
N_EXPERTS = N_GROUPS * EXPERTS_PER_GROUP
D_EXPERT = 256
TOP_K = 2
EXPERT_BLOCK = 128
LN_EPS = 1e-5
RMS_EPS = 1e-6

LANES = 128
SUBLANES = 8
HEAD_PAD = 128
VMEM_LIMIT = 56 * 1024 * 1024

_C_Q0 = 0
_C_KV0 = MLA_Q_RANK
_C_MQK0 = _C_KV0 + MLA_KV_RANK
_C_MV0 = _C_MQK0 + 2 * D_MLSTM
_C_MO0 = _C_MV0 + D_MLSTM
_C_SMALL0 = _C_MO0 + D_MLSTM
D_IN_PAD = _C_SMALL0 + LANES
_S_I0 = 2 * MLA_ROPE_DIM
_S_F0 = _S_I0 + MLSTM_HEADS
_S_ONE = MLA_ROPE_DIM


def _cparams(sem, vmem=VMEM_LIMIT):
    return pltpu.CompilerParams(dimension_semantics=sem, vmem_limit_bytes=vmem)


def _proj_kernel(x_ref, win_ref, wq_ref, wkv_ref, qg_ref, kvg_ref, cw_ref, cb_ref, gb_ref,
                 tabk_ref, cq_ref, sq_ref,
                 q_out, k_out, v_out, mq_out, mk_out, mv_out, mo_out, g_out, carry_ref, *, tm):
    i = pl.program_id(0)

    @pl.when(i == 0)
    def _():
        carry_ref[...] = jnp.zeros_like(carry_ref)

    xb = x_ref[...].astype(BF16)
    p = jnp.dot(xb, win_ref[...], preferred_element_type=F32)

    def rms(v, g):
        return v * lax.rsqrt(jnp.mean(v * v, axis=-1, keepdims=True) + RMS_EPS) * g

    cqn = rms(p[:, _C_Q0:_C_KV0], qg_ref[...]).astype(BF16)
    ckvn = rms(p[:, _C_KV0:_C_MQK0], kvg_ref[...]).astype(BF16)

    qq = jnp.dot(cqn, wq_ref[...], preferred_element_type=F32)
    cq = cq_ref[...]
    sq = sq_ref[...]
    hw = MLA_HEADS * HEAD_PAD
    for h in range(MLA_HEADS):
        a = qq[:, h * HEAD_PAD:(h + 1) * HEAD_PAD]
        b = qq[:, hw + h * HEAD_PAD: hw + (h + 1) * HEAD_PAD]
        q_out[:, h * HEAD_PAD:(h + 1) * HEAD_PAD] = (a * cq + b * sq).astype(BF16)

    small = p[:, _C_SMALL0:D_IN_PAD]
    lane = lax.broadcasted_iota(jnp.int32, small.shape, 1)
    prod = small * tabk_ref[...]
    kr = prod + pltpu.roll(prod, LANES - MLA_ROPE_DIM, 1)
    kvs = jnp.where(lane < MLA_ROPE_DIM, kr, jnp.where(lane == _S_ONE, 1.0, 0.0))
    kvin = jnp.concatenate([ckvn, kvs.astype(BF16)], axis=1)
    kv = jnp.dot(kvin, wkv_ref[...], preferred_element_type=F32)
    k_out[...] = kv[:, :hw].astype(BF16)
    v_out[...] = kv[:, hw:].T.astype(BF16)

    g = small + gb_ref[...]
    lsig = jnp.minimum(g, 0.0) - jnp.log1p(jnp.exp(-jnp.abs(g)))
    is_f = (lane >= _S_F0) & (lane < _S_F0 + MLSTM_HEADS)
    g_out[...] = jnp.where(is_f, lsig, g)

    mqk = p[:, _C_MQK0:_C_MV0]
    ext = jnp.concatenate([carry_ref[...], mqk], axis=0)
    carry_ref[...] = mqk[tm - SUBLANES:tm, :]
    acc = cb_ref[...] + mqk * cw_ref[CONV_WIDTH - 1:CONV_WIDTH, :]
    for j in range(CONV_WIDTH - 1):
        off = SUBLANES - (CONV_WIDTH - 1) + j
        acc = acc + ext[off:off + tm, :] * cw_ref[j:j + 1, :]
    y = acc * (1.0 / (1.0 + jnp.exp(-acc)))
    mq_out[...] = (y[:, :D_MLSTM] * (MLSTM_HEAD_DIM ** -0.5)).astype(BF16)
    mk_out[...] = y[:, D_MLSTM:].astype(BF16)
    mv_out[...] = p[:, _C_MV0:_C_MO0].astype(BF16)
    mo = p[:, _C_MO0:_C_SMALL0]
    mo_out[...] = (1.0 / (1.0 + jnp.exp(-mo))).astype(BF16)


def _proj(x, win, wq, wkv, qg, kvg, cw, cb, gb, tabk, cqt, sqt, *, tm):
    S = x.shape[0]
    hw = MLA_HEADS * HEAD_PAD
    row = lambda w: pl.BlockSpec((tm, w), lambda i: (i, 0))
    full = lambda a: pl.BlockSpec(a.shape, lambda i: (0,) * a.ndim)
    out_shapes = (
        jax.ShapeDtypeStruct((S, hw), BF16), jax.ShapeDtypeStruct((S, hw), BF16),
        jax.ShapeDtypeStruct((hw, S), BF16),
        jax.ShapeDtypeStruct((S, D_MLSTM), BF16), jax.ShapeDtypeStruct((S, D_MLSTM), BF16),
        jax.ShapeDtypeStruct((S, D_MLSTM), BF16), jax.ShapeDtypeStruct((S, D_MLSTM), BF16),
        jax.ShapeDtypeStruct((S, LANES), F32))
    return pl.pallas_call(
        functools.partial(_proj_kernel, tm=tm),
        grid=(S // tm,),
        in_specs=[row(D_MODEL), full(win), full(wq), full(wkv), full(qg), full(kvg), full(cw),
                  full(cb), full(gb), row(LANES), row(LANES), row(LANES)],
        out_specs=(row(hw), row(hw), pl.BlockSpec((hw, tm), lambda i: (0, i)), row(D_MLSTM), row(D_MLSTM), row(D_MLSTM),
                   row(D_MLSTM), row(LANES)),
        out_shape=out_shapes,
        scratch_shapes=[pltpu.VMEM((SUBLANES, 2 * D_MLSTM), F32)],
        compiler_params=_cparams(("arbitrary",)),
        name="proj",
    )(x, win, wq, wkv, qg, kvg, cw, cb, gb, tabk, cqt, sqt)


VT_ROWS = 80


def _attn_kernel(q_ref, k_ref, vt_ref, og_ref, o_ref, s_ref, *, bq, heads_per_step):
    qi = pl.program_id(1)
    nh = heads_per_step
    cols = [(h * HEAD_PAD, (h + 1) * HEAD_PAD) for h in range(nh)]
    qs = [q_ref[:, c0:c1] for c0, c1 in cols]

    def scores(j, h):
        kt = k_ref[pl.ds(pl.multiple_of(j * bq, bq), bq), cols[h][0]:cols[h][1]]
        return lax.dot_general(kt, qs[h], (((1,), (1,)), ((), ())), preferred_element_type=F32)

    def accumulate(j, h, s, m, acc):
        vt = vt_ref[h * HEAD_PAD:h * HEAD_PAD + VT_ROWS, pl.ds(pl.multiple_of(j * bq, bq), bq)]
        m_new = jnp.maximum(m, jnp.max(s, axis=0, keepdims=True))
        alpha = jnp.exp2(m - m_new)
        p = jnp.exp2(s - m_new)
        return m_new, alpha * acc + jnp.dot(vt, p.astype(BF16), preferred_element_type=F32)

    def step(j, carry, cur, diagonal=False, prefetch=True):
        new = []
        for h in range(nh):
            m, acc = carry[2 * h:2 * h + 2]
            if prefetch:
                s_ref[(1 - cur) * nh + h] = scores(j + 1, h)
            s = s_ref[cur * nh + h]
            if diagonal:
                key = lax.broadcasted_iota(jnp.int32, (bq, bq), 0)
                qry = lax.broadcasted_iota(jnp.int32, (bq, bq), 1)
                s = jnp.where(key <= qry, s, -jnp.inf)
            new += list(accumulate(j, h, s, m, acc))
        return tuple(new)

    def pair(i, carry):
        return step(2 * i + 1, step(2 * i, carry, 0), 1)

    init = []
    for h in range(nh):
        s_ref[h] = scores(0, h)
        init += [jnp.full((1, bq), -jnp.inf, F32), jnp.zeros((VT_ROWS, bq), F32)]
    n_pairs = lax.shift_right_logical(qi, 1)
    n_quads = lax.shift_right_logical(qi, 2)
    n_octs = lax.shift_right_logical(qi, 3)
    quad = lambda i, cr: pair(2 * i + 1, pair(2 * i, cr))
    carry = lax.fori_loop(0, n_octs, lambda i, cr: quad(2 * i + 1, quad(2 * i, cr)), tuple(init))
    carry = lax.fori_loop(2 * n_octs, n_quads, quad, carry)
    carry = lax.fori_loop(2 * n_quads, n_pairs, pair, carry)
    carry = lax.cond(
        qi % 2 == 1,
        lambda cr: step(qi, step(qi - 1, cr, 0), 1, diagonal=True, prefetch=False),
        lambda cr: step(qi, cr, 0, diagonal=True, prefetch=False),
        carry)
    outs = []
    for h in range(nh):
        acc = carry[2 * h + 1]
        o = acc[:MLA_V_DIM, :] / acc[MLA_V_DIM:MLA_V_DIM + 1, :]
        o = o * lax.rsqrt(jnp.mean(o * o, axis=0, keepdims=True) + RMS_EPS)
        outs.append(o.T)
    o_ref[...] = (jnp.concatenate(outs, axis=1) * og_ref[...]).astype(BF16)


def _attention(q, k, vt, og, *, bq, heads_per_step=2):
    S = q.shape[0]
    n_hp = MLA_HEADS // heads_per_step
    wq = heads_per_step * HEAD_PAD
    wo = heads_per_step * MLA_V_DIM
    return pl.pallas_call(
        functools.partial(_attn_kernel, bq=bq, heads_per_step=heads_per_step),
        grid=(n_hp, S // bq),
        in_specs=[pl.BlockSpec((bq, wq), lambda h, i: (i, h)),
                  pl.BlockSpec((S, wq), lambda h, i: (0, h)),
                  pl.BlockSpec((wq, S), lambda h, i: (h, 0)),
                  pl.BlockSpec((1, wo), lambda h, i: (0, h))],
        out_specs=pl.BlockSpec((bq, wo), lambda h, i: (i, h)),
        out_shape=jax.ShapeDtypeStruct((S, MLA_HEADS * MLA_V_DIM), BF16),
        scratch_shapes=[pltpu.VMEM((2 * heads_per_step, bq, bq), F32)],
        compiler_params=_cparams(("arbitrary", "arbitrary")),
        name="attention",
    )(q, k, vt, og)


def _mlstm_kernel(q_ref, k_ref, v_ref, o_ref, gc_ref, gr_ref, og_ref, h_out, c_ref, m_ref, *, L):
    ci = pl.program_id(0)

    @pl.when(ci == 0)
    def _():
        c_ref[...] = jnp.zeros_like(c_ref)
        m_ref[...] = jnp.zeros_like(m_ref)

    r = lax.broadcasted_iota(jnp.int32, (L, L), 0)
    c = lax.broadcasted_iota(jnp.int32, (L, L), 1)
    tri = c <= r
    lane = lax.broadcasted_iota(jnp.int32, (L, LANES), 1)
    ones_blk = jnp.where(lane == 0, 1.0, 0.0).astype(BF16)
    gc = gc_ref[...]
    gr = gr_ref[...]
    d = MLSTM_HEAD_DIM
    for h in range(MLSTM_HEADS):
        q = q_ref[:, h * d:(h + 1) * d]
        k = k_ref[:, h * d:(h + 1) * d]
        v = v_ref[:, h * d:(h + 1) * d]
        li_col = gc[:, h:h + 1]
        lf_col = gc[:, MLSTM_HEADS + h:MLSTM_HEADS + h + 1]
        li_row = gr[h:h + 1, :]
        lf_row = gr[MLSTM_HEADS + h:MLSTM_HEADS + h + 1, :]
        b_col = jnp.sum(jnp.where(tri, lf_row, 0.0), axis=1, keepdims=True)
        b_row = jnp.sum(jnp.where(r <= c, lf_col, 0.0), axis=0, keepdims=True)
        m_prev = m_ref[h][0:1, 0:1]
        log_d = jnp.where(tri, b_col - b_row + li_row, -jnp.inf)
        log_inter = b_col + m_prev
        m_t = jnp.maximum(log_inter, jnp.max(log_d, axis=1, keepdims=True))
        w_intra = jnp.exp(log_d - m_t)
        w_inter = jnp.exp(log_inter - m_t)
        s = lax.dot_general(q, k, (((1,), (1,)), ((), ())), preferred_element_type=F32) * w_intra
        vaug = jnp.concatenate([v, ones_blk], axis=1)
        r1 = jnp.dot(s.astype(BF16), vaug, preferred_element_type=F32)
        cst = c_ref[h]
        r2 = jnp.dot(q, cst.astype(BF16), preferred_element_type=F32)
        num = r1[:, :d] + w_inter * r2[:, :d]
        den = r1[:, d:d + 1] + w_inter * r2[:, d:d + 1]
        hv = num / jnp.maximum(jnp.abs(den), jnp.exp(-m_t))
        b_end = b_col[L - 1:L, :]
        log_w = b_end - b_col + li_col
        m_new = jnp.maximum(b_end + m_prev, jnp.max(log_w, axis=0, keepdims=True))
        w_s = jnp.exp(log_w - m_new)
        decay = jnp.exp(b_end + m_prev - m_new)
        kw = (k.astype(F32) * w_s).astype(BF16)
        upd = lax.dot_general(kw, vaug, (((0,), (0,)), ((), ())), preferred_element_type=F32)
        c_ref[h] = decay * cst + upd
        m_ref[h] = jnp.broadcast_to(m_new, (SUBLANES, LANES))
        hn = hv * lax.rsqrt(jnp.mean(hv * hv, axis=-1, keepdims=True) + RMS_EPS)
        gate = o_ref[:, h * d:(h + 1) * d].astype(F32)
        h_out[:, h * d:(h + 1) * d] = (hn * gate * og_ref[:, h * d:(h + 1) * d]).astype(BF16)


def _mlstm(mq, mk, mv, mo, gcol, grow, og, *, L):
    S = mq.shape[0]
    d = MLSTM_HEAD_DIM
    row = pl.BlockSpec((L, D_MLSTM), lambda i: (i, 0))
    return pl.pallas_call(
        functools.partial(_mlstm_kernel, L=L),
        grid=(S // L,),
        in_specs=[row, row, row, row,
                  pl.BlockSpec((L, 2 * MLSTM_HEADS), lambda i: (i, 0)),
                  pl.BlockSpec((2 * MLSTM_HEADS, L), lambda i: (0, i)),
                  pl.BlockSpec((1, D_MLSTM), lambda i: (0, 0))],
        out_specs=row,
        out_shape=jax.ShapeDtypeStruct((S, D_MLSTM), BF16),
        scratch_shapes=[pltpu.VMEM((MLSTM_HEADS, d, 2 * d), F32),
                        pltpu.VMEM((MLSTM_HEADS, SUBLANES, LANES), F32)],
        compiler_params=_cparams(("arbitrary",)),
        name="mlstm",
    )(mq, mk, mv, mo, gcol, grow, og)


def _layer_norm(z, g, b):
    mu = jnp.mean(z, axis=-1, keepdims=True)
    zc = z - mu
    var = jnp.mean(zc * zc, axis=-1, keepdims=True)
    return zc * lax.rsqrt(var + LN_EPS) * g + b


_R_GROUP0 = N_EXPERTS
_META_W, _META_E = 0, TOP_K


def _first_lane(mask, lane):
    return jnp.min(jnp.where(mask, lane, LANES), axis=-1, keepdims=True)


def _route_tokens(lg):
    lane = lax.broadcasted_iota(jnp.int32, lg.shape, 1)
    is_g = (lane >= _R_GROUP0) & (lane < _R_GROUP0 + N_GROUPS)
    gl = jnp.where(is_g, lg, -jnp.inf)
    ge = jnp.exp(gl - jnp.max(gl, axis=-1, keepdims=True))
    gp = ge / jnp.sum(ge, axis=-1, keepdims=True)
    g_p = jnp.max(gp, axis=-1, keepdims=True)
    g_top = _first_lane((gp == g_p) & is_g, lane) - _R_GROUP0
    in_grp = (lane < N_EXPERTS) & (lax.shift_right_logical(lane, 3) == g_top)
    el = jnp.where(in_grp, lg, -jnp.inf)
    ee = jnp.exp(el - jnp.max(el, axis=-1, keepdims=True))
    p = ee / jnp.sum(ee, axis=-1, keepdims=True)
    p1 = jnp.max(p, axis=-1, keepdims=True)
    i1 = _first_lane((p == p1) & in_grp, lane)
    pm = jnp.where(in_grp & (lane != i1), p, -jnp.inf)
    p2 = jnp.max(pm, axis=-1, keepdims=True)
    i2 = _first_lane(pm == p2, lane)
    den = p1 + p2
    w1 = g_p * (p1 / den)
    w2 = g_p * (p2 / den)
    meta = jnp.where(lane == _META_W, w1, jnp.where(lane == _META_W + 1, w2, 0.0))
    meta = jnp.where(lane == _META_E, i1.astype(F32), jnp.where(lane == _META_E + 1, i2.astype(F32), meta))
    chosen = ((lane == i1) | (lane == i2)).astype(BF16)
    return meta, chosen


def _outproj_kernel(x_ref, a_ref, h_ref, wo_ref, g_ref, b_ref, wr_ref, x1t_out, meta_out, oh_out, *,
                    alpha, tm):
    mix = jnp.concatenate([a_ref[...], h_ref[...]], axis=1)
    y = jnp.dot(mix, wo_ref[...], preferred_element_type=F32)
    x1 = _layer_norm(alpha * x_ref[...] + y, g_ref[...], b_ref[...])
    for c in range(SUBLANES):
        x1t_out[pl.ds(c, tm, stride=SUBLANES), :] = x1[:, c * LANES:(c + 1) * LANES]
    x_hi = x1.astype(BF16)
    x_lo = (x1 - x_hi.astype(F32)).astype(BF16)
    t = jnp.dot(x_hi, wr_ref[...], preferred_element_type=F32)
    lg = (t[:, :LANES] + t[:, LANES:]) + jnp.dot(x_lo, wr_ref[:, :LANES], preferred_element_type=F32)
    meta_out[...], oh_out[...] = _route_tokens(lg)


def _outproj(x, a, hm, wo, g, b, wr, *, tm, alpha):
    S = x.shape[0]
    row = lambda w: pl.BlockSpec((tm, w), lambda i: (i, 0))
    full = lambda arr: pl.BlockSpec(arr.shape, lambda i: (0,) * arr.ndim)
    return pl.pallas_call(
        functools.partial(_outproj_kernel, alpha=alpha, tm=tm),
        grid=(S // tm,),
        in_specs=[row(D_MODEL), row(MLA_HEADS * MLA_V_DIM), row(D_MLSTM), full(wo), full(g), full(b),
                  full(wr)],
        out_specs=(pl.BlockSpec((tm * SUBLANES, LANES), lambda i: (i, 0)), row(LANES), row(LANES)),
        out_shape=(jax.ShapeDtypeStruct((S * SUBLANES, LANES), F32),
                   jax.ShapeDtypeStruct((S, LANES), F32),
                   jax.ShapeDtypeStruct((S, LANES), BF16)),
        compiler_params=_cparams(("arbitrary",)),
        name="outproj",
    )(x, a, hm, wo, g, b, wr)


def _rank_kernel(oh_ref, meta_ref, rank_out, cnt_out, base_ref, *, tr):
    i = pl.program_id(0)

    @pl.when(i == 0)
    def _():
        base_ref[...] = jnp.zeros_like(base_ref)

    oh = oh_ref[...]
    r = lax.broadcasted_iota(jnp.int32, (tr, tr), 0)
    c = lax.broadcasted_iota(jnp.int32, (tr, tr), 1)
    before = (c < r).astype(BF16)
    seen = base_ref[0:1, :] + jnp.dot(before, oh, preferred_element_type=F32)
    lane = lax.broadcasted_iota(jnp.int32, (tr, LANES), 1)
    meta = meta_ref[...]
    out = jnp.zeros((tr, LANES), F32)
    for k in range(TOP_K):
        e_k = meta[:, _META_E + k:_META_E + k + 1].astype(jnp.int32)
        rk = jnp.sum(jnp.where(lane == e_k, seen, 0.0), axis=-1, keepdims=True)
        out = jnp.where(lane == k, rk, out)
    rank_out[...] = out
    total = base_ref[0:1, :] + jnp.sum(oh.astype(F32), axis=0, keepdims=True)
    base_ref[...] = jnp.broadcast_to(total, base_ref.shape)
    cnt_out[...] = jnp.broadcast_to(total, cnt_out.shape)


def _rank(oh, meta, *, tr):
    S = oh.shape[0]
    row = pl.BlockSpec((tr, LANES), lambda i: (i, 0))
    return pl.pallas_call(
        functools.partial(_rank_kernel, tr=tr),
        grid=(S // tr,),
        in_specs=[row, row],
        out_specs=(row, pl.BlockSpec((SUBLANES, LANES), lambda i: (0, 0))),
        out_shape=(jax.ShapeDtypeStruct((S, LANES), F32), jax.ShapeDtypeStruct((SUBLANES, LANES), F32)),
        scratch_shapes=[pltpu.VMEM((SUBLANES, LANES), F32)],
        compiler_params=_cparams(("arbitrary",)),
        name="rank",
    )(oh, meta)


TOKEN_TILE = SUBLANES
DMA_GROUP = 8


def _rows_from_tiles(ref, n):
    return jnp.concatenate([ref[pl.ds(c, n, stride=TOKEN_TILE), :] for c in range(TOKEN_TILE)], axis=1)


def _rows_to_tiles(ref, v):
    for c in range(TOKEN_TILE):
        ref[pl.ds(c, v.shape[0], stride=TOKEN_TILE), :] = v[:, c * LANES:(c + 1) * LANES]


def _pow2_chunks(n, top, fn):
    off = 0
    bit = top
    while bit:
        @pl.when((n & bit) != 0)
        def _(bit=bit, off=off):
            fn(off, bit)
        off = off + (n & bit)
        bit //= 2


def _dispatch_kernel(cnt_ref, pst_ref, d_ref, x_ref, xs_hbm, xbuf, zbuf, sem, zsem, *, tt):
    i = pl.program_id(0)
    n = pl.num_programs(0)
    G = TOKEN_TILE
    slot = lax.rem(i, 2)

    def row_copy(src_row, dst_row, rows, sl):
        return pltpu.make_async_copy(xbuf.at[sl, pl.ds(src_row * G, rows * G)],
                                     xs_hbm.at[pl.ds(dst_row * G, rows * G)], sem.at[sl])

    def zero_copy(dst_row, rows):
        return pltpu.make_async_copy(zbuf.at[pl.ds(0, rows * G)], xs_hbm.at[pl.ds(dst_row * G, rows * G)], zsem)

    @pl.when(i == 0)
    def _():
        zbuf[...] = jnp.zeros_like(zbuf)

        def fill(e, wait):
            pad = (-cnt_ref[e]) & (EXPERT_BLOCK - 1)
            first = pst_ref[e] + cnt_ref[e]
            if wait:
                _pow2_chunks(pad, EXPERT_BLOCK // 2, lambda off, size: zero_copy(0, size).wait())
            else:
                _pow2_chunks(pad, EXPERT_BLOCK // 2, lambda off, size: zero_copy(first + off, size).start())
            return 0
        lax.fori_loop(0, N_EXPERTS, lambda e, c: fill(e, False), 0)
        lax.fori_loop(0, N_EXPERTS, lambda e, c: fill(e, True), 0)
        half = EXPERT_BLOCK // 2
        used = (pst_ref[N_EXPERTS - 1] + cnt_ref[N_EXPERTS - 1] + EXPERT_BLOCK - 1) // EXPERT_BLOCK * 2
        n_half = xs_hbm.shape[0] // (half * G)

        def tail(c, wait):
            if wait:
                zero_copy(0, half).wait()
            else:
                zero_copy(c * half, half).start()
            return 0
        lax.fori_loop(used, n_half, lambda c, _: tail(c, False), 0)
        lax.fori_loop(used, n_half, lambda c, _: tail(c, True), 0)

    def wait_step(sl):
        for _ in range(TOP_K):
            row_copy(0, 0, tt, sl).wait()

    for sl in range(2):
        @pl.when(slot == sl)
        def _(sl=sl):
            @pl.when(i >= 2)
            def _():
                wait_step(sl)
            xbuf[sl] = x_ref[...]
            for t0 in range(0, tt, DMA_GROUP):
                idx = [d_ref[0, 0, j] for j in range(TOP_K * t0, TOP_K * (t0 + DMA_GROUP))]
                for j, dst in enumerate(idx):
                    row_copy(t0 + j // TOP_K, dst, 1, sl).start(priority=j % 2)

    @pl.when(i == n - 1)
    def _():
        @pl.when(n >= 2)
        def _():
            wait_step(1 - slot)
        wait_step(slot)


def _dispatch(x1t, dest, counts, pad_starts, n_slots, *, tt):
    G = TOKEN_TILE
    T = x1t.shape[0] // G
    nt = T // tt
    grid_spec = pltpu.PrefetchScalarGridSpec(
        num_scalar_prefetch=2,
        grid=(nt,),
        in_specs=[pl.BlockSpec((1, 1, TOP_K * tt), lambda i, c, p: (i, 0, 0), memory_space=pltpu.SMEM),
                  pl.BlockSpec((tt * G, LANES), lambda i, c, p: (i, 0))],
        out_specs=pl.BlockSpec(memory_space=pl.ANY),
        scratch_shapes=[pltpu.VMEM((2, tt * G, LANES), F32),
                        pltpu.VMEM((EXPERT_BLOCK // 2 * G, LANES), F32),
                        pltpu.SemaphoreType.DMA((2,)), pltpu.SemaphoreType.DMA])
    return pl.pallas_call(
        functools.partial(_dispatch_kernel, tt=tt),
        grid_spec=grid_spec,
        out_shape=jax.ShapeDtypeStruct((n_slots * G, LANES), F32),
        compiler_params=_cparams(("arbitrary",)),
        name="dispatch",
    )(counts, pad_starts, dest.reshape(nt, 1, TOP_K * tt), x1t)


X_RING = 3


def _moe_kernel(bexp_ref, nvb_ref, xs_hbm, wg_ref, wu_ref, wd_ref, ys_ref, xbuf, wgb, wub, wdb, xsem):
    b = pl.program_id(0)
    nvb = nvb_ref[0]
    rows = EXPERT_BLOCK * TOKEN_TILE

    def block_copy(j, sl):
        return pltpu.make_async_copy(xs_hbm.at[pl.ds(j * rows, rows)], xbuf.at[sl], xsem.at[sl])

    @pl.when(b == 0)
    def _():
        for j in range(X_RING - 1):
            @pl.when(j < nvb)
            def _(j=j):
                block_copy(j, j).start()

    ahead = b + (X_RING - 1)

    @pl.when(ahead < nvb)
    def _():
        block_copy(ahead, lax.rem(ahead, X_RING)).start()

    @pl.when((b == 0) | (bexp_ref[b] != bexp_ref[jnp.maximum(b - 1, 0)]))
    def _():
        wgb[...] = wg_ref[0, 0].astype(BF16)
        wub[...] = wu_ref[0, 0].astype(BF16)
        wdb[...] = wd_ref[0, 0].astype(BF16)

    @pl.when(b < nvb)
    def _():
        sl = lax.rem(b, X_RING)
        block_copy(b, sl).wait()
        xs = _rows_from_tiles(xbuf.at[sl], EXPERT_BLOCK).astype(BF16)
        g = jnp.dot(xs, wgb[...], preferred_element_type=F32)
        u = jnp.dot(xs, wub[...], preferred_element_type=F32)
        hid = (g * (1.0 / (1.0 + jnp.exp(-g)))) * u
        _rows_to_tiles(ys_ref, jnp.dot(hid.astype(BF16), wdb[...], preferred_element_type=F32))

    @pl.when(b >= nvb_ref[0])
    def _():
        ys_ref[...] = jnp.zeros_like(ys_ref)


def _moe(xs, block_expert, n_valid_blocks, wg, wu, wd, layer):
    G = TOKEN_TILE
    assert G * LANES == D_MODEL
    R = EXPERT_BLOCK
    nb = xs.shape[0] // (R * G)
    blk = lambda b, be, nvb: (b, 0)
    wspec = lambda shp: pl.BlockSpec((1, 1) + shp, lambda b, be, nvb: (layer, be[b], 0, 0))
    grid_spec = pltpu.PrefetchScalarGridSpec(
        num_scalar_prefetch=2,
        grid=(nb,),
        in_specs=[pl.BlockSpec(memory_space=pl.ANY),
                  wspec((D_MODEL, D_EXPERT)), wspec((D_MODEL, D_EXPERT)), wspec((D_EXPERT, D_MODEL))],
        out_specs=pl.BlockSpec((R * G, LANES), blk),
        scratch_shapes=[pltpu.VMEM((X_RING, R * G, LANES), F32),
                        pltpu.VMEM((D_MODEL, D_EXPERT), BF16), pltpu.VMEM((D_MODEL, D_EXPERT), BF16),
                        pltpu.VMEM((D_EXPERT, D_MODEL), BF16), pltpu.SemaphoreType.DMA((X_RING,))])
    return pl.pallas_call(
        _moe_kernel,
        grid_spec=grid_spec,
        out_shape=jax.ShapeDtypeStruct(xs.shape, F32),
        compiler_params=_cparams(("arbitrary",)),
        name="experts",
    )(block_expert, n_valid_blocks, xs, wg, wu, wd)


def _combine_kernel(d_ref, dn_ref, x_ref, w_ref, g_ref, b_ref, ys_hbm, o_ref, ybuf, sem, *, alpha, tt):
    i = pl.program_id(0)
    n = pl.num_programs(0)
    G = TOKEN_TILE
    slot = lax.rem(i, 2)

    def row_copy(src_row, r, rows, sl):
        return pltpu.make_async_copy(ys_hbm.at[pl.ds(src_row * G, rows * G)],
                                     ybuf.at[sl, pl.ds(r * G, rows * G)], sem.at[sl])

    def fetch(idx_ref, sl):
        for t0 in range(0, tt, DMA_GROUP):
            idx = [idx_ref[0, 0, j] for j in range(TOP_K * t0, TOP_K * (t0 + DMA_GROUP))]
            for j, src in enumerate(idx):
                row_copy(src, (j % TOP_K) * tt + t0 + j // TOP_K, 1, sl).start(priority=j % 2)

    @pl.when(i == 0)
    def _():
        fetch(d_ref, 0)

    for sl in range(2):
        @pl.when(slot == sl)
        def _(sl=sl):
            row_copy(0, 0, TOP_K * tt, sl).wait()
            fetch(dn_ref, 1 - sl)
            w = w_ref[...]
            y = (_rows_from_tiles(ybuf.at[sl, pl.ds(0, tt * G)], tt) * w[:, 0:1]
                 + _rows_from_tiles(ybuf.at[sl, pl.ds(tt * G, tt * G)], tt) * w[:, 1:2])
            o_ref[...] = _layer_norm(alpha * _rows_from_tiles(x_ref, tt) + y, g_ref[...], b_ref[...])

    @pl.when(i == n - 1)
    def _():
        row_copy(0, 0, TOP_K * tt, 1 - slot).wait()


def _combine(x1t, ys, dest, w, g, b, *, tt, alpha):
    G = TOKEN_TILE
    S = x1t.shape[0] // G
    nt = S // tt
    d3 = dest.reshape(nt, 1, TOP_K * tt)
    smem_blk = lambda f: pl.BlockSpec((1, 1, TOP_K * tt), f, memory_space=pltpu.SMEM)
    return pl.pallas_call(
        functools.partial(_combine_kernel, alpha=alpha, tt=tt),
        grid=(nt,),
        in_specs=[smem_blk(lambda i: (i, 0, 0)),
                  smem_blk(lambda i: (jnp.minimum(i + 1, nt - 1), 0, 0)),
                  pl.BlockSpec((tt * G, LANES), lambda i: (i, 0)),
                  pl.BlockSpec((tt, TOP_K), lambda i: (i, 0)),
                  pl.BlockSpec((1, D_MODEL), lambda i: (0, 0)),
                  pl.BlockSpec((1, D_MODEL), lambda i: (0, 0)),
                  pl.BlockSpec(memory_space=pl.ANY)],
        out_specs=pl.BlockSpec((tt, D_MODEL), lambda i: (i, 0)),
        out_shape=jax.ShapeDtypeStruct((S, D_MODEL), F32),
        scratch_shapes=[pltpu.VMEM((2, TOP_K * tt * G, LANES), F32), pltpu.SemaphoreType.DMA((2,))],
        compiler_params=_cparams(("arbitrary",)),
        name="combine",
    )(d3, d3, x1t, w, g, b, ys)


def _route(meta, rank, cnt, T):
    weights = meta[:, _META_W:_META_W + TOP_K]
    flat_e = meta[:, _META_E:_META_E + TOP_K].astype(jnp.int32).reshape(-1)
    n_assign = T * TOP_K
    counts = cnt[0, :N_EXPERTS].astype(jnp.int32)
    padded = (counts + EXPERT_BLOCK - 1) // EXPERT_BLOCK * EXPERT_BLOCK
    pad_ends = jnp.cumsum(padded)
    pad_starts = pad_ends - padded
    dest = pad_starts[flat_e] + rank[:, :TOP_K].astype(jnp.int32).reshape(-1)
    n_blocks = n_assign // EXPERT_BLOCK + N_EXPERTS
    block_start = jnp.arange(n_blocks, dtype=jnp.int32) * EXPERT_BLOCK
    block_expert = jnp.minimum(jnp.sum((pad_ends[None, :] <= block_start[:, None]).astype(jnp.int32), axis=1),
                               N_EXPERTS - 1).astype(jnp.int32)
    n_valid_blocks = (pad_ends[-1:] // EXPERT_BLOCK).astype(jnp.int32)
    return (dest, counts.astype(jnp.int32), pad_starts.astype(jnp.int32), block_expert, n_valid_blocks,
            n_blocks * EXPERT_BLOCK, weights)


def _in_weight(w_in):
    kr0 = MLA_Q_RANK + MLA_KV_RANK
    mq0 = kr0 + MLA_ROPE_DIM
    i0 = mq0 + 4 * D_MLSTM
    half = MLA_ROPE_DIM // 2
    n_small = 2 * MLA_ROPE_DIM + 2 * MLSTM_HEADS
    pad = jnp.zeros(w_in.shape[:-1] + (LANES - n_small,), w_in.dtype)
    return jnp.concatenate(
        [w_in[..., :kr0], w_in[..., mq0:i0], w_in[..., kr0:mq0], w_in[..., kr0 + half:mq0],
         w_in[..., kr0:kr0 + half], w_in[..., i0:i0 + 2 * MLSTM_HEADS], pad], axis=-1).astype(BF16)


def _q_weight(w_qb):
    lead = w_qb.shape[:-1]
    w = w_qb.reshape(lead + (MLA_HEADS, MLA_QK_DIM))
    half = MLA_ROPE_DIM // 2
    z = lambda n: jnp.zeros(lead + (MLA_HEADS, n), w_qb.dtype)
    main = jnp.concatenate([w, z(HEAD_PAD - MLA_QK_DIM)], axis=-1)
    swp = jnp.concatenate([z(MLA_NOPE_DIM), w[..., MLA_NOPE_DIM + half:], w[..., MLA_NOPE_DIM:MLA_NOPE_DIM + half],
                           z(HEAD_PAD - MLA_QK_DIM)], axis=-1)
    hw = MLA_HEADS * HEAD_PAD
    return jnp.concatenate([main.reshape(lead + (hw,)), swp.reshape(lead + (hw,))], axis=-1).astype(BF16)


def _kv_weight(w_kvb_l):
    hw = MLA_HEADS * HEAD_PAD
    w = w_kvb_l.reshape(MLA_KV_RANK, MLA_HEADS, MLA_NOPE_DIM + MLA_V_DIM)
    top_k = jnp.pad(w[:, :, :MLA_NOPE_DIM], ((0, 0), (0, 0), (0, HEAD_PAD - MLA_NOPE_DIM)))
    top_v = jnp.pad(w[:, :, MLA_NOPE_DIM:], ((0, 0), (0, 0), (0, HEAD_PAD - MLA_V_DIM)))
    bot_k = np.zeros((LANES, MLA_HEADS, HEAD_PAD), np.float32)
    bot_v = np.zeros((LANES, MLA_HEADS, HEAD_PAD), np.float32)
    for r in range(MLA_ROPE_DIM):
        bot_k[r, :, MLA_NOPE_DIM + r] = 1.0
    bot_v[_S_ONE, :, MLA_V_DIM] = 1.0
    wk = jnp.concatenate([top_k.reshape(MLA_KV_RANK, hw), jnp.asarray(bot_k).reshape(LANES, hw)], 0)
    wv = jnp.concatenate([top_v.reshape(MLA_KV_RANK, hw), jnp.asarray(bot_v).reshape(LANES, hw)], 0)
    return jnp.concatenate([wk, wv], axis=1).astype(BF16)


def _rope_tables(positions):
    half = MLA_ROPE_DIM // 2
    inv_freq = ROPE_THETA ** (-jnp.arange(half, dtype=F32) / half)
    ang = positions.astype(F32)[:, None] * inv_freq
    cos, sin = jnp.cos(ang), jnp.sin(ang)
    ct = jnp.concatenate([cos, cos], -1)
    st = jnp.concatenate([-sin, sin], -1)
    S = positions.shape[0]
    z = lambda w: jnp.zeros((S, w), F32)
    scale = MLA_QK_DIM ** -0.5 * float(np.log2(np.e))
    tabk = jnp.concatenate([ct, st, z(LANES - 2 * MLA_ROPE_DIM)], -1)
    cqt = jnp.concatenate([jnp.full((S, MLA_NOPE_DIM), scale, F32), ct * scale,
                           z(HEAD_PAD - MLA_QK_DIM)], -1)
    sqt = jnp.concatenate([z(MLA_NOPE_DIM), st * scale, z(HEAD_PAD - MLA_QK_DIM)], -1)
    return tabk, cqt, sqt


def _pick(S, pref):
    t = pref
    while S % t:
        t //= 2
    return t


def kernel(x, positions, w_in, conv_w, conv_b, gate_b, q_a_g, kv_a_g, w_qb, w_kvb, out_g, w_out, ln1_g, ln1_b, w_rg, w_re, w_gate, w_up, w_down, ln2_g, ln2_b):
    B, S, D = x.shape
    assert B == 1 and D == D_MODEL
    depth = w_in.shape[0]
    alpha = float((2 * depth) ** 0.25)
    tm = _pick(S, 512)
    bq = _pick(S, 512)
    L = _pick(S, 256)
    tt = _pick(S, 512)
    assert tm % SUBLANES == 0 and bq % LANES == 0 and L % LANES == 0

    tabk, cqt, sqt = _rope_tables(positions[0])
    win_all = _in_weight(w_in)
    wq_all = _q_weight(w_qb)
    gb_tab = jnp.zeros((depth, 1, LANES), F32).at[:, 0, _S_I0:_S_I0 + 2 * MLSTM_HEADS].set(gate_b)
    wr_f32 = jnp.concatenate([w_re, w_rg, jnp.zeros((depth, D_MODEL, LANES - N_GROUPS - N_EXPERTS), F32)], -1)
    wr_hi = wr_f32.astype(BF16)
    wr_all = jnp.concatenate([wr_hi, (wr_f32 - wr_hi.astype(F32)).astype(BF16)], -1)
    wo_all = w_out.astype(BF16)
    n_att = MLA_HEADS * MLA_V_DIM

    xs = x[0]
    for l in range(depth):
        q, k, v, mq, mk, mv, mo, gates = _proj(
            xs, win_all[l], wq_all[l], _kv_weight(w_kvb[l]), q_a_g[l][None], kv_a_g[l][None],
            conv_w[l], conv_b[l][None], gb_tab[l], tabk, cqt, sqt, tm=tm)
        a = _attention(q, k, v, out_g[l][None, :n_att], bq=bq)
        gcol = gates[:, _S_I0:_S_I0 + 2 * MLSTM_HEADS]
        hm = _mlstm(mq, mk, mv, mo, gcol, gcol.T, out_g[l][None, n_att:], L=L)
        x1t, meta, chosen = _outproj(xs, a, hm, wo_all[l], ln1_g[l][None], ln1_b[l][None], wr_all[l],
                                     tm=tm, alpha=alpha)
        rank, cnt = _rank(chosen, meta, tr=tm)
        dest, counts, pad_starts, bexp, nvb, n_slots, rw = _route(meta, rank, cnt, S)
        xsort = _dispatch(x1t, dest, counts, pad_starts, n_slots, tt=tt)
        ysort = _moe(xsort, bexp, nvb, w_gate, w_up, w_down, l)
        xs = _combine(x1t, ysort, dest, rw, ln2_g[l][None], ln2_b[l][None], tt=tt, alpha=alpha)
    return xs[None]
```
